```python
import math
import jax, jax.numpy as jnp
from jax import lax
import numpy as np

D_MODEL = 1024
BATCH = 16
SEQ = 2048
DEPTH = 1
DEC_BATCH = 2
DEC_SEQ = 16384
PAST_LEN = 128

D_FF = 2816
PLE_DIM = 256
HA = 8
Q_LORA = 384
KV_LORA = 256
NOPE_DIM = 64
ROPE_DIM = 32
V_DIM = 64
ROPE_THETA = 10000.0
Q_BLOCK = 128
HB = 8
KVH = 2
REP = HB // KVH
HD = 64
WINDOW = 128
W_BLOCK = 128
REL_BUCKETS = 32
REL_MAX_DIST = 128
IN_SPLITS = (Q_LORA, KV_LORA, ROPE_DIM, HB * HD, KVH * HD, KVH * HD, D_MODEL, D_MODEL)
IN_COLS = sum(IN_SPLITS)
EPS = 1e-6
NEG = -1e30

kernel_name = "hybrid_mla_window_gqa_encoder"


def rmsnorm(x, g):
    xf = x.astype(jnp.float32)
    y = xf * lax.rsqrt(jnp.mean(xf * xf, axis=-1, keepdims=True) + EPS)
    return (y * g.astype(jnp.float32)).astype(x.dtype)


def swiglu(x, w1, w3, w2):
    return (jax.nn.silu(x @ w1) * (x @ w3)) @ w2


def rope_tables(S):
    inv = 1.0 / (ROPE_THETA ** (jnp.arange(0, ROPE_DIM, 2, dtype=jnp.float32) / ROPE_DIM))
    ang = jnp.arange(S, dtype=jnp.float32)[:, None] * inv[None, :]
    return jnp.cos(ang), jnp.sin(ang)


def apply_rope(x, cos, sin):
    x1, x2 = jnp.split(x, 2, axis=-1)
    shape = (1, cos.shape[0]) + (1,) * (x.ndim - 3) + (cos.shape[1],)
    c = cos.reshape(shape).astype(x.dtype)
    s = sin.reshape(shape).astype(x.dtype)
    return jnp.concatenate([x1 * c - x2 * s, x1 * s + x2 * c], axis=-1)


def t5_bucket(rel):
    nb = REL_BUCKETS // 2
    max_exact = nb // 2
    ret = jnp.where(rel > 0, nb, 0)
    n = jnp.abs(rel)
    nf = jnp.maximum(n, 1).astype(jnp.float32)
    large = max_exact + (jnp.log(nf / max_exact) / math.log(REL_MAX_DIST / max_exact)
                         * (nb - max_exact)).astype(jnp.int32)
    large = jnp.minimum(large, nb - 1)
    return ret + jnp.where(n < max_exact, n, large)


def mla_attention(c_q, c_kv, k_rope_raw, q_norm_g, kv_norm_g, w_uq, w_uk, w_uv):
    B, S, _ = c_q.shape
    cq = rmsnorm(c_q, q_norm_g)
    ckv = rmsnorm(c_kv, kv_norm_g)
    q = (cq @ w_uq).reshape(B, S, HA, NOPE_DIM + ROPE_DIM)
    cos, sin = rope_tables(S)
    q_nope = q[..., :NOPE_DIM]
    q_rope = apply_rope(q[..., NOPE_DIM:], cos, sin)
    k_rope = apply_rope(k_rope_raw, cos, sin)
    k_nope = (ckv @ w_uk).reshape(B, S, HA, NOPE_DIM)
    v = (ckv @ w_uv).reshape(B, S, HA, V_DIM)
    scale = (NOPE_DIM + ROPE_DIM) ** -0.5
    nq = S // Q_BLOCK
    qn = q_nope.reshape(B, nq, Q_BLOCK, HA, NOPE_DIM).transpose(1, 0, 2, 3, 4)
    qr = q_rope.reshape(B, nq, Q_BLOCK, HA, ROPE_DIM).transpose(1, 0, 2, 3, 4)

    def block(args):
        qn_b, qr_b = args
        s = (jnp.einsum('bqhd,bkhd->bhqk', qn_b, k_nope)
             + jnp.einsum('bqhr,bkr->bhqk', qr_b, k_rope))
        p = jax.nn.softmax(s.astype(jnp.float32) * scale, axis=-1).astype(v.dtype)
        return jnp.einsum('bhqk,bkhd->bqhd', p, v)

    o = lax.map(block, (qn, qr))
    return o.transpose(1, 0, 2, 3, 4).reshape(B, S, HA * V_DIM)


def window_attention(q, k, v, sink, rel_bias):
    B, S, _ = q.shape
    nb = S // W_BLOCK
    q = q.reshape(B, nb, W_BLOCK, KVH, REP, HD)
    pad = ((0, 0), (W_BLOCK, W_BLOCK), (0, 0), (0, 0))
    kp = jnp.pad(k.reshape(B, S, KVH, HD), pad)
    vp = jnp.pad(v.reshape(B, S, KVH, HD), pad)

    def bands(t):
        t = t.reshape(B, nb + 2, W_BLOCK, KVH, HD)
        return jnp.concatenate([t[:, :-2], t[:, 1:-1], t[:, 2:]], axis=2)

    kb = bands(kp)
    vb = bands(vp)
    s = jnp.einsum('bnqgrd,bnkgd->bngrqk', q, kb).astype(jnp.float32) * (HD ** -0.5)
    rel = (jnp.arange(3 * W_BLOCK, dtype=jnp.int32)[None, :] - W_BLOCK
           - jnp.arange(W_BLOCK, dtype=jnp.int32)[:, None])
    kpos = (jnp.arange(nb, dtype=jnp.int32)[:, None] * W_BLOCK - W_BLOCK
            + jnp.arange(3 * W_BLOCK, dtype=jnp.int32)[None, :])
    valid = (jnp.abs(rel) <= WINDOW)[None] & ((kpos >= 0) & (kpos < S))[:, None, :]
    bias = rel_bias[t5_bucket(rel)].astype(jnp.float32)
    bias = bias.transpose(2, 0, 1).reshape(KVH, REP, W_BLOCK, 3 * W_BLOCK)
    s = jnp.where(valid[None, :, None, None], s + bias[None, None], NEG)
    sink_col = jnp.broadcast_to(sink.astype(jnp.float32).reshape(KVH, REP, 1, 1),
                                s.shape[:-1] + (1,))
    p = jax.nn.softmax(jnp.concatenate([s, sink_col], axis=-1), axis=-1)[..., :-1]
    o = jnp.einsum('bngrqk,bnkgd->bnqgrd', p.astype(vb.dtype), vb)
    return o.reshape(B, S, HB * HD)


def encoder_layer(x, p, rel_bias,
                  ffn1_pre_g, ffn1_post_g, ffn1_w1, ffn1_w3, ffn1_w2,
                  mix_pre_g, mix_post_g, w_in, q_norm_g, kv_norm_g, w_uq, w_uk, w_uv,
                  sink, w_proj_a, w_proj_b, w_out,
                  ffn2_pre_g, ffn2_post_g, ffn2_w1, ffn2_w3, ffn2_w2,
                  ple_pre_g, ple_post_g, w_ple_gate, w_ple_proj):
    h = x + 0.5 * rmsnorm(swiglu(rmsnorm(x, ffn1_pre_g), ffn1_w1, ffn1_w3, ffn1_w2), ffn1_post_g)
    u = rmsnorm(h, mix_pre_g)
    z = u @ w_in
    idx = list(np.cumsum(IN_SPLITS)[:-1])
    c_q, c_kv, k_rope, q_b, k_b, v_b, g_a, g_b = jnp.split(z, idx, axis=-1)
    y_a = mla_attention(c_q, c_kv, k_rope, q_norm_g, kv_norm_g, w_uq, w_uk, w_uv)
    y_b = window_attention(q_b, k_b, v_b, sink, rel_bias)
    m = jax.nn.sigmoid(g_a) * (y_a @ w_proj_a) + jax.nn.sigmoid(g_b) * (y_b @ w_proj_b)
    h = h + rmsnorm(m @ w_out, mix_post_g)
    h = h + 0.5 * rmsnorm(swiglu(rmsnorm(h, ffn2_pre_g), ffn2_w1, ffn2_w3, ffn2_w2), ffn2_post_g)
    e = (p @ w_ple_proj) * jax.nn.sigmoid(rmsnorm(h, ple_pre_g) @ w_ple_gate)
    return h + rmsnorm(e, ple_post_g)


def setup_inputs(seed: int = 0) -> dict:
    key = jax.random.key(seed)
    ks = iter(jax.random.split(key, 64))
    f32 = jnp.float32

    def w(shape, fan_in):
        return jax.random.normal(next(ks), shape, f32) * (fan_in ** -0.5)

    def g(n):
        return 1.0 + 0.05 * jax.random.normal(next(ks), (DEPTH, n), f32)

    L = DEPTH
    return {
        "x_prompt": jax.random.normal(next(ks), (BATCH, SEQ, D_MODEL), f32),
        "x_sample": jax.random.normal(next(ks), (DEC_BATCH, DEC_SEQ, D_MODEL), f32),
        "p_prompt": jax.random.normal(next(ks), (DEPTH, BATCH, SEQ, PLE_DIM), f32),
        "p_sample": jax.random.normal(next(ks), (DEPTH, DEC_BATCH, DEC_SEQ, PLE_DIM), f32),
        "rel_bias": 0.5 * jax.random.normal(next(ks), (REL_BUCKETS, HB), f32),
        "ffn1_pre_g": g(D_MODEL),
        "ffn1_post_g": g(D_MODEL),
        "ffn1_w1": w((L, D_MODEL, D_FF), D_MODEL),
        "ffn1_w3": w((L, D_MODEL, D_FF), D_MODEL),
        "ffn1_w2": w((L, D_FF, D_MODEL), D_FF),
        "mix_pre_g": g(D_MODEL),
        "mix_post_g": g(D_MODEL),
        "w_in": w((L, D_MODEL, IN_COLS), D_MODEL),
        "q_norm_g": g(Q_LORA),
        "kv_norm_g": g(KV_LORA),
        "w_uq": w((L, Q_LORA, HA * (NOPE_DIM + ROPE_DIM)), Q_LORA),
        "w_uk": w((L, KV_LORA, HA * NOPE_DIM), KV_LORA),
        "w_uv": w((L, KV_LORA, HA * V_DIM), KV_LORA),
        "sink": 0.5 * jax.random.normal(next(ks), (L, HB), f32),
        "w_proj_a": w((L, HA * V_DIM, D_MODEL), HA * V_DIM),
        "w_proj_b": w((L, HB * HD, D_MODEL), HB * HD),
        "w_out": w((L, D_MODEL, D_MODEL), D_MODEL),
        "ffn2_pre_g": g(D_MODEL),
        "ffn2_post_g": g(D_MODEL),
        "ffn2_w1": w((L, D_MODEL, D_FF), D_MODEL),
        "ffn2_w3": w((L, D_MODEL, D_FF), D_MODEL),
        "ffn2_w2": w((L, D_FF, D_MODEL), D_FF),
        "ple_pre_g": g(D_MODEL),
        "ple_post_g": g(D_MODEL),
        "w_ple_gate": w((L, D_MODEL, D_MODEL), D_MODEL),
        "w_ple_proj": w((L, PLE_DIM, D_MODEL), PLE_DIM),
    }


def reference(x_prompt, x_sample, p_prompt, p_sample, rel_bias,
              ffn1_pre_g, ffn1_post_g, ffn1_w1, ffn1_w3, ffn1_w2,
              mix_pre_g, mix_post_g, w_in, q_norm_g, kv_norm_g, w_uq, w_uk, w_uv,
              sink, w_proj_a, w_proj_b, w_out,
              ffn2_pre_g, ffn2_post_g, ffn2_w1, ffn2_w3, ffn2_w2,
              ple_pre_g, ple_post_g, w_ple_gate, w_ple_proj):
    stacked = (ffn1_pre_g, ffn1_post_g, ffn1_w1, ffn1_w3, ffn1_w2,
               mix_pre_g, mix_post_g, w_in, q_norm_g, kv_norm_g, w_uq, w_uk, w_uv,
               sink, w_proj_a, w_proj_b, w_out,
               ffn2_pre_g, ffn2_post_g, ffn2_w1, ffn2_w3, ffn2_w2,
               ple_pre_g, ple_post_g, w_ple_gate, w_ple_proj)
    y_prompt = x_prompt
    y_sample = x_sample
    for i in range(DEPTH):
        lw = [t[i] for t in stacked]
        y_prompt = encoder_layer(y_prompt, p_prompt[i], rel_bias, *lw)
        y_sample = encoder_layer(y_sample, p_sample[i], rel_bias, *lw)
    return (y_prompt, y_sample)
```

```python
import functools
import math

import jax
import jax.numpy as jnp
from jax import lax
from jax.experimental import pallas as pl
from jax.experimental.pallas import tpu as pltpu

D_MODEL = 1024
D_FF = 2816
PLE_DIM = 256
HA = 8
Q_LORA = 384
KV_LORA = 256
NOPE_DIM = 64
ROPE_DIM = 32
V_DIM = 64
ROPE_THETA = 10000.0
HB = 8
KVH = 2
REP = HB // KVH
HD = 64
WINDOW = 128
REL_BUCKETS = 32
REL_MAX_DIST = 128
EPS = 1e-6
NEG = -1e30

LANES = 128
HALF = LANES // 2
VMEM_LIMIT = 56 * 1024 * 1024

TOK_TILE = 512
FF_CHUNK = 256
MLA_Q_TILE = 512
MLA_SUB = 4
WIN_Q_TILE = 2 * WINDOW

F32 = jnp.float32
BF16 = jnp.bfloat16

_C_CQ = 0
_C_CKV = _C_CQ + Q_LORA
_C_KR = _C_CKV + KV_LORA
_C_QB = _C_KR + 2 * LANES
_C_KB = _C_QB + HB * LANES
_C_VBE = _C_KB + KVH * LANES
_C_VBO = _C_VBE + KVH * LANES
_C_GA = _C_VBO + KVH * LANES
_C_GB = _C_GA + D_MODEL
_C_END = _C_GB + D_MODEL


def _params(n_axes):
    return pltpu.CompilerParams(dimension_semantics=("arbitrary",) * n_axes,
                                vmem_limit_bytes=VMEM_LIMIT)


def _const_spec(shape):
    nd = len(shape)
    return pl.BlockSpec(shape, lambda *_: (0,) * nd)


def _rms(x, g):
    return x * lax.rsqrt(jnp.mean(x * x, axis=-1, keepdims=True) + EPS) * g


def _dot(a, b):
    return jnp.dot(a, b, preferred_element_type=F32)


def _swiglu(xn, w1_ref, w3_ref, w2_ref):
    acc = None
    for c in range(D_FF // FF_CHUNK):
        sl = slice(c * FF_CHUNK, (c + 1) * FF_CHUNK)
        a = _dot(xn, w1_ref[:, sl])
        b = _dot(xn, w3_ref[:, sl])
        act = (a * jax.nn.sigmoid(a) * b).astype(BF16)
        d = _dot(act, w2_ref[sl, :])
        acc = d if acc is None else acc + d
    return acc


def _ffn_kernel(x_ref, gpre_ref, gpost_ref, w1_ref, w3_ref, w2_ref, o_ref):
    x = x_ref[...]
    xn = _rms(x, gpre_ref[...]).astype(BF16)
    f = _swiglu(xn, w1_ref, w3_ref, w2_ref)
    o_ref[...] = x + 0.5 * _rms(f, gpost_ref[...])


def _ffn(x, gpre, gpost, w1, w3, w2):
    m = x.shape[0]
    tok = pl.BlockSpec((TOK_TILE, D_MODEL), lambda i: (i, 0))
    return pl.pallas_call(
        _ffn_kernel,
        grid=(m // TOK_TILE,),
        in_specs=[tok, _const_spec((1, D_MODEL)), _const_spec((1, D_MODEL)),
                  _const_spec(w1.shape), _const_spec(w3.shape), _const_spec(w2.shape)],
        out_specs=tok,
        out_shape=jax.ShapeDtypeStruct((m, D_MODEL), F32),
        compiler_params=_params(1),
        name="ffn",
    )(x, gpre, gpost, w1, w3, w2)


def _inproj_kernel(h_ref, g_ref, cos_ref, sin_ref, win_ref, qng_ref, kvng_ref,
                   wq_ref, wkt_ref, wv_ref,
                   qa_ref, kat_ref, va_ref, qb_ref, kbt_ref, vbe_ref, vbo_ref,
                   ga_ref, gb_ref):
    u = _rms(h_ref[...], g_ref[...]).astype(BF16)
    cos_t = cos_ref[...]
    sin_t = sin_ref[...]
    lane = lax.broadcasted_iota(jnp.int32, (TOK_TILE, LANES), 1)

    def proj(lo, hi):
        return _dot(u, win_ref[:, lo:hi])

    zq = proj(_C_QB, _C_KB)
    for hb in range(HB):
        qb_ref[hb] = zq[:, hb * LANES:(hb + 1) * LANES].astype(BF16)
    kbt = proj(_C_KB, _C_VBE).T
    zve = proj(_C_VBE, _C_VBO)
    zvo = proj(_C_VBO, _C_GA)
    for g in range(KVH):
        kbt_ref[g] = kbt[g * LANES:(g + 1) * LANES].astype(BF16)
        vbe_ref[g] = jnp.where(lane == HALF, 1.0, zve[:, g * LANES:(g + 1) * LANES]).astype(BF16)
        vbo_ref[g] = jnp.where(lane == 0, 1.0, zvo[:, g * LANES:(g + 1) * LANES]).astype(BF16)
    ga_ref[...] = jax.nn.sigmoid(proj(_C_GA, _C_GB)).astype(BF16)
    gb_ref[...] = jax.nn.sigmoid(proj(_C_GB, _C_END)).astype(BF16)

    cq = _rms(proj(_C_CQ, _C_CKV), qng_ref[...]).astype(BF16)
    ckv = _rms(proj(_C_CKV, _C_KR), kvng_ref[...])
    zkr = proj(_C_KR, _C_QB)
    kr = zkr[:, :LANES] * cos_t + zkr[:, LANES:] * sin_t
    krt = kr.T
    ckvt = ckv.T.astype(BF16)
    ckv = ckv.astype(BF16)
    kt_all = _dot(wkt_ref[...], ckvt)
    v_all = _dot(ckv, wv_ref[...])
    scale = (NOPE_DIM + ROPE_DIM) ** -0.5
    for h in range(HA):
        qq = _dot(cq, wq_ref[h])
        q = (qq[:, :LANES] * cos_t + qq[:, LANES:] * sin_t) * scale
        qa_ref[h] = q.astype(BF16)
        kat_ref[h, 0] = (kt_all[h * LANES:(h + 1) * LANES] + krt).astype(BF16)
        one_lane = HALF if h % 2 == 0 else 0
        va_ref[h] = jnp.where(lane == one_lane, 1.0,
                              v_all[:, h * LANES:(h + 1) * LANES]).astype(BF16)


def _inproj(h, g, cos_t, sin_t, win, qng, kvng, wq, wkt, wv, seq):
    m = h.shape[0]
    nt = m // TOK_TILE
    per_seq = seq // TOK_TILE
    tok = lambda w: pl.BlockSpec((TOK_TILE, w), lambda i: (i, 0))
    tab = pl.BlockSpec((TOK_TILE, LANES), lambda i: (i % per_seq, 0))
    heads = lambda n: pl.BlockSpec((n, TOK_TILE, LANES), lambda i: (0, i, 0))
    out_shape = (
        jax.ShapeDtypeStruct((HA, m, LANES), BF16),
        jax.ShapeDtypeStruct((HA, nt, LANES, TOK_TILE), BF16),
        jax.ShapeDtypeStruct((HA, m, LANES), BF16),
        jax.ShapeDtypeStruct((HB, m, LANES), BF16),
        jax.ShapeDtypeStruct((KVH, LANES, m), BF16),
        jax.ShapeDtypeStruct((KVH, m, LANES), BF16),
        jax.ShapeDtypeStruct((KVH, m, LANES), BF16),
        jax.ShapeDtypeStruct((m, D_MODEL), BF16),
        jax.ShapeDtypeStruct((m, D_MODEL), BF16),
    )
    out_specs = (
        heads(HA),
        pl.BlockSpec((HA, 1, LANES, TOK_TILE), lambda i: (0, i, 0, 0)),
        heads(HA),
        heads(HB),
        pl.BlockSpec((KVH, LANES, TOK_TILE), lambda i: (0, 0, i)),
        heads(KVH),
        heads(KVH),
        tok(D_MODEL),
        tok(D_MODEL),
    )
    return pl.pallas_call(
        _inproj_kernel,
        grid=(nt,),
        in_specs=[tok(D_MODEL), _const_spec((1, D_MODEL)), tab, tab, _const_spec(win.shape),
                  _const_spec((1, Q_LORA)), _const_spec((1, KV_LORA)),
                  _const_spec(wq.shape), _const_spec(wkt.shape), _const_spec(wv.shape)],
        out_specs=out_specs,
        out_shape=out_shape,
        compiler_params=_params(1),
        name="inproj",
    )(h, g, cos_t, sin_t, win, qng, kvng, wq, wkt, wv)


def _mla_kernel(q_ref, kt_ref, v_ref, o_ref, *, n_groups):
    lane = lax.broadcasted_iota(jnp.int32, (MLA_Q_TILE, LANES), 1)
    outs = []
    for j in range(2):
        q = q_ref[j]

        def step(gi, carry, j=j, q=q):
            m_old, acc = carry
            s = [_dot(q, kt_ref[j, gi * MLA_SUB + u]) for u in range(MLA_SUB)]
            m_cur = functools.reduce(jnp.maximum, [jnp.max(x, axis=-1, keepdims=True) for x in s])
            m_new = jnp.maximum(m_old, m_cur)
            pv = None
            for u in range(MLA_SUB):
                p = jnp.exp(s[u] - m_new).astype(BF16)
                start = pl.multiple_of((gi * MLA_SUB + u) * TOK_TILE, TOK_TILE)
                d = _dot(p, v_ref[j, pl.ds(start, TOK_TILE), :])
                pv = d if pv is None else pv + d
            return m_new, jnp.exp(m_old - m_new) * acc + pv

        init = (jnp.full((MLA_Q_TILE, 1), -jnp.inf, F32), jnp.zeros((MLA_Q_TILE, LANES), F32))
        if n_groups == 1:
            _, acc = step(0, init)
        else:
            _, acc = lax.fori_loop(0, n_groups, step, init)
        denom_lane = HALF if j == 0 else 0
        outs.append(acc / acc[:, denom_lane:denom_lane + 1])
    o_ref[...] = jnp.where(lane < HALF, outs[0], outs[1]).astype(BF16)


def _mla(qa, kat, va, batch, seq):
    m = qa.shape[1]
    nq = seq // MLA_Q_TILE
    nkc = seq // TOK_TILE
    n_groups = nkc // MLA_SUB
    return pl.pallas_call(
        functools.partial(_mla_kernel, n_groups=n_groups),
        grid=(batch, HA // 2, nq),
        in_specs=[
            pl.BlockSpec((2, MLA_Q_TILE, LANES), lambda b, hp, i: (hp, b * nq + i, 0)),
            pl.BlockSpec((2, nkc, LANES, TOK_TILE), lambda b, hp, i: (hp, b, 0, 0)),
            pl.BlockSpec((2, seq, LANES), lambda b, hp, i: (hp, b, 0)),
        ],
        out_specs=pl.BlockSpec((MLA_Q_TILE, LANES), lambda b, hp, i: (b * nq + i, hp)),
        out_shape=jax.ShapeDtypeStruct((m, HA * V_DIM), BF16),
        compiler_params=_params(3),
        name="mla",
    )(qa, kat, va)


def _bias_kernel(bucket_ref, relb_ref, o_ref):
    bucket = bucket_ref[...]
    for h in range(HB):
        acc = jnp.full(bucket.shape, NEG, F32)
        for b in range(REL_BUCKETS):
            acc = jnp.where(bucket == b, relb_ref[b * HB + h], acc)
        o_ref[h] = acc


def _bias_table(rel_bias):
    r = jnp.arange(WIN_Q_TILE, dtype=jnp.int32)[:, None]
    c = jnp.arange(2 * WIN_Q_TILE, dtype=jnp.int32)[None, :]
    rel = c - WINDOW - r
    nb = REL_BUCKETS // 2
    max_exact = nb // 2
    ret = jnp.where(rel > 0, nb, 0)
    n = jnp.abs(rel)
    nf = jnp.maximum(n, 1).astype(F32)
    large = max_exact + (jnp.log(nf / max_exact) / math.log(REL_MAX_DIST / max_exact)
                         * (nb - max_exact)).astype(jnp.int32)
    large = jnp.minimum(large, nb - 1)
    bucket = ret + jnp.where(n < max_exact, n, large)
    bucket = jnp.where(n <= WINDOW, bucket, -1)
    return pl.pallas_call(
        _bias_kernel,
        in_specs=[pl.BlockSpec(memory_space=pltpu.VMEM), pl.BlockSpec(memory_space=pltpu.SMEM)],
        out_specs=pl.BlockSpec(memory_space=pltpu.VMEM),
        out_shape=jax.ShapeDtypeStruct((HB, WIN_Q_TILE, 2 * WIN_Q_TILE), F32),
        name="rel_bias_table",
    )(bucket, rel_bias.reshape(-1))


def _win_kernel(sink_ref, q_ref, bias_ref, ktp_ref, ktc_ref, ktn_ref,
                vep_ref, vec_ref, ven_ref, vop_ref, voc_ref, von_ref, o_ref, *, n_tiles):
    g = pl.program_id(0)
    i = pl.program_id(2)
    tq = WIN_Q_TILE
    q = q_ref[...].reshape(REP * tq, LANES)
    s = jnp.concatenate([_dot(q, ktp_ref[0][:, tq - WINDOW:]),
                         _dot(q, ktc_ref[0]),
                         _dot(q, ktn_ref[0][:, :WINDOW])], axis=1)
    s = s + bias_ref[0]
    col = lax.broadcasted_iota(jnp.int32, s.shape, 1)
    valid = jnp.logical_and(jnp.logical_or(i > 0, col >= WINDOW),
                            jnp.logical_or(i < n_tiles - 1, col < WINDOW + tq))
    s = jnp.where(valid, s, NEG)
    sink = jnp.concatenate([jnp.full((tq, 1), sink_ref[g * REP + r], F32) for r in range(REP)],
                           axis=0)
    m = jnp.maximum(jnp.max(s, axis=-1, keepdims=True), sink)
    p = jnp.exp(s - m).astype(BF16)
    sink_p = jnp.exp(sink - m)
    v_even = jnp.concatenate([vep_ref[0][tq - WINDOW:], vec_ref[0], ven_ref[0][:WINDOW]], axis=0)
    v_odd = jnp.concatenate([vop_ref[0][tq - WINDOW:], voc_ref[0], von_ref[0][:WINDOW]], axis=0)
    lane = lax.broadcasted_iota(jnp.int32, (tq, LANES), 1)
    outs = []
    for r in range(REP):
        rows = slice(r * tq, (r + 1) * tq)
        pv = _dot(p[rows], v_even if r % 2 == 0 else v_odd)
        denom_lane = HALF if r % 2 == 0 else 0
        outs.append(pv / (pv[:, denom_lane:denom_lane + 1] + sink_p[rows]))
    o_ref[...] = jnp.concatenate([jnp.where(lane < HALF, outs[0], outs[1]),
                                  jnp.where(lane < HALF, outs[2], outs[3])], axis=1).astype(BF16)


def _win(sink, qb, bias, kbt, vbe, vbo, batch, seq):
    m = qb.shape[1]
    tq = WIN_Q_TILE
    nt = seq // tq

    def kt_spec(off):
        return pl.BlockSpec((1, LANES, tq),
                            lambda g, b, i: (g, 0, b * nt + jnp.clip(i + off, 0, nt - 1)))

    def v_spec(off):
        return pl.BlockSpec((1, tq, LANES),
                            lambda g, b, i: (g, b * nt + jnp.clip(i + off, 0, nt - 1), 0))

    return pl.pallas_call(
        functools.partial(_win_kernel, n_tiles=nt),
        grid=(KVH, batch, nt),
        in_specs=[
            pl.BlockSpec(memory_space=pltpu.SMEM),
            pl.BlockSpec((REP, tq, LANES), lambda g, b, i: (g, b * nt + i, 0)),
            pl.BlockSpec((1, REP * tq, 2 * tq), lambda g, b, i: (g, 0, 0)),
            kt_spec(-1), kt_spec(0), kt_spec(1),
            v_spec(-1), v_spec(0), v_spec(1),
            v_spec(-1), v_spec(0), v_spec(1),
        ],
        out_specs=pl.BlockSpec((tq, REP * HD), lambda g, b, i: (b * nt + i, g)),
        out_shape=jax.ShapeDtypeStruct((m, HB * HD), BF16),
        compiler_params=_params(3),
        name="window",
    )(sink, qb, bias, kbt, kbt, kbt, vbe, vbe, vbe, vbo, vbo, vbo)


def _mix_kernel(h_ref, ya_ref, yb_ref, ga_ref, gb_ref, pa_ref, pb_ref, wo_ref, g_ref, o_ref):
    ma = _dot(ya_ref[...], pa_ref[...])
    mb = _dot(yb_ref[...], pb_ref[...])
    mix = (ga_ref[...].astype(F32) * ma + gb_ref[...].astype(F32) * mb).astype(BF16)
    o_ref[...] = h_ref[...] + _rms(_dot(mix, wo_ref[...]), g_ref[...])


def _mix(h, ya, yb, ga, gb, pa, pb, wo, g):
    m = h.shape[0]
    tok = lambda w: pl.BlockSpec((TOK_TILE, w), lambda i: (i, 0))
    return pl.pallas_call(
        _mix_kernel,
        grid=(m // TOK_TILE,),
        in_specs=[tok(D_MODEL), tok(HA * V_DIM), tok(HB * HD), tok(D_MODEL), tok(D_MODEL),
                  _const_spec(pa.shape), _const_spec(pb.shape), _const_spec(wo.shape),
                  _const_spec((1, D_MODEL))],
        out_specs=tok(D_MODEL),
        out_shape=jax.ShapeDtypeStruct((m, D_MODEL), F32),
        compiler_params=_params(1),
        name="mix",
    )(h, ya, yb, ga, gb, pa, pb, wo, g)


def _ple_kernel(h_ref, p_ref, gpre_ref, gpost_ref, wg_ref, wp_ref, o_ref):
    h = h_ref[...]
    gate = jax.nn.sigmoid(_dot(_rms(h, gpre_ref[...]).astype(BF16), wg_ref[...]))
    e = _dot(p_ref[...].astype(BF16), wp_ref[...]) * gate
    o_ref[...] = h + _rms(e, gpost_ref[...])


def _ple(h, p, gpre, gpost, wg, wp):
    m = h.shape[0]
    tok = lambda w: pl.BlockSpec((TOK_TILE, w), lambda i: (i, 0))
    return pl.pallas_call(
        _ple_kernel,
        grid=(m // TOK_TILE,),
        in_specs=[tok(D_MODEL), tok(PLE_DIM), _const_spec((1, D_MODEL)), _const_spec((1, D_MODEL)),
                  _const_spec(wg.shape), _const_spec(wp.shape)],
        out_specs=tok(D_MODEL),
        out_shape=jax.ShapeDtypeStruct((m, D_MODEL), F32),
        compiler_params=_params(1),
        name="ple",
    )(h, p, gpre, gpost, wg, wp)


def _pad_cols(w, width):
    return jnp.pad(w, ((0, 0), (0, width - w.shape[1])))


def _prep_weights(w_in, w_uq, w_uk, w_uv):
    splits = (Q_LORA, KV_LORA, ROPE_DIM, HB * HD, KVH * HD, KVH * HD, D_MODEL, D_MODEL)
    offs = [0]
    for s in splits:
        offs.append(offs[-1] + s)
    w_cq, w_ckv, w_kr, w_qb, w_kb, w_vb, w_ga, w_gb = (
        w_in[:, offs[k]:offs[k + 1]] for k in range(len(splits)))
    r2 = ROPE_DIM // 2
    zeros = lambda n: jnp.zeros((w_in.shape[0], n), w_in.dtype)
    kr_placed = jnp.concatenate([zeros(NOPE_DIM), w_kr, zeros(LANES - NOPE_DIM - ROPE_DIM)], axis=1)
    kr_rot = jnp.concatenate([zeros(NOPE_DIM), -w_kr[:, r2:], w_kr[:, :r2],
                              zeros(LANES - NOPE_DIM - ROPE_DIM)], axis=1)
    qb = jnp.concatenate([_pad_cols(w_qb[:, h * HD:(h + 1) * HD] * (HD ** -0.5), LANES)
                          for h in range(HB)], axis=1)
    kb = jnp.concatenate([_pad_cols(w_kb[:, g * HD:(g + 1) * HD], LANES) for g in range(KVH)], axis=1)
    vbe = jnp.concatenate([_pad_cols(w_vb[:, g * HD:(g + 1) * HD], LANES) for g in range(KVH)], axis=1)
    vbo = jnp.concatenate([jnp.concatenate([zeros(HALF), w_vb[:, g * HD:(g + 1) * HD]], axis=1)
                           for g in range(KVH)], axis=1)
    win = jnp.concatenate([w_cq, w_ckv, kr_placed, kr_rot, qb, kb, vbe, vbo, w_ga, w_gb],
                          axis=1).astype(BF16)

    qd = NOPE_DIM + ROPE_DIM
    zq = lambda n: jnp.zeros((Q_LORA, n), w_uq.dtype)
    wq = []
    for h in range(HA):
        wh = w_uq[:, h * qd:(h + 1) * qd]
        nope, x1, x2 = wh[:, :NOPE_DIM], wh[:, NOPE_DIM:NOPE_DIM + r2], wh[:, NOPE_DIM + r2:]
        wq.append(jnp.concatenate([nope, x1, x2, zq(LANES - qd),
                                   zq(NOPE_DIM), -x2, x1, zq(LANES - qd)], axis=1))
    wq = jnp.stack(wq).astype(BF16)
    wkt = jnp.concatenate([_pad_cols(w_uk[:, h * NOPE_DIM:(h + 1) * NOPE_DIM], LANES)
                           for h in range(HA)], axis=1).T.astype(BF16)
    zv = jnp.zeros((KV_LORA, HALF), w_uv.dtype)
    wv = jnp.concatenate(
        [jnp.concatenate([w_uv[:, h * V_DIM:(h + 1) * V_DIM], zv] if h % 2 == 0
                         else [zv, w_uv[:, h * V_DIM:(h + 1) * V_DIM]], axis=1)
         for h in range(HA)], axis=1).astype(BF16)
    return win, wq, wkt, wv


def _rope_tables(seq):
    inv = 1.0 / (ROPE_THETA ** (jnp.arange(0, ROPE_DIM, 2, dtype=F32) / ROPE_DIM))
    ang = jnp.arange(seq, dtype=F32)[:, None] * inv[None, :]
    cos, sin = jnp.cos(ang), jnp.sin(ang)
    pad = jnp.zeros((seq, LANES - NOPE_DIM - ROPE_DIM), F32)
    cos_t = jnp.concatenate([jnp.ones((seq, NOPE_DIM), F32), cos, cos, pad], axis=1)
    sin_t = jnp.concatenate([jnp.zeros((seq, NOPE_DIM), F32), sin, sin, pad], axis=1)
    return cos_t, sin_t


def _encoder_layer(x, p, bias, sink, wts):
    batch, seq, _ = x.shape
    m = batch * seq
    x2 = x.reshape(m, D_MODEL)
    p2 = p.reshape(m, PLE_DIM)
    cos_t, sin_t = _rope_tables(seq)
    h = _ffn(x2, wts["ffn1_pre_g"], wts["ffn1_post_g"], wts["ffn1_w1"], wts["ffn1_w3"], wts["ffn1_w2"])
    qa, kat, va, qb, kbt, vbe, vbo, ga, gb = _inproj(
        h, wts["mix_pre_g"], cos_t, sin_t, wts["win"], wts["q_norm_g"], wts["kv_norm_g"],
        wts["wq"], wts["wkt"], wts["wv"], seq)
    ya = _mla(qa, kat, va, batch, seq)
    yb = _win(sink, qb, bias, kbt, vbe, vbo, batch, seq)
    h = _mix(h, ya, yb, ga, gb, wts["w_proj_a"], wts["w_proj_b"], wts["w_out"], wts["mix_post_g"])
    h = _ffn(h, wts["ffn2_pre_g"], wts["ffn2_post_g"], wts["ffn2_w1"], wts["ffn2_w3"], wts["ffn2_w2"])
    out = _ple(h, p2, wts["ple_pre_g"], wts["ple_post_g"], wts["w_ple_gate"], wts["w_ple_proj"])
    return out.reshape(batch, seq, D_MODEL)


def kernel(x_prompt, x_sample, p_prompt, p_sample, rel_bias, ffn1_pre_g, ffn1_post_g, ffn1_w1, ffn1_w3, ffn1_w2, mix_pre_g, mix_post_g, w_in, q_norm_g, kv_norm_g, w_uq, w_uk, w_uv, sink, w_proj_a, w_proj_b, w_out, ffn2_pre_g, ffn2_post_g, ffn2_w1, ffn2_w3, ffn2_w2, ple_pre_g, ple_post_g, w_ple_gate, w_ple_proj):
    depth = ffn1_w1.shape[0]
    bias_all = _bias_table(rel_bias)
    bias = bias_all.reshape(KVH, REP * WIN_Q_TILE, 2 * WIN_Q_TILE)
    y_prompt, y_sample = x_prompt, x_sample
    for layer in range(depth):
        win, wq, wkt, wv = _prep_weights(w_in[layer], w_uq[layer], w_uk[layer], w_uv[layer])
        wts = {
            "win": win, "wq": wq, "wkt": wkt, "wv": wv,
            "q_norm_g": q_norm_g[layer][None], "kv_norm_g": kv_norm_g[layer][None],
        }
        for name, arr in (("ffn1_pre_g", ffn1_pre_g), ("ffn1_post_g", ffn1_post_g),
                          ("mix_pre_g", mix_pre_g), ("mix_post_g", mix_post_g),
                          ("ffn2_pre_g", ffn2_pre_g), ("ffn2_post_g", ffn2_post_g),
                          ("ple_pre_g", ple_pre_g), ("ple_post_g", ple_post_g)):
            wts[name] = arr[layer][None]
        for name, arr in (("ffn1_w1", ffn1_w1), ("ffn1_w3", ffn1_w3), ("ffn1_w2", ffn1_w2),
                          ("w_proj_a", w_proj_a), ("w_proj_b", w_proj_b), ("w_out", w_out),
                          ("ffn2_w1", ffn2_w1), ("ffn2_w3", ffn2_w3), ("ffn2_w2", ffn2_w2),
                          ("w_ple_gate", w_ple_gate), ("w_ple_proj", w_ple_proj)):
            wts[name] = arr[layer].astype(BF16)
        y_prompt = _encoder_layer(y_prompt, p_prompt[layer], bias, sink[layer], wts)
        y_sample = _encoder_layer(y_sample, p_sample[layer], bias, sink[layer], wts)
    return (y_prompt, y_sample)
```

```python
import functools
import math

import jax
import jax.numpy as jnp
from jax import lax
from jax.experimental import pallas as pl
from jax.experimental.pallas import tpu as pltpu

D_MODEL = 1024
D_FF = 2816
PLE_DIM = 256
HA = 8
Q_LORA = 384
KV_LORA = 256
NOPE_DIM = 64
ROPE_DIM = 32
V_DIM = 64
ROPE_THETA = 10000.0
HB = 8
KVH = 2
REP = HB // KVH
HD = 64
WINDOW = 128
REL_BUCKETS = 32
REL_MAX_DIST = 128
EPS = 1e-6
NEG = -1e30

LANES = 128
HALF = LANES // 2
VMEM_LIMIT = 56 * 1024 * 1024

TOK_TILE = 512
FF_CHUNK = 256
MLA_Q_TILE = TOK_TILE
MLA_KEY_GROUP = 2048
LOG2E = math.log2(math.e)
WIN_Q_TILE = 2 * WINDOW

F32 = jnp.float32
BF16 = jnp.bfloat16

_C_CQ = 0
_C_CKV = _C_CQ + Q_LORA
_C_KR = _C_CKV + KV_LORA
_C_QB = _C_KR + 2 * LANES
_C_KB = _C_QB + HB * LANES
_C_VBE = _C_KB + KVH * LANES
_C_VBO = _C_VBE + KVH * LANES
_C_GA = _C_VBO + KVH * LANES
_C_GB = _C_GA + D_MODEL
_C_END = _C_GB + D_MODEL


def _params(n_axes):
    return pltpu.CompilerParams(dimension_semantics=("arbitrary",) * n_axes,
                                vmem_limit_bytes=VMEM_LIMIT)


def _const_spec(shape):
    nd = len(shape)
    return pl.BlockSpec(shape, lambda *_: (0,) * nd)


def _rms(x, g):
    return x * lax.rsqrt(jnp.mean(x * x, axis=-1, keepdims=True) + EPS) * g


def _dot(a, b):
    return jnp.dot(a, b, preferred_element_type=F32)


def _swiglu(xn, w1_ref, w3_ref, w2_ref):
    acc = None
    for c in range(D_FF // FF_CHUNK):
        sl = slice(c * FF_CHUNK, (c + 1) * FF_CHUNK)
        a = _dot(xn, w1_ref[:, sl])
        b = _dot(xn, w3_ref[:, sl])
        act = (a * jax.nn.sigmoid(a) * b).astype(BF16)
        d = _dot(act, w2_ref[sl, :])
        acc = d if acc is None else acc + d
    return acc


def _ffn_kernel(x_ref, gpre_ref, gpost_ref, w1_ref, w3_ref, w2_ref, o_ref):
    x = x_ref[...]
    xn = _rms(x, gpre_ref[...]).astype(BF16)
    f = _swiglu(xn, w1_ref, w3_ref, w2_ref)
    o_ref[...] = x + 0.5 * _rms(f, gpost_ref[...])


def _ffn(x, gpre, gpost, w1, w3, w2):
    m = x.shape[0]
    tok = pl.BlockSpec((TOK_TILE, D_MODEL), lambda i: (i, 0))
    return pl.pallas_call(
        _ffn_kernel,
        grid=(m // TOK_TILE,),
        in_specs=[tok, _const_spec((1, D_MODEL)), _const_spec((1, D_MODEL)),
                  _const_spec(w1.shape), _const_spec(w3.shape), _const_spec(w2.shape)],
        out_specs=tok,
        out_shape=jax.ShapeDtypeStruct((m, D_MODEL), F32),
        compiler_params=_params(1),
        name="ffn",
    )(x, gpre, gpost, w1, w3, w2)


def _inproj_kernel(h_ref, g_ref, cos_ref, sin_ref, cost_ref, sint_ref, win_ref, qng_ref, kvng_ref,
                   wqt_ref, wk_ref, wvt_ref,
                   qat_ref, ka_ref, vat_ref, qb_ref, kbt_ref, vbe_ref, vbo_ref,
                   ga_ref, gb_ref):
    u = _rms(h_ref[...], g_ref[...]).astype(BF16)
    lane = lax.broadcasted_iota(jnp.int32, (TOK_TILE, LANES), 1)
    row = lax.broadcasted_iota(jnp.int32, (LANES, TOK_TILE), 0)

    def proj(lo, hi):
        return _dot(u, win_ref[:, lo:hi])

    zq = proj(_C_QB, _C_KB)
    for hb in range(HB):
        qb_ref[hb] = zq[:, hb * LANES:(hb + 1) * LANES].astype(BF16)
    kbt = proj(_C_KB, _C_VBE).T
    zve = proj(_C_VBE, _C_VBO)
    zvo = proj(_C_VBO, _C_GA)
    for g in range(KVH):
        kbt_ref[g] = kbt[g * LANES:(g + 1) * LANES].astype(BF16)
        vbe_ref[g] = jnp.where(lane == HALF, 1.0, zve[:, g * LANES:(g + 1) * LANES]).astype(BF16)
        vbo_ref[g] = jnp.where(lane == 0, 1.0, zvo[:, g * LANES:(g + 1) * LANES]).astype(BF16)
    ga_ref[...] = jax.nn.sigmoid(proj(_C_GA, _C_GB)).astype(BF16)
    gb_ref[...] = jax.nn.sigmoid(proj(_C_GB, _C_END)).astype(BF16)

    cqt = _rms(proj(_C_CQ, _C_CKV), qng_ref[...]).T.astype(BF16)
    ckv = _rms(proj(_C_CKV, _C_KR), kvng_ref[...])
    ckvt = ckv.T.astype(BF16)
    zkr = proj(_C_KR, _C_QB)
    kr = zkr[:, :LANES] * cos_ref[...] + zkr[:, LANES:] * sin_ref[...]
    k_all = _dot(ckv.astype(BF16), wk_ref[...])
    vt_all = _dot(wvt_ref[...], ckvt)
    cos_tt = cost_ref[...]
    sin_tt = sint_ref[...]
    qscale = (NOPE_DIM + ROPE_DIM) ** -0.5 * LOG2E
    for h in range(HA):
        qqt = _dot(wqt_ref[h], cqt)
        qt = (qqt[:LANES] * cos_tt + qqt[LANES:] * sin_tt) * qscale
        qat_ref[h] = qt.astype(BF16)
        ka_ref[h] = (k_all[:, h * LANES:(h + 1) * LANES] + kr).astype(BF16)
        vat_ref[h, 0] = jnp.where(row == V_DIM, 1.0, vt_all[h * LANES:(h + 1) * LANES]).astype(BF16)


def _inproj(h, g, tables, win, qng, kvng, wqt, wk, wvt, seq):
    m = h.shape[0]
    nt = m // TOK_TILE
    per_seq = seq // TOK_TILE
    per_group = MLA_KEY_GROUP // TOK_TILE
    tok = lambda w: pl.BlockSpec((TOK_TILE, w), lambda i: (i, 0))
    tab = pl.BlockSpec((TOK_TILE, LANES), lambda i: (i % per_seq, 0))
    tab_t = pl.BlockSpec((LANES, TOK_TILE), lambda i: (0, i % per_seq))
    heads = lambda n: pl.BlockSpec((n, TOK_TILE, LANES), lambda i: (0, i, 0))
    heads_t = lambda n: pl.BlockSpec((n, LANES, TOK_TILE), lambda i: (0, 0, i))
    out_shape = (
        jax.ShapeDtypeStruct((HA, LANES, m), BF16),
        jax.ShapeDtypeStruct((HA, m, LANES), BF16),
        jax.ShapeDtypeStruct((HA, m // MLA_KEY_GROUP, LANES, MLA_KEY_GROUP), BF16),
        jax.ShapeDtypeStruct((HB, m, LANES), BF16),
        jax.ShapeDtypeStruct((KVH, LANES, m), BF16),
        jax.ShapeDtypeStruct((KVH, m, LANES), BF16),
        jax.ShapeDtypeStruct((KVH, m, LANES), BF16),
        jax.ShapeDtypeStruct((m, D_MODEL), BF16),
        jax.ShapeDtypeStruct((m, D_MODEL), BF16),
    )
    out_specs = (
        heads_t(HA),
        heads(HA),
        pl.BlockSpec((HA, 1, LANES, TOK_TILE), lambda i: (0, i // per_group, 0, i % per_group)),
        heads(HB),
        heads_t(KVH),
        heads(KVH),
        heads(KVH),
        tok(D_MODEL),
        tok(D_MODEL),
    )
    cos_t, sin_t, cos_tt, sin_tt = tables
    return pl.pallas_call(
        _inproj_kernel,
        grid=(nt,),
        in_specs=[tok(D_MODEL), _const_spec((1, D_MODEL)), tab, tab, tab_t, tab_t,
                  _const_spec(win.shape), _const_spec((1, Q_LORA)), _const_spec((1, KV_LORA)),
                  _const_spec(wqt.shape), _const_spec(wk.shape), _const_spec(wvt.shape)],
        out_specs=out_specs,
        out_shape=out_shape,
        compiler_params=_params(1),
        name="inproj",
    )(h, g, cos_t, sin_t, cos_tt, sin_tt, win, qng, kvng, wqt, wk, wvt)


def _mla_kernel(qt_ref, k_ref, vt_ref, o_ref, s_ref, *, n_groups):
    gk = MLA_KEY_GROUP
    halves = []
    for j in range(2):
        qt = qt_ref[j]

        def scores(slot, gi, j=j, qt=qt):
            start = pl.multiple_of(gi * gk, gk)
            s_ref[slot] = _dot(k_ref[j, pl.ds(start, gk), :], qt)

        def consume(slot, gi, carry, j=j):
            m_old, acc = carry
            s = s_ref[slot]
            m_new = jnp.maximum(m_old, jnp.max(s, axis=0, keepdims=True))
            p = jnp.exp2(s - m_new).astype(BF16)
            return m_new, jnp.exp2(m_old - m_new) * acc + _dot(vt_ref[j, gi], p)

        carry = (jnp.full((1, MLA_Q_TILE), -jnp.inf, F32), jnp.zeros((LANES, MLA_Q_TILE), F32))
        scores(0, 0)
        if n_groups > 1:
            def pair(t, carry, scores=scores, consume=consume):
                scores(1, 2 * t + 1)
                carry = consume(0, 2 * t, carry)
                scores(0, 2 * t + 2)
                return consume(1, 2 * t + 1, carry)

            carry = lax.fori_loop(0, n_groups // 2 - 1, pair, carry)
            scores(1, n_groups - 1)
            carry = consume(0, n_groups - 2, carry)
            carry = consume(1, n_groups - 1, carry)
        else:
            carry = consume(0, 0, carry)
        acc = carry[1]
        halves.append(acc[:V_DIM] / acc[V_DIM:V_DIM + 1])
    o_ref[...] = jnp.concatenate(halves, axis=0).T.astype(BF16)


def _mla(qat, ka, vat, batch, seq):
    m = ka.shape[1]
    nq = seq // MLA_Q_TILE
    n_groups = seq // MLA_KEY_GROUP
    assert n_groups == 1 or n_groups % 2 == 0
    return pl.pallas_call(
        functools.partial(_mla_kernel, n_groups=n_groups),
        grid=(batch, HA // 2, nq),
        in_specs=[
            pl.BlockSpec((2, LANES, MLA_Q_TILE), lambda b, hp, i: (hp, 0, b * nq + i)),
            pl.BlockSpec((2, seq, LANES), lambda b, hp, i: (hp, b, 0)),
            pl.BlockSpec((2, n_groups, LANES, MLA_KEY_GROUP), lambda b, hp, i: (hp, b, 0, 0)),
        ],
        out_specs=pl.BlockSpec((MLA_Q_TILE, LANES), lambda b, hp, i: (b * nq + i, hp)),
        out_shape=jax.ShapeDtypeStruct((m, HA * V_DIM), BF16),
        scratch_shapes=[pltpu.VMEM((2, MLA_KEY_GROUP, MLA_Q_TILE), F32)],
        compiler_params=_params(3),
        name="mla",
    )(qat, ka, vat)


def _bias_kernel(bucket_ref, relb_ref, o_ref):
    bucket = bucket_ref[...]
    for h in range(HB):
        acc = jnp.full(bucket.shape, NEG, F32)
        for b in range(REL_BUCKETS):
            acc = jnp.where(bucket == b, relb_ref[b * HB + h], acc)
        o_ref[h] = acc


def _bias_table(rel_bias):
    r = jnp.arange(WIN_Q_TILE, dtype=jnp.int32)[:, None]
    c = jnp.arange(2 * WIN_Q_TILE, dtype=jnp.int32)[None, :]
    rel = c - WINDOW - r
    nb = REL_BUCKETS // 2
    max_exact = nb // 2
    ret = jnp.where(rel > 0, nb, 0)
    n = jnp.abs(rel)
    nf = jnp.maximum(n, 1).astype(F32)
    large = max_exact + (jnp.log(nf / max_exact) / math.log(REL_MAX_DIST / max_exact)
                         * (nb - max_exact)).astype(jnp.int32)
    large = jnp.minimum(large, nb - 1)
    bucket = ret + jnp.where(n < max_exact, n, large)
    bucket = jnp.where(n <= WINDOW, bucket, -1)
    return pl.pallas_call(
        _bias_kernel,
        in_specs=[pl.BlockSpec(memory_space=pltpu.VMEM), pl.BlockSpec(memory_space=pltpu.SMEM)],
        out_specs=pl.BlockSpec(memory_space=pltpu.VMEM),
        out_shape=jax.ShapeDtypeStruct((HB, WIN_Q_TILE, 2 * WIN_Q_TILE), F32),
        name="rel_bias_table",
    )(bucket, rel_bias.reshape(-1))


def _win_kernel(sink_ref, q_ref, bias_ref, ktp_ref, ktc_ref, ktn_ref,
                vep_ref, vec_ref, ven_ref, vop_ref, voc_ref, von_ref, o_ref, *, n_tiles):
    g = pl.program_id(0)
    i = pl.program_id(2)
    tq = WIN_Q_TILE
    q = q_ref[...].reshape(REP * tq, LANES)
    s = jnp.concatenate([_dot(q, ktp_ref[0][:, tq - WINDOW:]),
                         _dot(q, ktc_ref[0]),
                         _dot(q, ktn_ref[0][:, :WINDOW])], axis=1)
    s = s + bias_ref[0]
    col = lax.broadcasted_iota(jnp.int32, s.shape, 1)
    valid = jnp.logical_and(jnp.logical_or(i > 0, col >= WINDOW),
                            jnp.logical_or(i < n_tiles - 1, col < WINDOW + tq))
    s = jnp.where(valid, s, NEG)
    sink = jnp.concatenate([jnp.full((tq, 1), sink_ref[g * REP + r], F32) for r in range(REP)],
                           axis=0)
    m = jnp.maximum(jnp.max(s, axis=-1, keepdims=True), sink)
    p = jnp.exp(s - m).astype(BF16)
    sink_p = jnp.exp(sink - m)
    v_even = jnp.concatenate([vep_ref[0][tq - WINDOW:], vec_ref[0], ven_ref[0][:WINDOW]], axis=0)
    v_odd = jnp.concatenate([vop_ref[0][tq - WINDOW:], voc_ref[0], von_ref[0][:WINDOW]], axis=0)
    lane = lax.broadcasted_iota(jnp.int32, (tq, LANES), 1)
    outs = []
    for r in range(REP):
        rows = slice(r * tq, (r + 1) * tq)
        pv = _dot(p[rows], v_even if r % 2 == 0 else v_odd)
        denom_lane = HALF if r % 2 == 0 else 0
        outs.append(pv / (pv[:, denom_lane:denom_lane + 1] + sink_p[rows]))
    o_ref[...] = jnp.concatenate([jnp.where(lane < HALF, outs[0], outs[1]),
                                  jnp.where(lane < HALF, outs[2], outs[3])], axis=1).astype(BF16)


def _win(sink, qb, bias, kbt, vbe, vbo, batch, seq):
    m = qb.shape[1]
    tq = WIN_Q_TILE
    nt = seq // tq

    def kt_spec(off):
        return pl.BlockSpec((1, LANES, tq),
                            lambda g, b, i: (g, 0, b * nt + jnp.clip(i + off, 0, nt - 1)))

    def v_spec(off):
        return pl.BlockSpec((1, tq, LANES),
                            lambda g, b, i: (g, b * nt + jnp.clip(i + off, 0, nt - 1), 0))

    return pl.pallas_call(
        functools.partial(_win_kernel, n_tiles=nt),
        grid=(KVH, batch, nt),
        in_specs=[
            pl.BlockSpec(memory_space=pltpu.SMEM),
            pl.BlockSpec((REP, tq, LANES), lambda g, b, i: (g, b * nt + i, 0)),
            pl.BlockSpec((1, REP * tq, 2 * tq), lambda g, b, i: (g, 0, 0)),
            kt_spec(-1), kt_spec(0), kt_spec(1),
            v_spec(-1), v_spec(0), v_spec(1),
            v_spec(-1), v_spec(0), v_spec(1),
        ],
        out_specs=pl.BlockSpec((tq, REP * HD), lambda g, b, i: (b * nt + i, g)),
        out_shape=jax.ShapeDtypeStruct((m, HB * HD), BF16),
        compiler_params=_params(3),
        name="window",
    )(sink, qb, bias, kbt, kbt, kbt, vbe, vbe, vbe, vbo, vbo, vbo)


def _mix_kernel(h_ref, ya_ref, yb_ref, ga_ref, gb_ref, pa_ref, pb_ref, wo_ref, g_ref, o_ref):
    ma = _dot(ya_ref[...], pa_ref[...])
    mb = _dot(yb_ref[...], pb_ref[...])
    mix = (ga_ref[...].astype(F32) * ma + gb_ref[...].astype(F32) * mb).astype(BF16)
    o_ref[...] = h_ref[...] + _rms(_dot(mix, wo_ref[...]), g_ref[...])


def _mix(h, ya, yb, ga, gb, pa, pb, wo, g):
    m = h.shape[0]
    tok = lambda w: pl.BlockSpec((TOK_TILE, w), lambda i: (i, 0))
    return pl.pallas_call(
        _mix_kernel,
        grid=(m // TOK_TILE,),
        in_specs=[tok(D_MODEL), tok(HA * V_DIM), tok(HB * HD), tok(D_MODEL), tok(D_MODEL),
                  _const_spec(pa.shape), _const_spec(pb.shape), _const_spec(wo.shape),
                  _const_spec((1, D_MODEL))],
        out_specs=tok(D_MODEL),
        out_shape=jax.ShapeDtypeStruct((m, D_MODEL), F32),
        compiler_params=_params(1),
        name="mix",
    )(h, ya, yb, ga, gb, pa, pb, wo, g)


def _ple_kernel(h_ref, p_ref, gpre_ref, gpost_ref, wg_ref, wp_ref, o_ref):
    h = h_ref[...]
    gate = jax.nn.sigmoid(_dot(_rms(h, gpre_ref[...]).astype(BF16), wg_ref[...]))
    e = _dot(p_ref[...].astype(BF16), wp_ref[...]) * gate
    o_ref[...] = h + _rms(e, gpost_ref[...])


def _ple(h, p, gpre, gpost, wg, wp):
    m = h.shape[0]
    tok = lambda w: pl.BlockSpec((TOK_TILE, w), lambda i: (i, 0))
    return pl.pallas_call(
        _ple_kernel,
        grid=(m // TOK_TILE,),
        in_specs=[tok(D_MODEL), tok(PLE_DIM), _const_spec((1, D_MODEL)), _const_spec((1, D_MODEL)),
                  _const_spec(wg.shape), _const_spec(wp.shape)],
        out_specs=tok(D_MODEL),
        out_shape=jax.ShapeDtypeStruct((m, D_MODEL), F32),
        compiler_params=_params(1),
        name="ple",
    )(h, p, gpre, gpost, wg, wp)


def _pad_cols(w, width):
    return jnp.pad(w, ((0, 0), (0, width - w.shape[1])))


def _prep_weights(w_in, w_uq, w_uk, w_uv):
    splits = (Q_LORA, KV_LORA, ROPE_DIM, HB * HD, KVH * HD, KVH * HD, D_MODEL, D_MODEL)
    offs = [0]
    for s in splits:
        offs.append(offs[-1] + s)
    w_cq, w_ckv, w_kr, w_qb, w_kb, w_vb, w_ga, w_gb = (
        w_in[:, offs[k]:offs[k + 1]] for k in range(len(splits)))
    r2 = ROPE_DIM // 2
    zeros = lambda n: jnp.zeros((w_in.shape[0], n), w_in.dtype)
    kr_placed = jnp.concatenate([zeros(NOPE_DIM), w_kr, zeros(LANES - NOPE_DIM - ROPE_DIM)], axis=1)
    kr_rot = jnp.concatenate([zeros(NOPE_DIM), -w_kr[:, r2:], w_kr[:, :r2],
                              zeros(LANES - NOPE_DIM - ROPE_DIM)], axis=1)
    qb = jnp.concatenate([_pad_cols(w_qb[:, h * HD:(h + 1) * HD] * (HD ** -0.5), LANES)
                          for h in range(HB)], axis=1)
    kb = jnp.concatenate([_pad_cols(w_kb[:, g * HD:(g + 1) * HD], LANES) for g in range(KVH)], axis=1)
    vbe = jnp.concatenate([_pad_cols(w_vb[:, g * HD:(g + 1) * HD], LANES) for g in range(KVH)], axis=1)
    vbo = jnp.concatenate([jnp.concatenate([zeros(HALF), w_vb[:, g * HD:(g + 1) * HD]], axis=1)
                           for g in range(KVH)], axis=1)
    win = jnp.concatenate([w_cq, w_ckv, kr_placed, kr_rot, qb, kb, vbe, vbo, w_ga, w_gb],
                          axis=1).astype(BF16)

    qd = NOPE_DIM + ROPE_DIM
    zq = lambda n: jnp.zeros((Q_LORA, n), w_uq.dtype)
    wq = []
    for h in range(HA):
        wh = w_uq[:, h * qd:(h + 1) * qd]
        nope, x1, x2 = wh[:, :NOPE_DIM], wh[:, NOPE_DIM:NOPE_DIM + r2], wh[:, NOPE_DIM + r2:]
        wq.append(jnp.concatenate([nope, x1, x2, zq(LANES - qd),
                                   zq(NOPE_DIM), -x2, x1, zq(LANES - qd)], axis=1))
    wqt = jnp.swapaxes(jnp.stack(wq), 1, 2).astype(BF16)
    wk = jnp.concatenate([_pad_cols(w_uk[:, h * NOPE_DIM:(h + 1) * NOPE_DIM], LANES)
                          for h in range(HA)], axis=1).astype(BF16)
    wvt = jnp.concatenate([_pad_cols(w_uv[:, h * V_DIM:(h + 1) * V_DIM], LANES)
                           for h in range(HA)], axis=1).T.astype(BF16)
    return win, wqt, wk, wvt


def _rope_tables(seq):
    inv = 1.0 / (ROPE_THETA ** (jnp.arange(0, ROPE_DIM, 2, dtype=F32) / ROPE_DIM))
    ang = jnp.arange(seq, dtype=F32)[:, None] * inv[None, :]
    cos, sin = jnp.cos(ang), jnp.sin(ang)
    pad = jnp.zeros((seq, LANES - NOPE_DIM - ROPE_DIM), F32)
    cos_t = jnp.concatenate([jnp.ones((seq, NOPE_DIM), F32), cos, cos, pad], axis=1)
    sin_t = jnp.concatenate([jnp.zeros((seq, NOPE_DIM), F32), sin, sin, pad], axis=1)
    return cos_t, sin_t, cos_t.T, sin_t.T


def _encoder_layer(x, p, bias, sink, wts):
    batch, seq, _ = x.shape
    m = batch * seq
    x2 = x.reshape(m, D_MODEL)
    p2 = p.reshape(m, PLE_DIM)
    h = _ffn(x2, wts["ffn1_pre_g"], wts["ffn1_post_g"], wts["ffn1_w1"], wts["ffn1_w3"], wts["ffn1_w2"])
    qat, ka, vat, qb, kbt, vbe, vbo, ga, gb = _inproj(
        h, wts["mix_pre_g"], _rope_tables(seq), wts["win"], wts["q_norm_g"], wts["kv_norm_g"],
        wts["wqt"], wts["wk"], wts["wvt"], seq)
    ya = _mla(qat, ka, vat, batch, seq)
    yb = _win(sink, qb, bias, kbt, vbe, vbo, batch, seq)
    h = _mix(h, ya, yb, ga, gb, wts["w_proj_a"], wts["w_proj_b"], wts["w_out"], wts["mix_post_g"])
    h = _ffn(h, wts["ffn2_pre_g"], wts["ffn2_post_g"], wts["ffn2_w1"], wts["ffn2_w3"], wts["ffn2_w2"])
    out = _ple(h, p2, wts["ple_pre_g"], wts["ple_post_g"], wts["w_ple_gate"], wts["w_ple_proj"])
    return out.reshape(batch, seq, D_MODEL)


def kernel(x_prompt, x_sample, p_prompt, p_sample, rel_bias, ffn1_pre_g, ffn1_post_g, ffn1_w1, ffn1_w3, ffn1_w2, mix_pre_g, mix_post_g, w_in, q_norm_g, kv_norm_g, w_uq, w_uk, w_uv, sink, w_proj_a, w_proj_b, w_out, ffn2_pre_g, ffn2_post_g, ffn2_w1, ffn2_w3, ffn2_w2, ple_pre_g, ple_post_g, w_ple_gate, w_ple_proj):
    depth = ffn1_w1.shape[0]
    bias_all = _bias_table(rel_bias)
    bias = bias_all.reshape(KVH, REP * WIN_Q_TILE, 2 * WIN_Q_TILE)
    y_prompt, y_sample = x_prompt, x_sample
    for layer in range(depth):
        win, wqt, wk, wvt = _prep_weights(w_in[layer], w_uq[layer], w_uk[layer], w_uv[layer])
        wts = {
            "win": win, "wqt": wqt, "wk": wk, "wvt": wvt,
            "q_norm_g": q_norm_g[layer][None], "kv_norm_g": kv_norm_g[layer][None],
        }
        for name, arr in (("ffn1_pre_g", ffn1_pre_g), ("ffn1_post_g", ffn1_post_g),
                          ("mix_pre_g", mix_pre_g), ("mix_post_g", mix_post_g),
                          ("ffn2_pre_g", ffn2_pre_g), ("ffn2_post_g", ffn2_post_g),
                          ("ple_pre_g", ple_pre_g), ("ple_post_g", ple_post_g)):
            wts[name] = arr[layer][None]
        for name, arr in (("ffn1_w1", ffn1_w1), ("ffn1_w3", ffn1_w3), ("ffn1_w2", ffn1_w2),
                          ("w_proj_a", w_proj_a), ("w_proj_b", w_proj_b), ("w_out", w_out),
                          ("ffn2_w1", ffn2_w1), ("ffn2_w3", ffn2_w3), ("ffn2_w2", ffn2_w2),
                          ("w_ple_gate", w_ple_gate), ("w_ple_proj", w_ple_proj)):
            wts[name] = arr[layer].astype(BF16)
        y_prompt = _encoder_layer(y_prompt, p_prompt[layer], bias, sink[layer], wts)
        y_sample = _encoder_layer(y_sample, p_sample[layer], bias, sink[layer], wts)
    return (y_prompt, y_sample)
```

```python
import functools
import math

import jax
import jax.numpy as jnp
from jax import lax
from jax.experimental import pallas as pl
from jax.experimental.pallas import tpu as pltpu

D_MODEL = 1024
D_FF = 2816
PLE_DIM = 256
HA = 8
Q_LORA = 384
KV_LORA = 256
NOPE_DIM = 64
ROPE_DIM = 32
V_DIM = 64
ROPE_THETA = 10000.0
HB = 8
KVH = 2
REP = HB // KVH
HD = 64
WINDOW = 128
REL_BUCKETS = 32
REL_MAX_DIST = 128
EPS = 1e-6
NEG = -1e30

LANES = 128
HALF = LANES // 2
VMEM_LIMIT = 56 * 1024 * 1024

TOK_TILE = 512
FF_CHUNK = 256
MLA_Q_TILE = TOK_TILE
MLA_KEY_GROUP = 2048
MLA_ITEMS_PER_STEP = 8
LOG2E = math.log2(math.e)
WIN_Q_TILE = 2 * WINDOW

F32 = jnp.float32
BF16 = jnp.bfloat16

_C_CQ = 0
_C_CKV = _C_CQ + Q_LORA
_C_KR = _C_CKV + KV_LORA
_C_QB = _C_KR + 2 * LANES
_C_KB = _C_QB + HB * LANES
_C_VBE = _C_KB + KVH * LANES
_C_VBO = _C_VBE + KVH * LANES
_C_GA = _C_VBO + KVH * LANES
_C_GB = _C_GA + D_MODEL
_C_END = _C_GB + D_MODEL


def _params(n_axes):
    return pltpu.CompilerParams(dimension_semantics=("arbitrary",) * n_axes,
                                vmem_limit_bytes=VMEM_LIMIT)


def _const_spec(shape):
    nd = len(shape)
    return pl.BlockSpec(shape, lambda *_: (0,) * nd)


def _rms(x, g):
    return x * lax.rsqrt(jnp.mean(x * x, axis=-1, keepdims=True) + EPS) * g


def _dot(a, b):
    return jnp.dot(a, b, preferred_element_type=F32)


def _swiglu(xn, w1_ref, w3_ref, w2_ref):
    acc = None
    for c in range(D_FF // FF_CHUNK):
        sl = slice(c * FF_CHUNK, (c + 1) * FF_CHUNK)
        a = _dot(xn, w1_ref[:, sl])
        b = _dot(xn, w3_ref[:, sl])
        act = (a * jax.nn.sigmoid(a) * b).astype(BF16)
        d = _dot(act, w2_ref[sl, :])
        acc = d if acc is None else acc + d
    return acc


def _ffn_kernel(x_ref, gpre_ref, gpost_ref, w1_ref, w3_ref, w2_ref, o_ref):
    x = x_ref[...]
    xn = _rms(x, gpre_ref[...]).astype(BF16)
    f = _swiglu(xn, w1_ref, w3_ref, w2_ref)
    o_ref[...] = x + 0.5 * _rms(f, gpost_ref[...])


def _ffn(x, gpre, gpost, w1, w3, w2):
    m = x.shape[0]
    tok = pl.BlockSpec((TOK_TILE, D_MODEL), lambda i: (i, 0))
    return pl.pallas_call(
        _ffn_kernel,
        grid=(m // TOK_TILE,),
        in_specs=[tok, _const_spec((1, D_MODEL)), _const_spec((1, D_MODEL)),
                  _const_spec(w1.shape), _const_spec(w3.shape), _const_spec(w2.shape)],
        out_specs=tok,
        out_shape=jax.ShapeDtypeStruct((m, D_MODEL), F32),
        compiler_params=_params(1),
        name="ffn",
    )(x, gpre, gpost, w1, w3, w2)


def _inproj_kernel(h_ref, g_ref, cos_ref, sin_ref, cost_ref, sint_ref, win_ref, qng_ref, kvng_ref,
                   wqt_ref, wk_ref, wvt_ref,
                   qat_ref, ka_ref, vat_ref, qb_ref, kbt_ref, vbe_ref, vbo_ref,
                   ga_ref, gb_ref):
    u = _rms(h_ref[...], g_ref[...]).astype(BF16)
    lane = lax.broadcasted_iota(jnp.int32, (TOK_TILE, LANES), 1)
    row = lax.broadcasted_iota(jnp.int32, (LANES, TOK_TILE), 0)

    def proj(lo, hi):
        return _dot(u, win_ref[:, lo:hi])

    zq = proj(_C_QB, _C_KB)
    for hb in range(HB):
        qb_ref[hb] = zq[:, hb * LANES:(hb + 1) * LANES].astype(BF16)
    kbt = proj(_C_KB, _C_VBE).T
    zve = proj(_C_VBE, _C_VBO)
    zvo = proj(_C_VBO, _C_GA)
    for g in range(KVH):
        kbt_ref[g] = kbt[g * LANES:(g + 1) * LANES].astype(BF16)
        vbe_ref[g] = jnp.where(lane == HALF, 1.0, zve[:, g * LANES:(g + 1) * LANES]).astype(BF16)
        vbo_ref[g] = jnp.where(lane == 0, 1.0, zvo[:, g * LANES:(g + 1) * LANES]).astype(BF16)
    ga_ref[...] = jax.nn.sigmoid(proj(_C_GA, _C_GB)).astype(BF16)
    gb_ref[...] = jax.nn.sigmoid(proj(_C_GB, _C_END)).astype(BF16)

    cqt = _rms(proj(_C_CQ, _C_CKV), qng_ref[...]).T.astype(BF16)
    ckv = _rms(proj(_C_CKV, _C_KR), kvng_ref[...])
    ckvt = ckv.T.astype(BF16)
    zkr = proj(_C_KR, _C_QB)
    kr = zkr[:, :LANES] * cos_ref[...] + zkr[:, LANES:] * sin_ref[...]
    k_all = _dot(ckv.astype(BF16), wk_ref[...])
    vt_all = _dot(wvt_ref[...], ckvt)
    cos_tt = cost_ref[...]
    sin_tt = sint_ref[...]
    qscale = (NOPE_DIM + ROPE_DIM) ** -0.5 * LOG2E
    for h in range(HA):
        qqt = _dot(wqt_ref[h], cqt)
        qt = (qqt[:LANES] * cos_tt + qqt[LANES:] * sin_tt) * qscale
        qat_ref[h] = qt.astype(BF16)
        ka_ref[h] = (k_all[:, h * LANES:(h + 1) * LANES] + kr).astype(BF16)
        vat_ref[h, 0] = jnp.where(row == V_DIM, 1.0, vt_all[h * LANES:(h + 1) * LANES]).astype(BF16)


def _inproj(h, g, tables, win, qng, kvng, wqt, wk, wvt, seq):
    m = h.shape[0]
    nt = m // TOK_TILE
    per_seq = seq // TOK_TILE
    per_group = MLA_KEY_GROUP // TOK_TILE
    tok = lambda w: pl.BlockSpec((TOK_TILE, w), lambda i: (i, 0))
    tab = pl.BlockSpec((TOK_TILE, LANES), lambda i: (i % per_seq, 0))
    tab_t = pl.BlockSpec((LANES, TOK_TILE), lambda i: (0, i % per_seq))
    heads = lambda n: pl.BlockSpec((n, TOK_TILE, LANES), lambda i: (0, i, 0))
    heads_t = lambda n: pl.BlockSpec((n, LANES, TOK_TILE), lambda i: (0, 0, i))
    out_shape = (
        jax.ShapeDtypeStruct((HA, LANES, m), BF16),
        jax.ShapeDtypeStruct((HA, m, LANES), BF16),
        jax.ShapeDtypeStruct((HA, m // MLA_KEY_GROUP, LANES, MLA_KEY_GROUP), BF16),
        jax.ShapeDtypeStruct((HB, m, LANES), BF16),
        jax.ShapeDtypeStruct((KVH, LANES, m), BF16),
        jax.ShapeDtypeStruct((KVH, m, LANES), BF16),
        jax.ShapeDtypeStruct((KVH, m, LANES), BF16),
        jax.ShapeDtypeStruct((m, D_MODEL), BF16),
        jax.ShapeDtypeStruct((m, D_MODEL), BF16),
    )
    out_specs = (
        heads_t(HA),
        heads(HA),
        pl.BlockSpec((HA, 1, LANES, TOK_TILE), lambda i: (0, i // per_group, 0, i % per_group)),
        heads(HB),
        heads_t(KVH),
        heads(KVH),
        heads(KVH),
        tok(D_MODEL),
        tok(D_MODEL),
    )
    cos_t, sin_t, cos_tt, sin_tt = tables
    return pl.pallas_call(
        _inproj_kernel,
        grid=(nt,),
        in_specs=[tok(D_MODEL), _const_spec((1, D_MODEL)), tab, tab, tab_t, tab_t,
                  _const_spec(win.shape), _const_spec((1, Q_LORA)), _const_spec((1, KV_LORA)),
                  _const_spec(wqt.shape), _const_spec(wk.shape), _const_spec(wvt.shape)],
        out_specs=out_specs,
        out_shape=out_shape,
        compiler_params=_params(1),
        name="inproj",
    )(h, g, cos_t, sin_t, cos_tt, sin_tt, win, qng, kvng, wqt, wk, wvt)


def _mla_kernel(qt_ref, k_ref, vt_ref, o_ref, s_ref, m_ref, *, n_groups, q_tiles):
    gk, tq = MLA_KEY_GROUP, MLA_Q_TILE

    def scores(slot, qi, j, g):
        start = g * gk if isinstance(g, int) else pl.multiple_of(g * gk, gk)
        s = _dot(k_ref[j, pl.ds(start, gk), :], qt_ref[j, :, qi * tq:(qi + 1) * tq])
        s_ref[slot] = s
        m_ref[slot] = jnp.max(s, axis=0, keepdims=True)

    def consume(slot, j, g, carry):
        m_old, acc = carry
        m_new = jnp.maximum(m_old, m_ref[slot])
        p = jnp.exp2(s_ref[slot] - m_new).astype(BF16)
        return m_new, jnp.exp2(m_old - m_new) * acc + _dot(vt_ref[j, g], p)

    def init():
        return jnp.full((1, tq), -jnp.inf, F32), jnp.zeros((LANES, tq), F32)

    def finish(carry):
        acc = carry[1]
        return acc[:V_DIM] / acc[V_DIM:V_DIM + 1]

    if n_groups == 1:
        items = [(qi, j) for qi in range(q_tiles) for j in range(2)]
        scores(0, *items[0], 0)
        halves = []
        for n, (qi, j) in enumerate(items):
            if n + 1 < len(items):
                scores((n + 1) % 2, *items[n + 1], 0)
            halves.append(finish(consume(n % 2, j, 0, init())))
            if j == 1:
                o_ref[qi * tq:(qi + 1) * tq, :] = jnp.concatenate(halves, axis=0).T.astype(BF16)
                halves = []
    else:
        halves = []
        for j in range(2):
            def pair(t, carry, j=j):
                scores(1, 0, j, 2 * t + 1)
                carry = consume(0, j, 2 * t, carry)
                scores(0, 0, j, 2 * t + 2)
                return consume(1, j, 2 * t + 1, carry)

            scores(0, 0, j, 0)
            carry = lax.fori_loop(0, n_groups // 2 - 1, pair, init())
            scores(1, 0, j, n_groups - 1)
            carry = consume(0, j, n_groups - 2, carry)
            halves.append(finish(consume(1, j, n_groups - 1, carry)))
        o_ref[...] = jnp.concatenate(halves, axis=0).T.astype(BF16)


def _mla(qat, ka, vat, batch, seq):
    m = ka.shape[1]
    n_groups = seq // MLA_KEY_GROUP
    assert n_groups == 1 or n_groups % 2 == 0
    q_tiles = MLA_ITEMS_PER_STEP // 2 if n_groups == 1 else 1
    tqs = q_tiles * MLA_Q_TILE
    nq = seq // tqs
    return pl.pallas_call(
        functools.partial(_mla_kernel, n_groups=n_groups, q_tiles=q_tiles),
        grid=(batch, HA // 2, nq),
        in_specs=[
            pl.BlockSpec((2, LANES, tqs), lambda b, hp, i: (hp, 0, b * nq + i)),
            pl.BlockSpec((2, seq, LANES), lambda b, hp, i: (hp, b, 0)),
            pl.BlockSpec((2, n_groups, LANES, MLA_KEY_GROUP), lambda b, hp, i: (hp, b, 0, 0)),
        ],
        out_specs=pl.BlockSpec((tqs, LANES), lambda b, hp, i: (b * nq + i, hp)),
        out_shape=jax.ShapeDtypeStruct((m, HA * V_DIM), BF16),
        scratch_shapes=[pltpu.VMEM((2, MLA_KEY_GROUP, MLA_Q_TILE), F32),
                        pltpu.VMEM((2, 1, MLA_Q_TILE), F32)],
        compiler_params=_params(3),
        name="mla",
    )(qat, ka, vat)


def _bias_kernel(bucket_ref, relb_ref, o_ref):
    bucket = bucket_ref[...]
    for h in range(HB):
        acc = jnp.full(bucket.shape, NEG, F32)
        for b in range(REL_BUCKETS):
            acc = jnp.where(bucket == b, relb_ref[b * HB + h], acc)
        o_ref[h] = acc


def _bias_table(rel_bias):
    r = jnp.arange(WIN_Q_TILE, dtype=jnp.int32)[:, None]
    c = jnp.arange(2 * WIN_Q_TILE, dtype=jnp.int32)[None, :]
    rel = c - WINDOW - r
    nb = REL_BUCKETS // 2
    max_exact = nb // 2
    ret = jnp.where(rel > 0, nb, 0)
    n = jnp.abs(rel)
    nf = jnp.maximum(n, 1).astype(F32)
    large = max_exact + (jnp.log(nf / max_exact) / math.log(REL_MAX_DIST / max_exact)
                         * (nb - max_exact)).astype(jnp.int32)
    large = jnp.minimum(large, nb - 1)
    bucket = ret + jnp.where(n < max_exact, n, large)
    bucket = jnp.where(n <= WINDOW, bucket, -1)
    return pl.pallas_call(
        _bias_kernel,
        in_specs=[pl.BlockSpec(memory_space=pltpu.VMEM), pl.BlockSpec(memory_space=pltpu.SMEM)],
        out_specs=pl.BlockSpec(memory_space=pltpu.VMEM),
        out_shape=jax.ShapeDtypeStruct((HB, WIN_Q_TILE, 2 * WIN_Q_TILE), F32),
        name="rel_bias_table",
    )(bucket, rel_bias.reshape(-1))


def _win_kernel(sink_ref, q_ref, bias_ref, ktp_ref, ktc_ref, ktn_ref,
                vep_ref, vec_ref, ven_ref, vop_ref, voc_ref, von_ref, o_ref, *, n_tiles):
    g = pl.program_id(0)
    i = pl.program_id(2)
    tq = WIN_Q_TILE
    q = q_ref[...].reshape(REP * tq, LANES)
    s = jnp.concatenate([_dot(q, ktp_ref[0][:, tq - WINDOW:]),
                         _dot(q, ktc_ref[0]),
                         _dot(q, ktn_ref[0][:, :WINDOW])], axis=1)
    s = s + bias_ref[0]
    col = lax.broadcasted_iota(jnp.int32, s.shape, 1)
    valid = jnp.logical_and(jnp.logical_or(i > 0, col >= WINDOW),
                            jnp.logical_or(i < n_tiles - 1, col < WINDOW + tq))
    s = jnp.where(valid, s, NEG)
    sink = jnp.concatenate([jnp.full((tq, 1), sink_ref[g * REP + r], F32) for r in range(REP)],
                           axis=0)
    m = jnp.maximum(jnp.max(s, axis=-1, keepdims=True), sink)
    p = jnp.exp(s - m).astype(BF16)
    sink_p = jnp.exp(sink - m)
    v_even = jnp.concatenate([vep_ref[0][tq - WINDOW:], vec_ref[0], ven_ref[0][:WINDOW]], axis=0)
    v_odd = jnp.concatenate([vop_ref[0][tq - WINDOW:], voc_ref[0], von_ref[0][:WINDOW]], axis=0)
    lane = lax.broadcasted_iota(jnp.int32, (tq, LANES), 1)
    outs = []
    for r in range(REP):
        rows = slice(r * tq, (r + 1) * tq)
        pv = _dot(p[rows], v_even if r % 2 == 0 else v_odd)
        denom_lane = HALF if r % 2 == 0 else 0
        outs.append(pv / (pv[:, denom_lane:denom_lane + 1] + sink_p[rows]))
    o_ref[...] = jnp.concatenate([jnp.where(lane < HALF, outs[0], outs[1]),
                                  jnp.where(lane < HALF, outs[2], outs[3])], axis=1).astype(BF16)


def _win(sink, qb, bias, kbt, vbe, vbo, batch, seq):
    m = qb.shape[1]
    tq = WIN_Q_TILE
    nt = seq // tq

    def kt_spec(off):
        return pl.BlockSpec((1, LANES, tq),
                            lambda g, b, i: (g, 0, b * nt + jnp.clip(i + off, 0, nt - 1)))

    def v_spec(off):
        return pl.BlockSpec((1, tq, LANES),
                            lambda g, b, i: (g, b * nt + jnp.clip(i + off, 0, nt - 1), 0))

    return pl.pallas_call(
        functools.partial(_win_kernel, n_tiles=nt),
        grid=(KVH, batch, nt),
        in_specs=[
            pl.BlockSpec(memory_space=pltpu.SMEM),
            pl.BlockSpec((REP, tq, LANES), lambda g, b, i: (g, b * nt + i, 0)),
            pl.BlockSpec((1, REP * tq, 2 * tq), lambda g, b, i: (g, 0, 0)),
            kt_spec(-1), kt_spec(0), kt_spec(1),
            v_spec(-1), v_spec(0), v_spec(1),
            v_spec(-1), v_spec(0), v_spec(1),
        ],
        out_specs=pl.BlockSpec((tq, REP * HD), lambda g, b, i: (b * nt + i, g)),
        out_shape=jax.ShapeDtypeStruct((m, HB * HD), BF16),
        compiler_params=_params(3),
        name="window",
    )(sink, qb, bias, kbt, kbt, kbt, vbe, vbe, vbe, vbo, vbo, vbo)


def _mix_kernel(h_ref, ya_ref, yb_ref, ga_ref, gb_ref, pa_ref, pb_ref, wo_ref, g_ref, o_ref):
    ma = _dot(ya_ref[...], pa_ref[...])
    mb = _dot(yb_ref[...], pb_ref[...])
    mix = (ga_ref[...].astype(F32) * ma + gb_ref[...].astype(F32) * mb).astype(BF16)
    o_ref[...] = h_ref[...] + _rms(_dot(mix, wo_ref[...]), g_ref[...])


def _mix(h, ya, yb, ga, gb, pa, pb, wo, g):
    m = h.shape[0]
    tok = lambda w: pl.BlockSpec((TOK_TILE, w), lambda i: (i, 0))
    return pl.pallas_call(
        _mix_kernel,
        grid=(m // TOK_TILE,),
        in_specs=[tok(D_MODEL), tok(HA * V_DIM), tok(HB * HD), tok(D_MODEL), tok(D_MODEL),
                  _const_spec(pa.shape), _const_spec(pb.shape), _const_spec(wo.shape),
                  _const_spec((1, D_MODEL))],
        out_specs=tok(D_MODEL),
        out_shape=jax.ShapeDtypeStruct((m, D_MODEL), F32),
        compiler_params=_params(1),
        name="mix",
    )(h, ya, yb, ga, gb, pa, pb, wo, g)


def _ple_kernel(h_ref, p_ref, gpre_ref, gpost_ref, wg_ref, wp_ref, o_ref):
    h = h_ref[...]
    gate = jax.nn.sigmoid(_dot(_rms(h, gpre_ref[...]).astype(BF16), wg_ref[...]))
    e = _dot(p_ref[...].astype(BF16), wp_ref[...]) * gate
    o_ref[...] = h + _rms(e, gpost_ref[...])


def _ple(h, p, gpre, gpost, wg, wp):
    m = h.shape[0]
    tok = lambda w: pl.BlockSpec((TOK_TILE, w), lambda i: (i, 0))
    return pl.pallas_call(
        _ple_kernel,
        grid=(m // TOK_TILE,),
        in_specs=[tok(D_MODEL), tok(PLE_DIM), _const_spec((1, D_MODEL)), _const_spec((1, D_MODEL)),
                  _const_spec(wg.shape), _const_spec(wp.shape)],
        out_specs=tok(D_MODEL),
        out_shape=jax.ShapeDtypeStruct((m, D_MODEL), F32),
        compiler_params=_params(1),
        name="ple",
    )(h, p, gpre, gpost, wg, wp)


def _pad_cols(w, width):
    return jnp.pad(w, ((0, 0), (0, width - w.shape[1])))


def _prep_weights(w_in, w_uq, w_uk, w_uv):
    splits = (Q_LORA, KV_LORA, ROPE_DIM, HB * HD, KVH * HD, KVH * HD, D_MODEL, D_MODEL)
    offs = [0]
    for s in splits:
        offs.append(offs[-1] + s)
    w_cq, w_ckv, w_kr, w_qb, w_kb, w_vb, w_ga, w_gb = (
        w_in[:, offs[k]:offs[k + 1]] for k in range(len(splits)))
    r2 = ROPE_DIM // 2
    zeros = lambda n: jnp.zeros((w_in.shape[0], n), w_in.dtype)
    kr_placed = jnp.concatenate([zeros(NOPE_DIM), w_kr, zeros(LANES - NOPE_DIM - ROPE_DIM)], axis=1)
    kr_rot = jnp.concatenate([zeros(NOPE_DIM), -w_kr[:, r2:], w_kr[:, :r2],
                              zeros(LANES - NOPE_DIM - ROPE_DIM)], axis=1)
    qb = jnp.concatenate([_pad_cols(w_qb[:, h * HD:(h + 1) * HD] * (HD ** -0.5), LANES)
                          for h in range(HB)], axis=1)
    kb = jnp.concatenate([_pad_cols(w_kb[:, g * HD:(g + 1) * HD], LANES) for g in range(KVH)], axis=1)
    vbe = jnp.concatenate([_pad_cols(w_vb[:, g * HD:(g + 1) * HD], LANES) for g in range(KVH)], axis=1)
    vbo = jnp.concatenate([jnp.concatenate([zeros(HALF), w_vb[:, g * HD:(g + 1) * HD]], axis=1)
                           for g in range(KVH)], axis=1)
    win = jnp.concatenate([w_cq, w_ckv, kr_placed, kr_rot, qb, kb, vbe, vbo, w_ga, w_gb],
                          axis=1).astype(BF16)

    qd = NOPE_DIM + ROPE_DIM
    zq = lambda n: jnp.zeros((Q_LORA, n), w_uq.dtype)
    wq = []
    for h in range(HA):
        wh = w_uq[:, h * qd:(h + 1) * qd]
        nope, x1, x2 = wh[:, :NOPE_DIM], wh[:, NOPE_DIM:NOPE_DIM + r2], wh[:, NOPE_DIM + r2:]
        wq.append(jnp.concatenate([nope, x1, x2, zq(LANES - qd),
                                   zq(NOPE_DIM), -x2, x1, zq(LANES - qd)], axis=1))
    wqt = jnp.swapaxes(jnp.stack(wq), 1, 2).astype(BF16)
    wk = jnp.concatenate([_pad_cols(w_uk[:, h * NOPE_DIM:(h + 1) * NOPE_DIM], LANES)
                          for h in range(HA)], axis=1).astype(BF16)
    wvt = jnp.concatenate([_pad_cols(w_uv[:, h * V_DIM:(h + 1) * V_DIM], LANES)
                           for h in range(HA)], axis=1).T.astype(BF16)
    return win, wqt, wk, wvt


def _rope_tables(seq):
    inv = 1.0 / (ROPE_THETA ** (jnp.arange(0, ROPE_DIM, 2, dtype=F32) / ROPE_DIM))
    ang = jnp.arange(seq, dtype=F32)[:, None] * inv[None, :]
    cos, sin = jnp.cos(ang), jnp.sin(ang)
    pad = jnp.zeros((seq, LANES - NOPE_DIM - ROPE_DIM), F32)
    cos_t = jnp.concatenate([jnp.ones((seq, NOPE_DIM), F32), cos, cos, pad], axis=1)
    sin_t = jnp.concatenate([jnp.zeros((seq, NOPE_DIM), F32), sin, sin, pad], axis=1)
    return cos_t, sin_t, cos_t.T, sin_t.T


def _encoder_layer(x, p, bias, sink, wts):
    batch, seq, _ = x.shape
    m = batch * seq
    x2 = x.reshape(m, D_MODEL)
    p2 = p.reshape(m, PLE_DIM)
    h = _ffn(x2, wts["ffn1_pre_g"], wts["ffn1_post_g"], wts["ffn1_w1"], wts["ffn1_w3"], wts["ffn1_w2"])
    qat, ka, vat, qb, kbt, vbe, vbo, ga, gb = _inproj(
        h, wts["mix_pre_g"], _rope_tables(seq), wts["win"], wts["q_norm_g"], wts["kv_norm_g"],
        wts["wqt"], wts["wk"], wts["wvt"], seq)
    ya = _mla(qat, ka, vat, batch, seq)
    yb = _win(sink, qb, bias, kbt, vbe, vbo, batch, seq)
    h = _mix(h, ya, yb, ga, gb, wts["w_proj_a"], wts["w_proj_b"], wts["w_out"], wts["mix_post_g"])
    h = _ffn(h, wts["ffn2_pre_g"], wts["ffn2_post_g"], wts["ffn2_w1"], wts["ffn2_w3"], wts["ffn2_w2"])
    out = _ple(h, p2, wts["ple_pre_g"], wts["ple_post_g"], wts["w_ple_gate"], wts["w_ple_proj"])
    return out.reshape(batch, seq, D_MODEL)


def kernel(x_prompt, x_sample, p_prompt, p_sample, rel_bias, ffn1_pre_g, ffn1_post_g, ffn1_w1, ffn1_w3, ffn1_w2, mix_pre_g, mix_post_g, w_in, q_norm_g, kv_norm_g, w_uq, w_uk, w_uv, sink, w_proj_a, w_proj_b, w_out, ffn2_pre_g, ffn2_post_g, ffn2_w1, ffn2_w3, ffn2_w2, ple_pre_g, ple_post_g, w_ple_gate, w_ple_proj):
    depth = ffn1_w1.shape[0]
    bias_all = _bias_table(rel_bias)
    bias = bias_all.reshape(KVH, REP * WIN_Q_TILE, 2 * WIN_Q_TILE)
    y_prompt, y_sample = x_prompt, x_sample
    for layer in range(depth):
        win, wqt, wk, wvt = _prep_weights(w_in[layer], w_uq[layer], w_uk[layer], w_uv[layer])
        wts = {
            "win": win, "wqt": wqt, "wk": wk, "wvt": wvt,
            "q_norm_g": q_norm_g[layer][None], "kv_norm_g": kv_norm_g[layer][None],
        }
        for name, arr in (("ffn1_pre_g", ffn1_pre_g), ("ffn1_post_g", ffn1_post_g),
                          ("mix_pre_g", mix_pre_g), ("mix_post_g", mix_post_g),
                          ("ffn2_pre_g", ffn2_pre_g), ("ffn2_post_g", ffn2_post_g),
                          ("ple_pre_g", ple_pre_g), ("ple_post_g", ple_post_g)):
            wts[name] = arr[layer][None]
        for name, arr in (("ffn1_w1", ffn1_w1), ("ffn1_w3", ffn1_w3), ("ffn1_w2", ffn1_w2),
                          ("w_proj_a", w_proj_a), ("w_proj_b", w_proj_b), ("w_out", w_out),
                          ("ffn2_w1", ffn2_w1), ("ffn2_w3", ffn2_w3), ("ffn2_w2", ffn2_w2),
                          ("w_ple_gate", w_ple_gate), ("w_ple_proj", w_ple_proj)):
            wts[name] = arr[layer].astype(BF16)
        y_prompt = _encoder_layer(y_prompt, p_prompt[layer], bias, sink[layer], wts)
        y_sample = _encoder_layer(y_sample, p_sample[layer], bias, sink[layer], wts)
    return (y_prompt, y_sample)
```

```python
import functools
import math

import jax
import jax.numpy as jnp
from jax import lax
from jax.experimental import pallas as pl
from jax.experimental.pallas import tpu as pltpu

D_MODEL = 1024
D_FF = 2816
PLE_DIM = 256
HA = 8
Q_LORA = 384
KV_LORA = 256
NOPE_DIM = 64
ROPE_DIM = 32
V_DIM = 64
ROPE_THETA = 10000.0
HB = 8
KVH = 2
REP = HB // KVH
HD = 64
WINDOW = 128
REL_BUCKETS = 32
REL_MAX_DIST = 128
EPS = 1e-6
NEG = -1e30

LANES = 128
VMEM_LIMIT = 56 * 1024 * 1024

TOK_TILE = 512
FF_CHUNK = 256
MLA_Q_TILE = TOK_TILE
MLA_KEY_GROUP = 2048
MLA_ITEMS_PER_STEP = 8
WIN_Q_TILE = 2 * WINDOW
WIN_STEP = 2 * WIN_Q_TILE
LOG2E = math.log2(math.e)

F32 = jnp.float32
BF16 = jnp.bfloat16

_N_CKV = 0
_N_KR = _N_CKV + KV_LORA
_N_KB = _N_KR + 2 * LANES
_N_GA = _N_KB + KVH * LANES
_N_GB = _N_GA + D_MODEL
_N_END = _N_GB + D_MODEL
_T_CQ = 0
_T_CKV = _T_CQ + Q_LORA
_T_QB = _T_CKV + KV_LORA
_T_VB = _T_QB + HB * LANES
_T_END = _T_VB + KVH * LANES


def _params(n_axes):
    return pltpu.CompilerParams(dimension_semantics=("arbitrary",) * n_axes,
                                vmem_limit_bytes=VMEM_LIMIT)


def _const_spec(shape):
    nd = len(shape)
    return pl.BlockSpec(shape, lambda *_: (0,) * nd)


def _rms(x, g, axis=-1):
    return x * lax.rsqrt(jnp.mean(x * x, axis=axis, keepdims=True) + EPS) * g


def _dot(a, b):
    return jnp.dot(a, b, preferred_element_type=F32)


def _with_ones_row(vt):
    row = lax.broadcasted_iota(jnp.int32, vt.shape, 0)
    return jnp.where(row == V_DIM, 1.0, vt)


def _swiglu(xn, w1_ref, w3_ref, w2_ref):
    acc = None
    for c in range(D_FF // FF_CHUNK):
        sl = slice(c * FF_CHUNK, (c + 1) * FF_CHUNK)
        a = _dot(xn, w1_ref[:, sl])
        b = _dot(xn, w3_ref[:, sl])
        act = (a * jax.nn.sigmoid(a) * b).astype(BF16)
        d = _dot(act, w2_ref[sl, :])
        acc = d if acc is None else acc + d
    return acc


def _ffn_kernel(x_ref, gpre_ref, gpost_ref, w1_ref, w3_ref, w2_ref, o_ref):
    x = x_ref[...]
    xn = _rms(x, gpre_ref[...]).astype(BF16)
    f = _swiglu(xn, w1_ref, w3_ref, w2_ref)
    o_ref[...] = x + 0.5 * _rms(f, gpost_ref[...])


def _ffn(x, gpre, gpost, w1, w3, w2):
    m = x.shape[0]
    tok = pl.BlockSpec((TOK_TILE, D_MODEL), lambda i: (i, 0))
    return pl.pallas_call(
        _ffn_kernel,
        grid=(m // TOK_TILE,),
        in_specs=[tok, _const_spec((1, D_MODEL)), _const_spec((1, D_MODEL)),
                  _const_spec(w1.shape), _const_spec(w3.shape), _const_spec(w2.shape)],
        out_specs=tok,
        out_shape=jax.ShapeDtypeStruct((m, D_MODEL), F32),
        compiler_params=_params(1),
        name="ffn",
    )(x, gpre, gpost, w1, w3, w2)


def _inproj_kernel(h_ref, g_ref, cos_ref, sin_ref, cost_ref, sint_ref, wnat_ref, wtr_ref,
                   qngt_ref, kvng_ref, kvngt_ref, wqt_ref, wk_ref, wvt_ref,
                   qat_ref, ka_ref, vat_ref, qbt_ref, kb_ref, vbt_ref, ga_ref, gb_ref):
    u = _rms(h_ref[...], g_ref[...])
    ub = u.astype(BF16)
    ut = u.T.astype(BF16)

    def nat(lo, hi):
        return _dot(ub, wnat_ref[:, lo:hi])

    def tr(lo, hi):
        return _dot(wtr_ref[lo:hi, :], ut)

    ga_ref[...] = jax.nn.sigmoid(nat(_N_GA, _N_GB)).astype(BF16)
    gb_ref[...] = jax.nn.sigmoid(nat(_N_GB, _N_END)).astype(BF16)

    qbt = tr(_T_QB, _T_VB) * LOG2E
    for hb in range(HB):
        qbt_ref[hb] = qbt[hb * LANES:(hb + 1) * LANES].astype(BF16)
    kb = nat(_N_KB, _N_GA)
    vbt = tr(_T_VB, _T_END)
    for g in range(KVH):
        kb_ref[g] = kb[:, g * LANES:(g + 1) * LANES].astype(BF16)
        vbt_ref[g] = _with_ones_row(vbt[g * LANES:(g + 1) * LANES]).astype(BF16)

    ckv = _rms(nat(_N_CKV, _N_KR), kvng_ref[...]).astype(BF16)
    zkr = nat(_N_KR, _N_KB)
    kr = zkr[:, :LANES] * cos_ref[...] + zkr[:, LANES:] * sin_ref[...]
    k_all = _dot(ckv, wk_ref[...])
    cqt = _rms(tr(_T_CQ, _T_CKV), qngt_ref[...], axis=0).astype(BF16)
    ckvt = _rms(tr(_T_CKV, _T_QB), kvngt_ref[...], axis=0).astype(BF16)
    vt_all = _dot(wvt_ref[...], ckvt)
    cos_tt = cost_ref[...]
    sin_tt = sint_ref[...]
    qscale = (NOPE_DIM + ROPE_DIM) ** -0.5 * LOG2E
    for h in range(HA):
        qqt = _dot(wqt_ref[h], cqt)
        qt = (qqt[:LANES] * cos_tt + qqt[LANES:] * sin_tt) * qscale
        qat_ref[h] = qt.astype(BF16)
        ka_ref[h] = (k_all[:, h * LANES:(h + 1) * LANES] + kr).astype(BF16)
        vat_ref[h, 0] = _with_ones_row(vt_all[h * LANES:(h + 1) * LANES]).astype(BF16)


def _inproj(h, g, tables, wnat, wtr, qngt, kvng, kvngt, wqt, wk, wvt, seq):
    m = h.shape[0]
    nt = m // TOK_TILE
    per_seq = seq // TOK_TILE
    per_group = MLA_KEY_GROUP // TOK_TILE
    tok = lambda w: pl.BlockSpec((TOK_TILE, w), lambda i: (i, 0))
    tab = pl.BlockSpec((TOK_TILE, LANES), lambda i: (i % per_seq, 0))
    tab_t = pl.BlockSpec((LANES, TOK_TILE), lambda i: (0, i % per_seq))
    heads = lambda n: pl.BlockSpec((n, TOK_TILE, LANES), lambda i: (0, i, 0))
    heads_t = lambda n: pl.BlockSpec((n, LANES, TOK_TILE), lambda i: (0, 0, i))
    out_shape = (
        jax.ShapeDtypeStruct((HA, LANES, m), BF16),
        jax.ShapeDtypeStruct((HA, m, LANES), BF16),
        jax.ShapeDtypeStruct((HA, m // MLA_KEY_GROUP, LANES, MLA_KEY_GROUP), BF16),
        jax.ShapeDtypeStruct((HB, LANES, m), BF16),
        jax.ShapeDtypeStruct((KVH, m, LANES), BF16),
        jax.ShapeDtypeStruct((KVH, LANES, m), BF16),
        jax.ShapeDtypeStruct((m, D_MODEL), BF16),
        jax.ShapeDtypeStruct((m, D_MODEL), BF16),
    )
    out_specs = (
        heads_t(HA),
        heads(HA),
        pl.BlockSpec((HA, 1, LANES, TOK_TILE), lambda i: (0, i // per_group, 0, i % per_group)),
        heads_t(HB),
        heads(KVH),
        heads_t(KVH),
        tok(D_MODEL),
        tok(D_MODEL),
    )
    cos_t, sin_t, cos_tt, sin_tt = tables
    return pl.pallas_call(
        _inproj_kernel,
        grid=(nt,),
        in_specs=[tok(D_MODEL), _const_spec((1, D_MODEL)), tab, tab, tab_t, tab_t,
                  _const_spec(wnat.shape), _const_spec(wtr.shape),
                  _const_spec((Q_LORA, 1)), _const_spec((1, KV_LORA)), _const_spec((KV_LORA, 1)),
                  _const_spec(wqt.shape), _const_spec(wk.shape), _const_spec(wvt.shape)],
        out_specs=out_specs,
        out_shape=out_shape,
        compiler_params=_params(1),
        name="inproj",
    )(h, g, cos_t, sin_t, cos_tt, sin_tt, wnat, wtr, qngt, kvng, kvngt, wqt, wk, wvt)


def _mla_kernel(qt_ref, k_ref, vt_ref, o_ref, s_ref, m_ref, *, n_groups, q_tiles):
    gk, tq = MLA_KEY_GROUP, MLA_Q_TILE

    def scores(slot, qi, j, g):
        start = g * gk if isinstance(g, int) else pl.multiple_of(g * gk, gk)
        s = _dot(k_ref[j, pl.ds(start, gk), :], qt_ref[j, :, qi * tq:(qi + 1) * tq])
        s_ref[slot] = s
        m_ref[slot] = jnp.max(s, axis=0, keepdims=True)

    def consume(slot, j, g, carry):
        m_old, acc = carry
        m_new = jnp.maximum(m_old, m_ref[slot])
        p = jnp.exp2(s_ref[slot] - m_new).astype(BF16)
        return m_new, jnp.exp2(m_old - m_new) * acc + _dot(vt_ref[j, g], p)

    def init():
        return jnp.full((1, tq), -jnp.inf, F32), jnp.zeros((LANES, tq), F32)

    def finish(carry):
        acc = carry[1]
        return acc[:V_DIM] / acc[V_DIM:V_DIM + 1]

    if n_groups == 1:
        items = [(qi, j) for qi in range(q_tiles) for j in range(2)]
        scores(0, *items[0], 0)
        halves = []
        for n, (qi, j) in enumerate(items):
            if n + 1 < len(items):
                scores((n + 1) % 2, *items[n + 1], 0)
            halves.append(finish(consume(n % 2, j, 0, init())))
            if j == 1:
                o_ref[qi * tq:(qi + 1) * tq, :] = jnp.concatenate(halves, axis=0).T.astype(BF16)
                halves = []
    else:
        halves = []
        for j in range(2):
            def pair(t, carry, j=j):
                scores(1, 0, j, 2 * t + 1)
                carry = consume(0, j, 2 * t, carry)
                scores(0, 0, j, 2 * t + 2)
                return consume(1, j, 2 * t + 1, carry)

            scores(0, 0, j, 0)
            carry = lax.fori_loop(0, n_groups // 2 - 1, pair, init())
            scores(1, 0, j, n_groups - 1)
            carry = consume(0, j, n_groups - 2, carry)
            halves.append(finish(consume(1, j, n_groups - 1, carry)))
        o_ref[...] = jnp.concatenate(halves, axis=0).T.astype(BF16)


def _mla(qat, ka, vat, batch, seq):
    m = ka.shape[1]
    n_groups = seq // MLA_KEY_GROUP
    assert n_groups == 1 or n_groups % 2 == 0
    q_tiles = MLA_ITEMS_PER_STEP // 2 if n_groups == 1 else 1
    tqs = q_tiles * MLA_Q_TILE
    nq = seq // tqs
    return pl.pallas_call(
        functools.partial(_mla_kernel, n_groups=n_groups, q_tiles=q_tiles),
        grid=(batch, HA // 2, nq),
        in_specs=[
            pl.BlockSpec((2, LANES, tqs), lambda b, hp, i: (hp, 0, b * nq + i)),
            pl.BlockSpec((2, seq, LANES), lambda b, hp, i: (hp, b, 0)),
            pl.BlockSpec((2, n_groups, LANES, MLA_KEY_GROUP), lambda b, hp, i: (hp, b, 0, 0)),
        ],
        out_specs=pl.BlockSpec((tqs, LANES), lambda b, hp, i: (b * nq + i, hp)),
        out_shape=jax.ShapeDtypeStruct((m, HA * V_DIM), BF16),
        scratch_shapes=[pltpu.VMEM((2, MLA_KEY_GROUP, MLA_Q_TILE), F32),
                        pltpu.VMEM((2, 1, MLA_Q_TILE), F32)],
        compiler_params=_params(3),
        name="mla",
    )(qat, ka, vat)


def _bias_kernel(bucket_ref, relb_ref, o_ref):
    bucket = bucket_ref[...]
    key = lax.broadcasted_iota(jnp.int32, bucket.shape, 0)
    tq = WIN_Q_TILE
    for h in range(HB):
        acc = jnp.full(bucket.shape, NEG, F32)
        for b in range(REL_BUCKETS):
            acc = jnp.where(bucket == b, relb_ref[b * HB + h] * LOG2E, acc)
        g, cols = h // REP, slice((h % REP) * tq, (h % REP + 1) * tq)
        o_ref[0, g, :, cols] = jnp.where(key < WINDOW, NEG, acc)
        o_ref[1, g, :, cols] = acc
        o_ref[2, g, :, cols] = jnp.where(key >= WINDOW + tq, NEG, acc)


def _bias_table(rel_bias):
    c = jnp.arange(2 * WIN_Q_TILE, dtype=jnp.int32)[:, None]
    r = jnp.arange(WIN_Q_TILE, dtype=jnp.int32)[None, :]
    rel = c - WINDOW - r
    nb = REL_BUCKETS // 2
    max_exact = nb // 2
    ret = jnp.where(rel > 0, nb, 0)
    n = jnp.abs(rel)
    nf = jnp.maximum(n, 1).astype(F32)
    large = max_exact + (jnp.log(nf / max_exact) / math.log(REL_MAX_DIST / max_exact)
                         * (nb - max_exact)).astype(jnp.int32)
    large = jnp.minimum(large, nb - 1)
    bucket = ret + jnp.where(n < max_exact, n, large)
    bucket = jnp.where(n <= WINDOW, bucket, -1)
    return pl.pallas_call(
        _bias_kernel,
        in_specs=[pl.BlockSpec(memory_space=pltpu.VMEM), pl.BlockSpec(memory_space=pltpu.SMEM)],
        out_specs=pl.BlockSpec(memory_space=pltpu.VMEM),
        out_shape=jax.ShapeDtypeStruct((3, KVH, 2 * WIN_Q_TILE, REP * WIN_Q_TILE), F32),
        name="rel_bias_table",
    )(bucket, rel_bias.reshape(-1))


def _win_kernel(sink_ref, qt_ref, bias_lo_ref, bias_hi_ref, kp_ref, kc_ref, kn_ref,
                vtp_ref, vtc_ref, vtn_ref, o_ref, s_ref):
    tq, edge = WIN_Q_TILE, WIN_STEP - WINDOW
    chains = [(g, t) for g in range(KVH) for t in range(WIN_STEP // tq)]
    bias_refs = (bias_lo_ref, bias_hi_ref)

    def keys(t):
        return slice(t * tq, t * tq + 2 * tq)

    def scores(slot, g, t):
        k_span = jnp.concatenate([kp_ref[g][edge:], kc_ref[g], kn_ref[g][:WINDOW]], axis=0)
        qt = jnp.concatenate([qt_ref[g * REP + r][:, t * tq:(t + 1) * tq] for r in range(REP)],
                             axis=1)
        s_ref[slot] = _dot(k_span[keys(t)], qt) + bias_refs[t][0, g]

    def consume(slot, g, t):
        s = s_ref[slot]
        sink = jnp.concatenate(
            [jnp.full((1, tq), sink_ref[g * REP + r] * LOG2E, F32) for r in range(REP)], axis=1)
        m = jnp.maximum(jnp.max(s, axis=0, keepdims=True), sink)
        p = jnp.exp2(s - m).astype(BF16)
        sink_p = jnp.exp2(sink - m)
        vt_span = jnp.concatenate([vtp_ref[g][:, edge:], vtc_ref[g], vtn_ref[g][:, :WINDOW]], axis=1)
        halves = []
        for r in range(REP):
            cols = slice(r * tq, (r + 1) * tq)
            pv = _dot(vt_span[:, keys(t)], p[:, cols])
            halves.append(pv[:HD] / (pv[V_DIM:V_DIM + 1] + sink_p[:, cols]))
        o_ref[t * tq:(t + 1) * tq, g * REP * HD:(g + 1) * REP * HD] = (
            jnp.concatenate(halves, axis=0).T.astype(BF16))

    scores(0, *chains[0])
    for n, chain in enumerate(chains):
        if n + 1 < len(chains):
            scores((n + 1) % 2, *chains[n + 1])
        consume(n % 2, *chain)


def _win(sink, qbt, bias, kb, vbt, batch, seq):
    m = kb.shape[1]
    ts = WIN_STEP
    nt = seq // ts

    def neighbour(i, off):
        return jnp.clip(i + off, 0, nt - 1)

    def k_spec(off):
        return pl.BlockSpec((KVH, ts, LANES), lambda b, i: (0, b * nt + neighbour(i, off), 0))

    def vt_spec(off):
        return pl.BlockSpec((KVH, LANES, ts), lambda b, i: (0, 0, b * nt + neighbour(i, off)))

    bias_block = (1, KVH, 2 * WIN_Q_TILE, REP * WIN_Q_TILE)
    return pl.pallas_call(
        _win_kernel,
        grid=(batch, nt),
        in_specs=[
            pl.BlockSpec(memory_space=pltpu.SMEM),
            pl.BlockSpec((HB, LANES, ts), lambda b, i: (0, 0, b * nt + i)),
            pl.BlockSpec(bias_block, lambda b, i: (jnp.where(i == 0, 0, 1), 0, 0, 0)),
            pl.BlockSpec(bias_block, lambda b, i: (jnp.where(i == nt - 1, 2, 1), 0, 0, 0)),
            k_spec(-1), k_spec(0), k_spec(1),
            vt_spec(-1), vt_spec(0), vt_spec(1),
        ],
        out_specs=pl.BlockSpec((ts, HB * HD), lambda b, i: (b * nt + i, 0)),
        out_shape=jax.ShapeDtypeStruct((m, HB * HD), BF16),
        scratch_shapes=[pltpu.VMEM((2, 2 * WIN_Q_TILE, REP * WIN_Q_TILE), F32)],
        compiler_params=_params(2),
        name="window",
    )(sink, qbt, bias, bias, kb, kb, kb, vbt, vbt, vbt)


def _mix_kernel(h_ref, ya_ref, yb_ref, ga_ref, gb_ref, pa_ref, pb_ref, wo_ref, g_ref, o_ref):
    ma = _dot(ya_ref[...], pa_ref[...])
    mb = _dot(yb_ref[...], pb_ref[...])
    mix = (ga_ref[...].astype(F32) * ma + gb_ref[...].astype(F32) * mb).astype(BF16)
    o_ref[...] = h_ref[...] + _rms(_dot(mix, wo_ref[...]), g_ref[...])


def _mix(h, ya, yb, ga, gb, pa, pb, wo, g):
    m = h.shape[0]
    tok = lambda w: pl.BlockSpec((TOK_TILE, w), lambda i: (i, 0))
    return pl.pallas_call(
        _mix_kernel,
        grid=(m // TOK_TILE,),
        in_specs=[tok(D_MODEL), tok(HA * V_DIM), tok(HB * HD), tok(D_MODEL), tok(D_MODEL),
                  _const_spec(pa.shape), _const_spec(pb.shape), _const_spec(wo.shape),
                  _const_spec((1, D_MODEL))],
        out_specs=tok(D_MODEL),
        out_shape=jax.ShapeDtypeStruct((m, D_MODEL), F32),
        compiler_params=_params(1),
        name="mix",
    )(h, ya, yb, ga, gb, pa, pb, wo, g)


def _ple_kernel(h_ref, p_ref, gpre_ref, gpost_ref, wg_ref, wp_ref, o_ref):
    h = h_ref[...]
    gate = jax.nn.sigmoid(_dot(_rms(h, gpre_ref[...]).astype(BF16), wg_ref[...]))
    e = _dot(p_ref[...].astype(BF16), wp_ref[...]) * gate
    o_ref[...] = h + _rms(e, gpost_ref[...])


def _ple(h, p, gpre, gpost, wg, wp):
    m = h.shape[0]
    tok = lambda w: pl.BlockSpec((TOK_TILE, w), lambda i: (i, 0))
    return pl.pallas_call(
        _ple_kernel,
        grid=(m // TOK_TILE,),
        in_specs=[tok(D_MODEL), tok(PLE_DIM), _const_spec((1, D_MODEL)), _const_spec((1, D_MODEL)),
                  _const_spec(wg.shape), _const_spec(wp.shape)],
        out_specs=tok(D_MODEL),
        out_shape=jax.ShapeDtypeStruct((m, D_MODEL), F32),
        compiler_params=_params(1),
        name="ple",
    )(h, p, gpre, gpost, wg, wp)


def _pad_cols(w, width):
    return jnp.pad(w, ((0, 0), (0, width - w.shape[1])))


def _prep_weights(w_in, w_uq, w_uk, w_uv):
    splits = (Q_LORA, KV_LORA, ROPE_DIM, HB * HD, KVH * HD, KVH * HD, D_MODEL, D_MODEL)
    offs = [0]
    for s in splits:
        offs.append(offs[-1] + s)
    w_cq, w_ckv, w_kr, w_qb, w_kb, w_vb, w_ga, w_gb = (
        w_in[:, offs[k]:offs[k + 1]] for k in range(len(splits)))
    r2 = ROPE_DIM // 2
    zeros = lambda n: jnp.zeros((w_in.shape[0], n), w_in.dtype)
    kr_placed = jnp.concatenate([zeros(NOPE_DIM), w_kr, zeros(LANES - NOPE_DIM - ROPE_DIM)], axis=1)
    kr_rot = jnp.concatenate([zeros(NOPE_DIM), -w_kr[:, r2:], w_kr[:, :r2],
                              zeros(LANES - NOPE_DIM - ROPE_DIM)], axis=1)
    heads128 = lambda w, n, scale=1.0: jnp.concatenate(
        [_pad_cols(w[:, h * HD:(h + 1) * HD] * scale, LANES) for h in range(n)], axis=1)
    wnat = jnp.concatenate([w_ckv, kr_placed, kr_rot, heads128(w_kb, KVH), w_ga, w_gb],
                           axis=1).astype(BF16)
    wtr = jnp.concatenate([w_cq, w_ckv, heads128(w_qb, HB, HD ** -0.5), heads128(w_vb, KVH)],
                          axis=1).T.astype(BF16)

    qd = NOPE_DIM + ROPE_DIM
    zq = lambda n: jnp.zeros((Q_LORA, n), w_uq.dtype)
    wq = []
    for h in range(HA):
        wh = w_uq[:, h * qd:(h + 1) * qd]
        nope, x1, x2 = wh[:, :NOPE_DIM], wh[:, NOPE_DIM:NOPE_DIM + r2], wh[:, NOPE_DIM + r2:]
        wq.append(jnp.concatenate([nope, x1, x2, zq(LANES - qd),
                                   zq(NOPE_DIM), -x2, x1, zq(LANES - qd)], axis=1))
    wqt = jnp.swapaxes(jnp.stack(wq), 1, 2).astype(BF16)
    wk = jnp.concatenate([_pad_cols(w_uk[:, h * NOPE_DIM:(h + 1) * NOPE_DIM], LANES)
                          for h in range(HA)], axis=1).astype(BF16)
    wvt = jnp.concatenate([_pad_cols(w_uv[:, h * V_DIM:(h + 1) * V_DIM], LANES)
                           for h in range(HA)], axis=1).T.astype(BF16)
    return wnat, wtr, wqt, wk, wvt


def _rope_tables(seq):
    inv = 1.0 / (ROPE_THETA ** (jnp.arange(0, ROPE_DIM, 2, dtype=F32) / ROPE_DIM))
    ang = jnp.arange(seq, dtype=F32)[:, None] * inv[None, :]
    cos, sin = jnp.cos(ang), jnp.sin(ang)
    pad = jnp.zeros((seq, LANES - NOPE_DIM - ROPE_DIM), F32)
    cos_t = jnp.concatenate([jnp.ones((seq, NOPE_DIM), F32), cos, cos, pad], axis=1)
    sin_t = jnp.concatenate([jnp.zeros((seq, NOPE_DIM), F32), sin, sin, pad], axis=1)
    return cos_t, sin_t, cos_t.T, sin_t.T


def _encoder_layer(x, p, bias, sink, wts):
    batch, seq, _ = x.shape
    m = batch * seq
    x2 = x.reshape(m, D_MODEL)
    p2 = p.reshape(m, PLE_DIM)
    h = _ffn(x2, wts["ffn1_pre_g"], wts["ffn1_post_g"], wts["ffn1_w1"], wts["ffn1_w3"], wts["ffn1_w2"])
    qat, ka, vat, qbt, kb, vbt, ga, gb = _inproj(
        h, wts["mix_pre_g"], _rope_tables(seq), wts["wnat"], wts["wtr"],
        wts["q_norm_gt"], wts["kv_norm_g"], wts["kv_norm_gt"], wts["wqt"], wts["wk"], wts["wvt"], seq)
    ya = _mla(qat, ka, vat, batch, seq)
    yb = _win(sink, qbt, bias, kb, vbt, batch, seq)
    h = _mix(h, ya, yb, ga, gb, wts["w_proj_a"], wts["w_proj_b"], wts["w_out"], wts["mix_post_g"])
    h = _ffn(h, wts["ffn2_pre_g"], wts["ffn2_post_g"], wts["ffn2_w1"], wts["ffn2_w3"], wts["ffn2_w2"])
    out = _ple(h, p2, wts["ple_pre_g"], wts["ple_post_g"], wts["w_ple_gate"], wts["w_ple_proj"])
    return out.reshape(batch, seq, D_MODEL)


def _layer_weights(layer, w):
    wnat, wtr, wqt, wk, wvt = _prep_weights(w["w_in"][layer], w["w_uq"][layer], w["w_uk"][layer],
                                            w["w_uv"][layer])
    wts = {
        "wnat": wnat, "wtr": wtr, "wqt": wqt, "wk": wk, "wvt": wvt,
        "q_norm_gt": w["q_norm_g"][layer][:, None],
        "kv_norm_g": w["kv_norm_g"][layer][None], "kv_norm_gt": w["kv_norm_g"][layer][:, None],
    }
    for name in ("ffn1_pre_g", "ffn1_post_g", "mix_pre_g", "mix_post_g",
                 "ffn2_pre_g", "ffn2_post_g", "ple_pre_g", "ple_post_g"):
        wts[name] = w[name][layer][None]
    for name in ("ffn1_w1", "ffn1_w3", "ffn1_w2", "w_proj_a", "w_proj_b", "w_out",
                 "ffn2_w1", "ffn2_w3", "ffn2_w2", "w_ple_gate", "w_ple_proj"):
        wts[name] = w[name][layer].astype(BF16)
    return wts


def kernel(x_prompt, x_sample, p_prompt, p_sample, rel_bias, ffn1_pre_g, ffn1_post_g, ffn1_w1, ffn1_w3, ffn1_w2, mix_pre_g, mix_post_g, w_in, q_norm_g, kv_norm_g, w_uq, w_uk, w_uv, sink, w_proj_a, w_proj_b, w_out, ffn2_pre_g, ffn2_post_g, ffn2_w1, ffn2_w3, ffn2_w2, ple_pre_g, ple_post_g, w_ple_gate, w_ple_proj):
    w = dict(ffn1_pre_g=ffn1_pre_g, ffn1_post_g=ffn1_post_g, ffn1_w1=ffn1_w1, ffn1_w3=ffn1_w3,
             ffn1_w2=ffn1_w2, mix_pre_g=mix_pre_g, mix_post_g=mix_post_g, w_in=w_in,
             q_norm_g=q_norm_g, kv_norm_g=kv_norm_g, w_uq=w_uq, w_uk=w_uk, w_uv=w_uv,
             w_proj_a=w_proj_a, w_proj_b=w_proj_b, w_out=w_out, ffn2_pre_g=ffn2_pre_g,
             ffn2_post_g=ffn2_post_g, ffn2_w1=ffn2_w1, ffn2_w3=ffn2_w3, ffn2_w2=ffn2_w2,
             ple_pre_g=ple_pre_g, ple_post_g=ple_post_g, w_ple_gate=w_ple_gate,
             w_ple_proj=w_ple_proj)
    bias = _bias_table(rel_bias)
    y_prompt, y_sample = x_prompt, x_sample
    for layer in range(ffn1_w1.shape[0]):
        wts = _layer_weights(layer, w)
        y_prompt = _encoder_layer(y_prompt, p_prompt[layer], bias, sink[layer], wts)
        y_sample = _encoder_layer(y_sample, p_sample[layer], bias, sink[layer], wts)
    return (y_prompt, y_sample)
```

```python
import functools
import math

import jax
import jax.numpy as jnp
from jax import lax
from jax.experimental import pallas as pl
from jax.experimental.pallas import tpu as pltpu

D_MODEL = 1024
D_FF = 2816
PLE_DIM = 256
HA = 8
Q_LORA = 384
KV_LORA = 256
NOPE_DIM = 64
ROPE_DIM = 32
V_DIM = 64
ROPE_THETA = 10000.0
HB = 8
KVH = 2
REP = HB // KVH
HD = 64
WINDOW = 128
REL_BUCKETS = 32
REL_MAX_DIST = 128
EPS = 1e-6
NEG = -1e30

LANES = 128
VMEM_LIMIT = 56 * 1024 * 1024

TOK_TILE = 512
FF_CHUNK = 256
MLA_Q_TILE = TOK_TILE
MLA_KEY_GROUP = 2048
MLA_ITEMS_PER_STEP = 8
WIN_Q_TILE = 2 * WINDOW
WIN_STEP = 2 * WIN_Q_TILE
LOG2E = math.log2(math.e)

F32 = jnp.float32
BF16 = jnp.bfloat16

_T_CQ = 0
_T_CKV = _T_CQ + Q_LORA
_T_KR = _T_CKV + KV_LORA
_T_QB = _T_KR + ROPE_DIM
_T_VB = _T_QB + HB * HD
_T_KB = _T_VB + KVH * HD
_T_END = _T_KB + KVH * HD
QK_DIM = NOPE_DIM + ROPE_DIM


def _params(n_axes):
    return pltpu.CompilerParams(dimension_semantics=("arbitrary",) * n_axes,
                                vmem_limit_bytes=VMEM_LIMIT)


def _const_spec(shape):
    nd = len(shape)
    return pl.BlockSpec(shape, lambda *_: (0,) * nd)


def _rms(x, g, axis=-1):
    return x * lax.rsqrt(jnp.mean(x * x, axis=axis, keepdims=True) + EPS) * g


def _dot(a, b):
    return jnp.dot(a, b, preferred_element_type=F32)


def _with_ones_row(vt):
    row = lax.broadcasted_iota(jnp.int32, vt.shape, 0)
    return jnp.where(row == V_DIM, 1.0, vt)


def _swiglu(xn, w1_ref, w3_ref, w2_ref):
    acc = None
    for c in range(D_FF // FF_CHUNK):
        sl = slice(c * FF_CHUNK, (c + 1) * FF_CHUNK)
        a = _dot(xn, w1_ref[:, sl])
        b = _dot(xn, w3_ref[:, sl])
        act = (a * jax.nn.sigmoid(a) * b).astype(BF16)
        d = _dot(act, w2_ref[sl, :])
        acc = d if acc is None else acc + d
    return acc


def _ffn_kernel(x_ref, gpre_ref, gpost_ref, w1_ref, w3_ref, w2_ref, o_ref):
    x = x_ref[...]
    xn = _rms(x, gpre_ref[...]).astype(BF16)
    f = _swiglu(xn, w1_ref, w3_ref, w2_ref)
    o_ref[...] = x + 0.5 * _rms(f, gpost_ref[...])


def _ffn(x, gpre, gpost, w1, w3, w2):
    m = x.shape[0]
    tok = pl.BlockSpec((TOK_TILE, D_MODEL), lambda i: (i, 0))
    return pl.pallas_call(
        _ffn_kernel,
        grid=(m // TOK_TILE,),
        in_specs=[tok, _const_spec((1, D_MODEL)), _const_spec((1, D_MODEL)),
                  _const_spec(w1.shape), _const_spec(w3.shape), _const_spec(w2.shape)],
        out_specs=tok,
        out_shape=jax.ShapeDtypeStruct((m, D_MODEL), F32),
        compiler_params=_params(1),
        name="ffn",
    )(x, gpre, gpost, w1, w3, w2)


def _pad_rows(x, rows):
    return jnp.concatenate([x, jnp.zeros((rows - x.shape[0], x.shape[1]), x.dtype)], axis=0)


def _rope_t(x, cos_t, sin_t):
    r2 = ROPE_DIM // 2
    x1, x2 = x[:r2], x[r2:]
    return jnp.concatenate([x1 * cos_t - x2 * sin_t, x1 * sin_t + x2 * cos_t], axis=0)


def _inproj_kernel(h_ref, g_ref, rope_ref, wgate_ref, wtr_ref, qngt_ref, kvngt_ref,
                   wqt_ref, wkt_ref, wvt_ref,
                   qat_ref, ka_ref, vat_ref, qbt_ref, kb_ref, vbt_ref, ga_ref, gb_ref):
    u = _rms(h_ref[...], g_ref[...])
    ub = u.astype(BF16)
    ut = u.T.astype(BF16)
    cos_t = rope_ref[:ROPE_DIM // 2]
    sin_t = rope_ref[ROPE_DIM // 2:]

    ga_ref[...] = jax.nn.sigmoid(_dot(ub, wgate_ref[:, :D_MODEL])).astype(BF16)
    gb_ref[...] = jax.nn.sigmoid(_dot(ub, wgate_ref[:, D_MODEL:])).astype(BF16)

    cqt = _rms(_dot(wtr_ref[_T_CQ:_T_CKV, :], ut), qngt_ref[...], axis=0).astype(BF16)
    ckvt = _rms(_dot(wtr_ref[_T_CKV:_T_KR, :], ut), kvngt_ref[...], axis=0).astype(BF16)
    rest = _dot(wtr_ref[_T_KR:_T_END, :], ut)
    off = lambda lo, hi: rest[lo - _T_KR:hi - _T_KR]
    krt = _rope_t(off(_T_KR, _T_QB), cos_t, sin_t)
    qbt = off(_T_QB, _T_VB) * LOG2E
    vbt = off(_T_VB, _T_KB)
    kbt = off(_T_KB, _T_END)
    for hb in range(HB):
        qbt_ref[hb] = _pad_rows(qbt[hb * HD:(hb + 1) * HD], LANES).astype(BF16)
    for g in range(KVH):
        kb_ref[g] = _pad_rows(kbt[g * HD:(g + 1) * HD], LANES).T.astype(BF16)
        vbt_ref[g] = _with_ones_row(_pad_rows(vbt[g * HD:(g + 1) * HD], LANES)).astype(BF16)

    qt_all = _dot(wqt_ref[...], cqt)
    knt_all = _dot(wkt_ref[...], ckvt)
    vt_all = _dot(wvt_ref[...], ckvt)
    qscale = QK_DIM ** -0.5 * LOG2E
    for h in range(HA):
        qh = qt_all[h * QK_DIM:(h + 1) * QK_DIM]
        qt = jnp.concatenate([qh[:NOPE_DIM], _rope_t(qh[NOPE_DIM:], cos_t, sin_t)], axis=0) * qscale
        qat_ref[h] = _pad_rows(qt, LANES).astype(BF16)
        kt = jnp.concatenate([knt_all[h * NOPE_DIM:(h + 1) * NOPE_DIM], krt], axis=0)
        ka_ref[h] = _pad_rows(kt, LANES).T.astype(BF16)
        vat_ref[h, 0] = _with_ones_row(_pad_rows(vt_all[h * V_DIM:(h + 1) * V_DIM], LANES)).astype(BF16)


def _inproj(h, g, rope_t, wgate, wtr, qngt, kvngt, wqt, wkt, wvt, seq):
    m = h.shape[0]
    nt = m // TOK_TILE
    per_seq = seq // TOK_TILE
    per_group = MLA_KEY_GROUP // TOK_TILE
    tok = lambda w: pl.BlockSpec((TOK_TILE, w), lambda i: (i, 0))
    heads = lambda n: pl.BlockSpec((n, TOK_TILE, LANES), lambda i: (0, i, 0))
    heads_t = lambda n: pl.BlockSpec((n, LANES, TOK_TILE), lambda i: (0, 0, i))
    out_shape = (
        jax.ShapeDtypeStruct((HA, LANES, m), BF16),
        jax.ShapeDtypeStruct((HA, m, LANES), BF16),
        jax.ShapeDtypeStruct((HA, m // MLA_KEY_GROUP, LANES, MLA_KEY_GROUP), BF16),
        jax.ShapeDtypeStruct((HB, LANES, m), BF16),
        jax.ShapeDtypeStruct((KVH, m, LANES), BF16),
        jax.ShapeDtypeStruct((KVH, LANES, m), BF16),
        jax.ShapeDtypeStruct((m, D_MODEL), BF16),
        jax.ShapeDtypeStruct((m, D_MODEL), BF16),
    )
    out_specs = (
        heads_t(HA),
        heads(HA),
        pl.BlockSpec((HA, 1, LANES, TOK_TILE), lambda i: (0, i // per_group, 0, i % per_group)),
        heads_t(HB),
        heads(KVH),
        heads_t(KVH),
        tok(D_MODEL),
        tok(D_MODEL),
    )
    return pl.pallas_call(
        _inproj_kernel,
        grid=(nt,),
        in_specs=[tok(D_MODEL), _const_spec((1, D_MODEL)),
                  pl.BlockSpec((ROPE_DIM, TOK_TILE), lambda i: (0, i % per_seq)),
                  _const_spec(wgate.shape), _const_spec(wtr.shape),
                  _const_spec((Q_LORA, 1)), _const_spec((KV_LORA, 1)),
                  _const_spec(wqt.shape), _const_spec(wkt.shape), _const_spec(wvt.shape)],
        out_specs=out_specs,
        out_shape=out_shape,
        compiler_params=_params(1),
        name="inproj",
    )(h, g, rope_t, wgate, wtr, qngt, kvngt, wqt, wkt, wvt)


def _mla_kernel(qt_ref, k_ref, vt_ref, o_ref, s_ref, m_ref, *, n_groups, q_tiles):
    gk, tq = MLA_KEY_GROUP, MLA_Q_TILE

    def scores(slot, qi, j, g):
        start = g * gk if isinstance(g, int) else pl.multiple_of(g * gk, gk)
        s = _dot(k_ref[j, pl.ds(start, gk), :], qt_ref[j, :, qi * tq:(qi + 1) * tq])
        s_ref[slot] = s
        m_ref[slot] = jnp.max(s, axis=0, keepdims=True)

    def consume(slot, j, g, carry):
        m_old, acc = carry
        m_new = jnp.maximum(m_old, m_ref[slot])
        p = jnp.exp2(s_ref[slot] - m_new).astype(BF16)
        return m_new, jnp.exp2(m_old - m_new) * acc + _dot(vt_ref[j, g], p)

    def init():
        return jnp.full((1, tq), -jnp.inf, F32), jnp.zeros((LANES, tq), F32)

    def finish(carry):
        acc = carry[1]
        return acc[:V_DIM] / acc[V_DIM:V_DIM + 1]

    if n_groups == 1:
        items = [(qi, j) for qi in range(q_tiles) for j in range(2)]
        scores(0, *items[0], 0)
        halves = []
        for n, (qi, j) in enumerate(items):
            if n + 1 < len(items):
                scores((n + 1) % 2, *items[n + 1], 0)
            halves.append(finish(consume(n % 2, j, 0, init())))
            if j == 1:
                o_ref[qi * tq:(qi + 1) * tq, :] = jnp.concatenate(halves, axis=0).T.astype(BF16)
                halves = []
    else:
        halves = []
        for j in range(2):
            def pair(t, carry, j=j):
                scores(1, 0, j, 2 * t + 1)
                carry = consume(0, j, 2 * t, carry)
                scores(0, 0, j, 2 * t + 2)
                return consume(1, j, 2 * t + 1, carry)

            scores(0, 0, j, 0)
            carry = lax.fori_loop(0, n_groups // 2 - 1, pair, init())
            scores(1, 0, j, n_groups - 1)
            carry = consume(0, j, n_groups - 2, carry)
            halves.append(finish(consume(1, j, n_groups - 1, carry)))
        o_ref[...] = jnp.concatenate(halves, axis=0).T.astype(BF16)


def _mla(qat, ka, vat, batch, seq):
    m = ka.shape[1]
    n_groups = seq // MLA_KEY_GROUP
    assert n_groups == 1 or n_groups % 2 == 0
    q_tiles = MLA_ITEMS_PER_STEP // 2 if n_groups == 1 else 1
    tqs = q_tiles * MLA_Q_TILE
    nq = seq // tqs
    return pl.pallas_call(
        functools.partial(_mla_kernel, n_groups=n_groups, q_tiles=q_tiles),
        grid=(batch, HA // 2, nq),
        in_specs=[
            pl.BlockSpec((2, LANES, tqs), lambda b, hp, i: (hp, 0, b * nq + i)),
            pl.BlockSpec((2, seq, LANES), lambda b, hp, i: (hp, b, 0)),
            pl.BlockSpec((2, n_groups, LANES, MLA_KEY_GROUP), lambda b, hp, i: (hp, b, 0, 0)),
        ],
        out_specs=pl.BlockSpec((tqs, LANES), lambda b, hp, i: (b * nq + i, hp)),
        out_shape=jax.ShapeDtypeStruct((m, HA * V_DIM), BF16),
        scratch_shapes=[pltpu.VMEM((2, MLA_KEY_GROUP, MLA_Q_TILE), F32),
                        pltpu.VMEM((2, 1, MLA_Q_TILE), F32)],
        compiler_params=_params(3),
        name="mla",
    )(qat, ka, vat)


def _bias_kernel(bucket_ref, relb_ref, o_ref):
    bucket = bucket_ref[...]
    key = lax.broadcasted_iota(jnp.int32, bucket.shape, 0)
    tq = WIN_Q_TILE
    for h in range(HB):
        acc = jnp.full(bucket.shape, NEG, F32)
        for b in range(REL_BUCKETS):
            acc = jnp.where(bucket == b, relb_ref[b * HB + h] * LOG2E, acc)
        g, cols = h // REP, slice((h % REP) * tq, (h % REP + 1) * tq)
        o_ref[0, g, :, cols] = jnp.where(key < WINDOW, NEG, acc)
        o_ref[1, g, :, cols] = acc
        o_ref[2, g, :, cols] = jnp.where(key >= WINDOW + tq, NEG, acc)


def _bias_table(rel_bias):
    c = jnp.arange(2 * WIN_Q_TILE, dtype=jnp.int32)[:, None]
    r = jnp.arange(WIN_Q_TILE, dtype=jnp.int32)[None, :]
    rel = c - WINDOW - r
    nb = REL_BUCKETS // 2
    max_exact = nb // 2
    ret = jnp.where(rel > 0, nb, 0)
    n = jnp.abs(rel)
    nf = jnp.maximum(n, 1).astype(F32)
    large = max_exact + (jnp.log(nf / max_exact) / math.log(REL_MAX_DIST / max_exact)
                         * (nb - max_exact)).astype(jnp.int32)
    large = jnp.minimum(large, nb - 1)
    bucket = ret + jnp.where(n < max_exact, n, large)
    bucket = jnp.where(n <= WINDOW, bucket, -1)
    return pl.pallas_call(
        _bias_kernel,
        in_specs=[pl.BlockSpec(memory_space=pltpu.VMEM), pl.BlockSpec(memory_space=pltpu.SMEM)],
        out_specs=pl.BlockSpec(memory_space=pltpu.VMEM),
        out_shape=jax.ShapeDtypeStruct((3, KVH, 2 * WIN_Q_TILE, REP * WIN_Q_TILE), F32),
        name="rel_bias_table",
    )(bucket, rel_bias.reshape(-1))


def _win_kernel(sink_ref, qt_ref, bias_lo_ref, bias_hi_ref, kp_ref, kc_ref, kn_ref,
                vtp_ref, vtc_ref, vtn_ref, o_ref, s_ref):
    tq, edge = WIN_Q_TILE, WIN_STEP - WINDOW
    chains = [(g, t) for g in range(KVH) for t in range(WIN_STEP // tq)]
    bias_refs = (bias_lo_ref, bias_hi_ref)

    def keys(t):
        return slice(t * tq, t * tq + 2 * tq)

    def scores(slot, g, t):
        k_span = jnp.concatenate([kp_ref[g][edge:], kc_ref[g], kn_ref[g][:WINDOW]], axis=0)
        qt = jnp.concatenate([qt_ref[g * REP + r][:, t * tq:(t + 1) * tq] for r in range(REP)],
                             axis=1)
        s_ref[slot] = _dot(k_span[keys(t)], qt) + bias_refs[t][0, g]

    def consume(slot, g, t):
        s = s_ref[slot]
        sink = jnp.concatenate(
            [jnp.full((1, tq), sink_ref[g * REP + r] * LOG2E, F32) for r in range(REP)], axis=1)
        m = jnp.maximum(jnp.max(s, axis=0, keepdims=True), sink)
        p = jnp.exp2(s - m).astype(BF16)
        sink_p = jnp.exp2(sink - m)
        vt_span = jnp.concatenate([vtp_ref[g][:, edge:], vtc_ref[g], vtn_ref[g][:, :WINDOW]], axis=1)
        halves = []
        for r in range(REP):
            cols = slice(r * tq, (r + 1) * tq)
            pv = _dot(vt_span[:, keys(t)], p[:, cols])
            halves.append(pv[:HD] / (pv[V_DIM:V_DIM + 1] + sink_p[:, cols]))
        o_ref[t * tq:(t + 1) * tq, g * REP * HD:(g + 1) * REP * HD] = (
            jnp.concatenate(halves, axis=0).T.astype(BF16))

    scores(0, *chains[0])
    for n, chain in enumerate(chains):
        if n + 1 < len(chains):
            scores((n + 1) % 2, *chains[n + 1])
        consume(n % 2, *chain)


def _win(sink, qbt, bias, kb, vbt, batch, seq):
    m = kb.shape[1]
    ts = WIN_STEP
    nt = seq // ts

    def neighbour(i, off):
        return jnp.clip(i + off, 0, nt - 1)

    def k_spec(off):
        return pl.BlockSpec((KVH, ts, LANES), lambda b, i: (0, b * nt + neighbour(i, off), 0))

    def vt_spec(off):
        return pl.BlockSpec((KVH, LANES, ts), lambda b, i: (0, 0, b * nt + neighbour(i, off)))

    bias_block = (1, KVH, 2 * WIN_Q_TILE, REP * WIN_Q_TILE)
    return pl.pallas_call(
        _win_kernel,
        grid=(batch, nt),
        in_specs=[
            pl.BlockSpec(memory_space=pltpu.SMEM),
            pl.BlockSpec((HB, LANES, ts), lambda b, i: (0, 0, b * nt + i)),
            pl.BlockSpec(bias_block, lambda b, i: (jnp.where(i == 0, 0, 1), 0, 0, 0)),
            pl.BlockSpec(bias_block, lambda b, i: (jnp.where(i == nt - 1, 2, 1), 0, 0, 0)),
            k_spec(-1), k_spec(0), k_spec(1),
            vt_spec(-1), vt_spec(0), vt_spec(1),
        ],
        out_specs=pl.BlockSpec((ts, HB * HD), lambda b, i: (b * nt + i, 0)),
        out_shape=jax.ShapeDtypeStruct((m, HB * HD), BF16),
        scratch_shapes=[pltpu.VMEM((2, 2 * WIN_Q_TILE, REP * WIN_Q_TILE), F32)],
        compiler_params=_params(2),
        name="window",
    )(sink, qbt, bias, bias, kb, kb, kb, vbt, vbt, vbt)


def _tail_kernel(h_ref, ya_ref, yb_ref, ga_ref, gb_ref, p_ref, gains_ref,
                 pa_ref, pb_ref, wo_ref, w1_ref, w3_ref, w2_ref, wg_ref, wp_ref, o_ref):
    g_mix, g_pre2, g_post2, g_pre_e, g_post_e = (gains_ref[k:k + 1] for k in range(5))
    ma = _dot(ya_ref[...], pa_ref[...])
    mb = _dot(yb_ref[...], pb_ref[...])
    mix = (ga_ref[...].astype(F32) * ma + gb_ref[...].astype(F32) * mb).astype(BF16)
    h = h_ref[...] + _rms(_dot(mix, wo_ref[...]), g_mix)
    f = _swiglu(_rms(h, g_pre2).astype(BF16), w1_ref, w3_ref, w2_ref)
    h = h + 0.5 * _rms(f, g_post2)
    gate = jax.nn.sigmoid(_dot(_rms(h, g_pre_e).astype(BF16), wg_ref[...]))
    e = _dot(p_ref[...].astype(BF16), wp_ref[...]) * gate
    o_ref[...] = h + _rms(e, g_post_e)


def _tail(h, ya, yb, ga, gb, p, gains, pa, pb, wo, w1, w3, w2, wg, wp):
    m = h.shape[0]
    tok = lambda w: pl.BlockSpec((TOK_TILE, w), lambda i: (i, 0))
    resident = lambda a: pl.BlockSpec(a.shape, lambda i: (0,) * a.ndim, pipeline_mode=pl.Buffered(1))
    weights = (pa, pb, wo, w1, w3, w2, wg, wp)
    return pl.pallas_call(
        _tail_kernel,
        grid=(m // TOK_TILE,),
        in_specs=[tok(D_MODEL), tok(HA * V_DIM), tok(HB * HD), tok(D_MODEL), tok(D_MODEL),
                  tok(PLE_DIM), _const_spec(gains.shape)] + [resident(a) for a in weights],
        out_specs=tok(D_MODEL),
        out_shape=jax.ShapeDtypeStruct((m, D_MODEL), F32),
        compiler_params=_params(1),
        name="tail",
    )(h, ya, yb, ga, gb, p, gains, *weights)


def _prep_weights(w_in, w_uq, w_uk, w_uv):
    splits = (Q_LORA, KV_LORA, ROPE_DIM, HB * HD, KVH * HD, KVH * HD, D_MODEL, D_MODEL)
    offs = [0]
    for s in splits:
        offs.append(offs[-1] + s)
    w_cq, w_ckv, w_kr, w_qb, w_kb, w_vb, w_ga, w_gb = (
        w_in[:, offs[k]:offs[k + 1]] for k in range(len(splits)))
    wgate = jnp.concatenate([w_ga, w_gb], axis=1).astype(BF16)
    wtr = jnp.concatenate([w_cq, w_ckv, w_kr, w_qb * HD ** -0.5, w_vb, w_kb],
                          axis=1).T.astype(BF16)
    return wgate, wtr, w_uq.T.astype(BF16), w_uk.T.astype(BF16), w_uv.T.astype(BF16)


def _rope_table_t(seq):
    inv = 1.0 / (ROPE_THETA ** (jnp.arange(0, ROPE_DIM, 2, dtype=F32) / ROPE_DIM))
    ang = inv[:, None] * jnp.arange(seq, dtype=F32)[None, :]
    return jnp.concatenate([jnp.cos(ang), jnp.sin(ang)], axis=0)


def _encoder_layer(x, p, bias, sink, wts):
    batch, seq, _ = x.shape
    m = batch * seq
    x2 = x.reshape(m, D_MODEL)
    p2 = p.reshape(m, PLE_DIM)
    h = _ffn(x2, wts["ffn1_pre_g"], wts["ffn1_post_g"], wts["ffn1_w1"], wts["ffn1_w3"], wts["ffn1_w2"])
    qat, ka, vat, qbt, kb, vbt, ga, gb = _inproj(
        h, wts["mix_pre_g"], _rope_table_t(seq), wts["wgate"], wts["wtr"],
        wts["q_norm_gt"], wts["kv_norm_gt"], wts["wqt"], wts["wkt"], wts["wvt"], seq)
    ya = _mla(qat, ka, vat, batch, seq)
    yb = _win(sink, qbt, bias, kb, vbt, batch, seq)
    gains = jnp.concatenate([wts["mix_post_g"], wts["ffn2_pre_g"], wts["ffn2_post_g"],
                             wts["ple_pre_g"], wts["ple_post_g"]], axis=0)
    out = _tail(h, ya, yb, ga, gb, p2, gains, wts["w_proj_a"], wts["w_proj_b"], wts["w_out"],
                wts["ffn2_w1"], wts["ffn2_w3"], wts["ffn2_w2"], wts["w_ple_gate"], wts["w_ple_proj"])
    return out.reshape(batch, seq, D_MODEL)


def _layer_weights(layer, w):
    wgate, wtr, wqt, wkt, wvt = _prep_weights(w["w_in"][layer], w["w_uq"][layer], w["w_uk"][layer],
                                              w["w_uv"][layer])
    wts = {
        "wgate": wgate, "wtr": wtr, "wqt": wqt, "wkt": wkt, "wvt": wvt,
        "q_norm_gt": w["q_norm_g"][layer][:, None], "kv_norm_gt": w["kv_norm_g"][layer][:, None],
    }
    for name in ("ffn1_pre_g", "ffn1_post_g", "mix_pre_g", "mix_post_g",
                 "ffn2_pre_g", "ffn2_post_g", "ple_pre_g", "ple_post_g"):
        wts[name] = w[name][layer][None]
    for name in ("ffn1_w1", "ffn1_w3", "ffn1_w2", "w_proj_a", "w_proj_b", "w_out",
                 "ffn2_w1", "ffn2_w3", "ffn2_w2", "w_ple_gate", "w_ple_proj"):
        wts[name] = w[name][layer].astype(BF16)
    return wts


def kernel(x_prompt, x_sample, p_prompt, p_sample, rel_bias, ffn1_pre_g, ffn1_post_g, ffn1_w1, ffn1_w3, ffn1_w2, mix_pre_g, mix_post_g, w_in, q_norm_g, kv_norm_g, w_uq, w_uk, w_uv, sink, w_proj_a, w_proj_b, w_out, ffn2_pre_g, ffn2_post_g, ffn2_w1, ffn2_w3, ffn2_w2, ple_pre_g, ple_post_g, w_ple_gate, w_ple_proj):
    w = dict(ffn1_pre_g=ffn1_pre_g, ffn1_post_g=ffn1_post_g, ffn1_w1=ffn1_w1, ffn1_w3=ffn1_w3,
             ffn1_w2=ffn1_w2, mix_pre_g=mix_pre_g, mix_post_g=mix_post_g, w_in=w_in,
             q_norm_g=q_norm_g, kv_norm_g=kv_norm_g, w_uq=w_uq, w_uk=w_uk, w_uv=w_uv,
             w_proj_a=w_proj_a, w_proj_b=w_proj_b, w_out=w_out, ffn2_pre_g=ffn2_pre_g,
             ffn2_post_g=ffn2_post_g, ffn2_w1=ffn2_w1, ffn2_w3=ffn2_w3, ffn2_w2=ffn2_w2,
             ple_pre_g=ple_pre_g, ple_post_g=ple_post_g, w_ple_gate=w_ple_gate,
             w_ple_proj=w_ple_proj)
    bias = _bias_table(rel_bias)
    y_prompt, y_sample = x_prompt, x_sample
    for layer in range(ffn1_w1.shape[0]):
        wts = _layer_weights(layer, w)
        y_prompt = _encoder_layer(y_prompt, p_prompt[layer], bias, sink[layer], wts)
        y_sample = _encoder_layer(y_sample, p_sample[layer], bias, sink[layer], wts)
    return (y_prompt, y_sample)
```

```python
import functools
import math

import jax
import jax.numpy as jnp
from jax import lax
from jax.experimental import pallas as pl
from jax.experimental.pallas import tpu as pltpu

D_MODEL = 1024
D_FF = 2816
PLE_DIM = 256
HA = 8
Q_LORA = 384
KV_LORA = 256
NOPE_DIM = 64
ROPE_DIM = 32
V_DIM = 64
ROPE_THETA = 10000.0
HB = 8
KVH = 2
REP = HB // KVH
HD = 64
WINDOW = 128
REL_BUCKETS = 32
REL_MAX_DIST = 128
EPS = 1e-6
NEG = -1e30

LANES = 128
VMEM_LIMIT = 56 * 1024 * 1024

TOK_TILE = 512
FF_CHUNK = 256
MLA_Q_TILE = TOK_TILE
MLA_KEY_GROUP = 2048
MLA_SUB = 8
MLA_ITEMS_PER_STEP = 8
WIN_Q_TILE = 2 * WINDOW
WIN_STEP = 2 * WIN_Q_TILE
LOG2E = math.log2(math.e)

F32 = jnp.float32
BF16 = jnp.bfloat16

_T_CQ = 0
_T_CKV = _T_CQ + Q_LORA
_T_KR = _T_CKV + KV_LORA
_T_QB = _T_KR + ROPE_DIM
_T_VB = _T_QB + HB * HD
_T_KB = _T_VB + KVH * HD
_T_END = _T_KB + KVH * HD
QK_DIM = NOPE_DIM + ROPE_DIM


def _params(n_axes):
    return pltpu.CompilerParams(dimension_semantics=("arbitrary",) * n_axes,
                                vmem_limit_bytes=VMEM_LIMIT)


def _const_spec(shape):
    nd = len(shape)
    return pl.BlockSpec(shape, lambda *_: (0,) * nd)


def _rms(x, g, axis=-1):
    return x * lax.rsqrt(jnp.mean(x * x, axis=axis, keepdims=True) + EPS) * g


def _dot(a, b):
    return jnp.dot(a, b, preferred_element_type=F32)


def _with_ones_row(vt):
    row = lax.broadcasted_iota(jnp.int32, vt.shape, 0)
    return jnp.where(row == V_DIM, 1.0, vt)


def _swiglu(xn, w1_ref, w3_ref, w2_ref):
    acc = None
    for c in range(D_FF // FF_CHUNK):
        sl = slice(c * FF_CHUNK, (c + 1) * FF_CHUNK)
        a = _dot(xn, w1_ref[:, sl])
        b = _dot(xn, w3_ref[:, sl])
        act = (a * jax.nn.sigmoid(a) * b).astype(BF16)
        d = _dot(act, w2_ref[sl, :])
        acc = d if acc is None else acc + d
    return acc


def _ffn_kernel(x_ref, gpre_ref, gpost_ref, w1_ref, w3_ref, w2_ref, o_ref):
    x = x_ref[...]
    xn = _rms(x, gpre_ref[...]).astype(BF16)
    f = _swiglu(xn, w1_ref, w3_ref, w2_ref)
    o_ref[...] = x + 0.5 * _rms(f, gpost_ref[...])


def _ffn(x, gpre, gpost, w1, w3, w2):
    m = x.shape[0]
    tok = pl.BlockSpec((TOK_TILE, D_MODEL), lambda i: (i, 0))
    return pl.pallas_call(
        _ffn_kernel,
        grid=(m // TOK_TILE,),
        in_specs=[tok, _const_spec((1, D_MODEL)), _const_spec((1, D_MODEL)),
                  _const_spec(w1.shape), _const_spec(w3.shape), _const_spec(w2.shape)],
        out_specs=tok,
        out_shape=jax.ShapeDtypeStruct((m, D_MODEL), F32),
        compiler_params=_params(1),
        name="ffn",
    )(x, gpre, gpost, w1, w3, w2)


def _pad_rows(x, rows):
    return jnp.concatenate([x, jnp.zeros((rows - x.shape[0], x.shape[1]), x.dtype)], axis=0)


def _rope_t(x, cos_t, sin_t):
    r2 = ROPE_DIM // 2
    x1, x2 = x[:r2], x[r2:]
    return jnp.concatenate([x1 * cos_t - x2 * sin_t, x1 * sin_t + x2 * cos_t], axis=0)


def _inproj_kernel(h_ref, g_ref, rope_ref, wgate_ref, wtr_ref, qngt_ref, kvngt_ref,
                   wqt_ref, wkt_ref, wvt_ref,
                   qat_ref, ka_ref, vat_ref, qbt_ref, kb_ref, vbt_ref, ga_ref, gb_ref):
    u = _rms(h_ref[...], g_ref[...])
    ub = u.astype(BF16)
    ut = u.T.astype(BF16)
    cos_t = rope_ref[:ROPE_DIM // 2]
    sin_t = rope_ref[ROPE_DIM // 2:]

    ga_ref[...] = jax.nn.sigmoid(_dot(ub, wgate_ref[:, :D_MODEL])).astype(BF16)
    gb_ref[...] = jax.nn.sigmoid(_dot(ub, wgate_ref[:, D_MODEL:])).astype(BF16)

    cqt = _rms(_dot(wtr_ref[_T_CQ:_T_CKV, :], ut), qngt_ref[...], axis=0).astype(BF16)
    ckvt = _rms(_dot(wtr_ref[_T_CKV:_T_KR, :], ut), kvngt_ref[...], axis=0).astype(BF16)
    rest = _dot(wtr_ref[_T_KR:_T_END, :], ut)
    off = lambda lo, hi: rest[lo - _T_KR:hi - _T_KR]
    krt = _rope_t(off(_T_KR, _T_QB), cos_t, sin_t)
    qbt = off(_T_QB, _T_VB) * LOG2E
    vbt = off(_T_VB, _T_KB)
    kbt = off(_T_KB, _T_END)
    for hb in range(HB):
        qbt_ref[hb] = _pad_rows(qbt[hb * HD:(hb + 1) * HD], LANES).astype(BF16)
    for g in range(KVH):
        kb_ref[g] = _pad_rows(kbt[g * HD:(g + 1) * HD], LANES).T.astype(BF16)
        vbt_ref[g] = _with_ones_row(_pad_rows(vbt[g * HD:(g + 1) * HD], LANES)).astype(BF16)

    qt_all = _dot(wqt_ref[...], cqt)
    knt_all = _dot(wkt_ref[...], ckvt)
    vt_all = _dot(wvt_ref[...], ckvt)
    qscale = QK_DIM ** -0.5 * LOG2E
    for h in range(HA):
        qh = qt_all[h * QK_DIM:(h + 1) * QK_DIM]
        qt = jnp.concatenate([qh[:NOPE_DIM], _rope_t(qh[NOPE_DIM:], cos_t, sin_t)], axis=0) * qscale
        qat_ref[h] = _pad_rows(qt, LANES).astype(BF16)
        kt = jnp.concatenate([knt_all[h * NOPE_DIM:(h + 1) * NOPE_DIM], krt], axis=0)
        ka_ref[h] = _pad_rows(kt, LANES).T.astype(BF16)
        vat_ref[h, 0] = _with_ones_row(_pad_rows(vt_all[h * V_DIM:(h + 1) * V_DIM], LANES)).astype(BF16)


def _inproj(h, g, rope_t, wgate, wtr, qngt, kvngt, wqt, wkt, wvt, seq):
    m = h.shape[0]
    nt = m // TOK_TILE
    per_seq = seq // TOK_TILE
    per_group = MLA_KEY_GROUP // TOK_TILE
    tok = lambda w: pl.BlockSpec((TOK_TILE, w), lambda i: (i, 0))
    heads = lambda n: pl.BlockSpec((n, TOK_TILE, LANES), lambda i: (0, i, 0))
    heads_t = lambda n: pl.BlockSpec((n, LANES, TOK_TILE), lambda i: (0, 0, i))
    out_shape = (
        jax.ShapeDtypeStruct((HA, LANES, m), BF16),
        jax.ShapeDtypeStruct((HA, m, LANES), BF16),
        jax.ShapeDtypeStruct((HA, m // MLA_KEY_GROUP, LANES, MLA_KEY_GROUP), BF16),
        jax.ShapeDtypeStruct((HB, LANES, m), BF16),
        jax.ShapeDtypeStruct((KVH, m, LANES), BF16),
        jax.ShapeDtypeStruct((KVH, LANES, m), BF16),
        jax.ShapeDtypeStruct((m, D_MODEL), BF16),
        jax.ShapeDtypeStruct((m, D_MODEL), BF16),
    )
    out_specs = (
        heads_t(HA),
        heads(HA),
        pl.BlockSpec((HA, 1, LANES, TOK_TILE), lambda i: (0, i // per_group, 0, i % per_group)),
        heads_t(HB),
        heads(KVH),
        heads_t(KVH),
        tok(D_MODEL),
        tok(D_MODEL),
    )
    return pl.pallas_call(
        _inproj_kernel,
        grid=(nt,),
        in_specs=[tok(D_MODEL), _const_spec((1, D_MODEL)),
                  pl.BlockSpec((ROPE_DIM, TOK_TILE), lambda i: (0, i % per_seq)),
                  _const_spec(wgate.shape), _const_spec(wtr.shape),
                  _const_spec((Q_LORA, 1)), _const_spec((KV_LORA, 1)),
                  _const_spec(wqt.shape), _const_spec(wkt.shape), _const_spec(wvt.shape)],
        out_specs=out_specs,
        out_shape=out_shape,
        compiler_params=_params(1),
        name="inproj",
    )(h, g, rope_t, wgate, wtr, qngt, kvngt, wqt, wkt, wvt)


def _mla_kernel(qt_ref, k_ref, vt_ref, o_ref, s_ref, m_ref, *, n_groups, q_tiles):
    sub = MLA_SUB if n_groups > 1 else 1
    gk, tq, ck = MLA_KEY_GROUP, MLA_Q_TILE, MLA_KEY_GROUP // sub

    def step(nxt, cur, carry):
        if cur is not None:
            cslot, cj, cg = cur
            m_old, acc = carry
            m_new = jnp.maximum(m_old, m_ref[cslot])
        pv = m_nxt = None
        for c in range(sub):
            rows = slice(c * ck, (c + 1) * ck)
            if nxt is not None:
                nslot, qi, nj, ng = nxt
                base = ng * gk + c * ck
                start = base if isinstance(base, int) else pl.multiple_of(base, ck)
                s = _dot(k_ref[nj, pl.ds(start, ck), :], qt_ref[nj, :, qi * tq:(qi + 1) * tq])
                s_ref[nslot, rows, :] = s
                mc = jnp.max(s, axis=0, keepdims=True)
                m_nxt = mc if m_nxt is None else jnp.maximum(m_nxt, mc)
            if cur is not None:
                p = jnp.exp2(s_ref[cslot, rows, :] - m_new).astype(BF16)
                d = _dot(vt_ref[cj, cg, :, rows], p)
                pv = d if pv is None else pv + d
        if nxt is not None:
            m_ref[nslot] = m_nxt
        if cur is not None:
            return m_new, jnp.exp2(m_old - m_new) * acc + pv

    def init():
        return jnp.full((1, tq), -jnp.inf, F32), jnp.zeros((LANES, tq), F32)

    def finish(carry):
        acc = carry[1]
        return acc[:V_DIM] / acc[V_DIM:V_DIM + 1]

    if n_groups == 1:
        items = [(qi, j) for qi in range(q_tiles) for j in range(2)]
        step((0, *items[0], 0), None, None)
        halves = []
        for n, (qi, j) in enumerate(items):
            nxt = ((n + 1) % 2, *items[n + 1], 0) if n + 1 < len(items) else None
            halves.append(finish(step(nxt, (n % 2, j, 0), init())))
            if j == 1:
                o_ref[qi * tq:(qi + 1) * tq, :] = jnp.concatenate(halves, axis=0).T.astype(BF16)
                halves = []
    else:
        halves = []
        step((0, 0, 0, 0), None, None)
        for j in range(2):
            def pair(t, carry, j=j):
                carry = step((1, 0, j, 2 * t + 1), (0, j, 2 * t), carry)
                return step((0, 0, j, 2 * t + 2), (1, j, 2 * t + 1), carry)

            carry = lax.fori_loop(0, n_groups // 2 - 1, pair, init())
            carry = step((1, 0, j, n_groups - 1), (0, j, n_groups - 2), carry)
            nxt = (0, 0, 1, 0) if j == 0 else None
            halves.append(finish(step(nxt, (1, j, n_groups - 1), carry)))
        o_ref[...] = jnp.concatenate(halves, axis=0).T.astype(BF16)


def _mla(qat, ka, vat, batch, seq):
    m = ka.shape[1]
    n_groups = seq // MLA_KEY_GROUP
    assert n_groups == 1 or n_groups % 2 == 0
    q_tiles = MLA_ITEMS_PER_STEP // 2 if n_groups == 1 else 1
    tqs = q_tiles * MLA_Q_TILE
    nq = seq // tqs
    return pl.pallas_call(
        functools.partial(_mla_kernel, n_groups=n_groups, q_tiles=q_tiles),
        grid=(batch, HA // 2, nq),
        in_specs=[
            pl.BlockSpec((2, LANES, tqs), lambda b, hp, i: (hp, 0, b * nq + i)),
            pl.BlockSpec((2, seq, LANES), lambda b, hp, i: (hp, b, 0)),
            pl.BlockSpec((2, n_groups, LANES, MLA_KEY_GROUP), lambda b, hp, i: (hp, b, 0, 0)),
        ],
        out_specs=pl.BlockSpec((tqs, LANES), lambda b, hp, i: (b * nq + i, hp)),
        out_shape=jax.ShapeDtypeStruct((m, HA * V_DIM), BF16),
        scratch_shapes=[pltpu.VMEM((2, MLA_KEY_GROUP, MLA_Q_TILE), F32),
                        pltpu.VMEM((2, 1, MLA_Q_TILE), F32)],
        compiler_params=_params(3),
        name="mla",
    )(qat, ka, vat)


def _bias_kernel(bucket_ref, relb_ref, o_ref):
    bucket = bucket_ref[...]
    key = lax.broadcasted_iota(jnp.int32, bucket.shape, 0)
    tq = WIN_Q_TILE
    for h in range(HB):
        acc = jnp.full(bucket.shape, NEG, F32)
        for b in range(REL_BUCKETS):
            acc = jnp.where(bucket == b, relb_ref[b * HB + h] * LOG2E, acc)
        g, cols = h // REP, slice((h % REP) * tq, (h % REP + 1) * tq)
        o_ref[0, g, :, cols] = jnp.where(key < WINDOW, NEG, acc)
        o_ref[1, g, :, cols] = acc
        o_ref[2, g, :, cols] = jnp.where(key >= WINDOW + tq, NEG, acc)


def _bias_table(rel_bias):
    c = jnp.arange(2 * WIN_Q_TILE, dtype=jnp.int32)[:, None]
    r = jnp.arange(WIN_Q_TILE, dtype=jnp.int32)[None, :]
    rel = c - WINDOW - r
    nb = REL_BUCKETS // 2
    max_exact = nb // 2
    ret = jnp.where(rel > 0, nb, 0)
    n = jnp.abs(rel)
    nf = jnp.maximum(n, 1).astype(F32)
    large = max_exact + (jnp.log(nf / max_exact) / math.log(REL_MAX_DIST / max_exact)
                         * (nb - max_exact)).astype(jnp.int32)
    large = jnp.minimum(large, nb - 1)
    bucket = ret + jnp.where(n < max_exact, n, large)
    bucket = jnp.where(n <= WINDOW, bucket, -1)
    return pl.pallas_call(
        _bias_kernel,
        in_specs=[pl.BlockSpec(memory_space=pltpu.VMEM), pl.BlockSpec(memory_space=pltpu.SMEM)],
        out_specs=pl.BlockSpec(memory_space=pltpu.VMEM),
        out_shape=jax.ShapeDtypeStruct((3, KVH, 2 * WIN_Q_TILE, REP * WIN_Q_TILE), F32),
        name="rel_bias_table",
    )(bucket, rel_bias.reshape(-1))


def _win_kernel(sink_ref, qt_ref, bias_lo_ref, bias_hi_ref, kp_ref, kc_ref, kn_ref,
                vtp_ref, vtc_ref, vtn_ref, o_ref, s_ref):
    tq, edge = WIN_Q_TILE, WIN_STEP - WINDOW
    chains = [(g, t) for g in range(KVH) for t in range(WIN_STEP // tq)]
    bias_refs = (bias_lo_ref, bias_hi_ref)

    def keys(t):
        return slice(t * tq, t * tq + 2 * tq)

    def scores(slot, g, t):
        k_span = jnp.concatenate([kp_ref[g][edge:], kc_ref[g], kn_ref[g][:WINDOW]], axis=0)
        qt = jnp.concatenate([qt_ref[g * REP + r][:, t * tq:(t + 1) * tq] for r in range(REP)],
                             axis=1)
        s_ref[slot] = _dot(k_span[keys(t)], qt) + bias_refs[t][0, g]

    def consume(slot, g, t):
        s = s_ref[slot]
        sink = jnp.concatenate(
            [jnp.full((1, tq), sink_ref[g * REP + r] * LOG2E, F32) for r in range(REP)], axis=1)
        m = jnp.maximum(jnp.max(s, axis=0, keepdims=True), sink)
        p = jnp.exp2(s - m).astype(BF16)
        sink_p = jnp.exp2(sink - m)
        vt_span = jnp.concatenate([vtp_ref[g][:, edge:], vtc_ref[g], vtn_ref[g][:, :WINDOW]], axis=1)
        halves = []
        for r in range(REP):
            cols = slice(r * tq, (r + 1) * tq)
            pv = _dot(vt_span[:, keys(t)], p[:, cols])
            halves.append(pv[:HD] / (pv[V_DIM:V_DIM + 1] + sink_p[:, cols]))
        o_ref[t * tq:(t + 1) * tq, g * REP * HD:(g + 1) * REP * HD] = (
            jnp.concatenate(halves, axis=0).T.astype(BF16))

    scores(0, *chains[0])
    for n, chain in enumerate(chains):
        if n + 1 < len(chains):
            scores((n + 1) % 2, *chains[n + 1])
        consume(n % 2, *chain)


def _win(sink, qbt, bias, kb, vbt, batch, seq):
    m = kb.shape[1]
    ts = WIN_STEP
    nt = seq // ts

    def neighbour(i, off):
        return jnp.clip(i + off, 0, nt - 1)

    def k_spec(off):
        return pl.BlockSpec((KVH, ts, LANES), lambda b, i: (0, b * nt + neighbour(i, off), 0))

    def vt_spec(off):
        return pl.BlockSpec((KVH, LANES, ts), lambda b, i: (0, 0, b * nt + neighbour(i, off)))

    bias_block = (1, KVH, 2 * WIN_Q_TILE, REP * WIN_Q_TILE)
    return pl.pallas_call(
        _win_kernel,
        grid=(batch, nt),
        in_specs=[
            pl.BlockSpec(memory_space=pltpu.SMEM),
            pl.BlockSpec((HB, LANES, ts), lambda b, i: (0, 0, b * nt + i)),
            pl.BlockSpec(bias_block, lambda b, i: (jnp.where(i == 0, 0, 1), 0, 0, 0)),
            pl.BlockSpec(bias_block, lambda b, i: (jnp.where(i == nt - 1, 2, 1), 0, 0, 0)),
            k_spec(-1), k_spec(0), k_spec(1),
            vt_spec(-1), vt_spec(0), vt_spec(1),
        ],
        out_specs=pl.BlockSpec((ts, HB * HD), lambda b, i: (b * nt + i, 0)),
        out_shape=jax.ShapeDtypeStruct((m, HB * HD), BF16),
        scratch_shapes=[pltpu.VMEM((2, 2 * WIN_Q_TILE, REP * WIN_Q_TILE), F32)],
        compiler_params=_params(2),
        name="window",
    )(sink, qbt, bias, bias, kb, kb, kb, vbt, vbt, vbt)


def _tail_kernel(h_ref, ya_ref, yb_ref, ga_ref, gb_ref, p_ref, gains_ref,
                 pa_ref, pb_ref, wo_ref, w1_ref, w3_ref, w2_ref, wg_ref, wp_ref, o_ref):
    g_mix, g_pre2, g_post2, g_pre_e, g_post_e = (gains_ref[k:k + 1] for k in range(5))
    ma = _dot(ya_ref[...], pa_ref[...])
    mb = _dot(yb_ref[...], pb_ref[...])
    mix = (ga_ref[...].astype(F32) * ma + gb_ref[...].astype(F32) * mb).astype(BF16)
    h = h_ref[...] + _rms(_dot(mix, wo_ref[...]), g_mix)
    f = _swiglu(_rms(h, g_pre2).astype(BF16), w1_ref, w3_ref, w2_ref)
    h = h + 0.5 * _rms(f, g_post2)
    gate = jax.nn.sigmoid(_dot(_rms(h, g_pre_e).astype(BF16), wg_ref[...]))
    e = _dot(p_ref[...].astype(BF16), wp_ref[...]) * gate
    o_ref[...] = h + _rms(e, g_post_e)


def _tail(h, ya, yb, ga, gb, p, gains, pa, pb, wo, w1, w3, w2, wg, wp):
    m = h.shape[0]
    tok = lambda w: pl.BlockSpec((TOK_TILE, w), lambda i: (i, 0))
    resident = lambda a: pl.BlockSpec(a.shape, lambda i: (0,) * a.ndim, pipeline_mode=pl.Buffered(1))
    weights = (pa, pb, wo, w1, w3, w2, wg, wp)
    return pl.pallas_call(
        _tail_kernel,
        grid=(m // TOK_TILE,),
        in_specs=[tok(D_MODEL), tok(HA * V_DIM), tok(HB * HD), tok(D_MODEL), tok(D_MODEL),
                  tok(PLE_DIM), _const_spec(gains.shape)] + [resident(a) for a in weights],
        out_specs=tok(D_MODEL),
        out_shape=jax.ShapeDtypeStruct((m, D_MODEL), F32),
        compiler_params=_params(1),
        name="tail",
    )(h, ya, yb, ga, gb, p, gains, *weights)


def _prep_weights(w_in, w_uq, w_uk, w_uv):
    splits = (Q_LORA, KV_LORA, ROPE_DIM, HB * HD, KVH * HD, KVH * HD, D_MODEL, D_MODEL)
    offs = [0]
    for s in splits:
        offs.append(offs[-1] + s)
    w_cq, w_ckv, w_kr, w_qb, w_kb, w_vb, w_ga, w_gb = (
        w_in[:, offs[k]:offs[k + 1]] for k in range(len(splits)))
    wgate = jnp.concatenate([w_ga, w_gb], axis=1).astype(BF16)
    wtr = jnp.concatenate([w_cq, w_ckv, w_kr, w_qb * HD ** -0.5, w_vb, w_kb],
                          axis=1).T.astype(BF16)
    return wgate, wtr, w_uq.T.astype(BF16), w_uk.T.astype(BF16), w_uv.T.astype(BF16)


def _rope_table_t(seq):
    inv = 1.0 / (ROPE_THETA ** (jnp.arange(0, ROPE_DIM, 2, dtype=F32) / ROPE_DIM))
    ang = inv[:, None] * jnp.arange(seq, dtype=F32)[None, :]
    return jnp.concatenate([jnp.cos(ang), jnp.sin(ang)], axis=0)


def _encoder_layer(x, p, bias, sink, wts):
    batch, seq, _ = x.shape
    m = batch * seq
    x2 = x.reshape(m, D_MODEL)
    p2 = p.reshape(m, PLE_DIM)
    h = _ffn(x2, wts["ffn1_pre_g"], wts["ffn1_post_g"], wts["ffn1_w1"], wts["ffn1_w3"], wts["ffn1_w2"])
    qat, ka, vat, qbt, kb, vbt, ga, gb = _inproj(
        h, wts["mix_pre_g"], _rope_table_t(seq), wts["wgate"], wts["wtr"],
        wts["q_norm_gt"], wts["kv_norm_gt"], wts["wqt"], wts["wkt"], wts["wvt"], seq)
    ya = _mla(qat, ka, vat, batch, seq)
    yb = _win(sink, qbt, bias, kb, vbt, batch, seq)
    gains = jnp.concatenate([wts["mix_post_g"], wts["ffn2_pre_g"], wts["ffn2_post_g"],
                             wts["ple_pre_g"], wts["ple_post_g"]], axis=0)
    out = _tail(h, ya, yb, ga, gb, p2, gains, wts["w_proj_a"], wts["w_proj_b"], wts["w_out"],
                wts["ffn2_w1"], wts["ffn2_w3"], wts["ffn2_w2"], wts["w_ple_gate"], wts["w_ple_proj"])
    return out.reshape(batch, seq, D_MODEL)


def _layer_weights(layer, w):
    wgate, wtr, wqt, wkt, wvt = _prep_weights(w["w_in"][layer], w["w_uq"][layer], w["w_uk"][layer],
                                              w["w_uv"][layer])
    wts = {
        "wgate": wgate, "wtr": wtr, "wqt": wqt, "wkt": wkt, "wvt": wvt,
        "q_norm_gt": w["q_norm_g"][layer][:, None], "kv_norm_gt": w["kv_norm_g"][layer][:, None],
    }
    for name in ("ffn1_pre_g", "ffn1_post_g", "mix_pre_g", "mix_post_g",
                 "ffn2_pre_g", "ffn2_post_g", "ple_pre_g", "ple_post_g"):
        wts[name] = w[name][layer][None]
    for name in ("ffn1_w1", "ffn1_w3", "ffn1_w2", "w_proj_a", "w_proj_b", "w_out",
                 "ffn2_w1", "ffn2_w3", "ffn2_w2", "w_ple_gate", "w_ple_proj"):
        wts[name] = w[name][layer].astype(BF16)
    return wts


def kernel(x_prompt, x_sample, p_prompt, p_sample, rel_bias, ffn1_pre_g, ffn1_post_g, ffn1_w1, ffn1_w3, ffn1_w2, mix_pre_g, mix_post_g, w_in, q_norm_g, kv_norm_g, w_uq, w_uk, w_uv, sink, w_proj_a, w_proj_b, w_out, ffn2_pre_g, ffn2_post_g, ffn2_w1, ffn2_w3, ffn2_w2, ple_pre_g, ple_post_g, w_ple_gate, w_ple_proj):
    w = dict(ffn1_pre_g=ffn1_pre_g, ffn1_post_g=ffn1_post_g, ffn1_w1=ffn1_w1, ffn1_w3=ffn1_w3,
             ffn1_w2=ffn1_w2, mix_pre_g=mix_pre_g, mix_post_g=mix_post_g, w_in=w_in,
             q_norm_g=q_norm_g, kv_norm_g=kv_norm_g, w_uq=w_uq, w_uk=w_uk, w_uv=w_uv,
             w_proj_a=w_proj_a, w_proj_b=w_proj_b, w_out=w_out, ffn2_pre_g=ffn2_pre_g,
             ffn2_post_g=ffn2_post_g, ffn2_w1=ffn2_w1, ffn2_w3=ffn2_w3, ffn2_w2=ffn2_w2,
             ple_pre_g=ple_pre_g, ple_post_g=ple_post_g, w_ple_gate=w_ple_gate,
             w_ple_proj=w_ple_proj)
    bias = _bias_table(rel_bias)
    y_prompt, y_sample = x_prompt, x_sample
    for layer in range(ffn1_w1.shape[0]):
        wts = _layer_weights(layer, w)
        y_prompt = _encoder_layer(y_prompt, p_prompt[layer], bias, sink[layer], wts)
        y_sample = _encoder_layer(y_sample, p_sample[layer], bias, sink[layer], wts)
    return (y_prompt, y_sample)
```

```python
import functools
import math

import jax
import jax.numpy as jnp
from jax import lax
from jax.experimental import pallas as pl
from jax.experimental.pallas import tpu as pltpu

D_MODEL = 1024
D_FF = 2816
PLE_DIM = 256
HA = 8
Q_LORA = 384
KV_LORA = 256
NOPE_DIM = 64
ROPE_DIM = 32
V_DIM = 64
ROPE_THETA = 10000.0
HB = 8
KVH = 2
REP = HB // KVH
HD = 64
WINDOW = 128
REL_BUCKETS = 32
REL_MAX_DIST = 128
EPS = 1e-6
NEG = -1e30

LANES = 128
VMEM_LIMIT = 56 * 1024 * 1024

TOK_TILE = 512
FF_CHUNK = 256
MLA_Q_TILE = TOK_TILE
MLA_KEY_GROUP = 2048
MLA_SUB = 8
MLA_ITEMS_PER_STEP = 8
WIN_Q_TILE = 2 * WINDOW
WIN_STEP = 2 * WIN_Q_TILE
LOG2E = math.log2(math.e)

F32 = jnp.float32
BF16 = jnp.bfloat16

_T_CQ = 0
_T_CKV = _T_CQ + Q_LORA
_T_KR = _T_CKV + KV_LORA
_T_QB = _T_KR + ROPE_DIM
_T_VB = _T_QB + HB * HD
_T_KB = _T_VB + KVH * HD
_T_END = _T_KB + KVH * HD
QK_DIM = NOPE_DIM + ROPE_DIM


def _params(n_axes):
    return pltpu.CompilerParams(dimension_semantics=("arbitrary",) * n_axes,
                                vmem_limit_bytes=VMEM_LIMIT)


def _const_spec(shape):
    nd = len(shape)
    return pl.BlockSpec(shape, lambda *_: (0,) * nd)


def _rms(x, g, axis=-1):
    return x * lax.rsqrt(jnp.mean(x * x, axis=axis, keepdims=True) + EPS) * g


def _dot(a, b):
    return jnp.dot(a, b, preferred_element_type=F32)


def _with_ones_row(vt):
    row = lax.broadcasted_iota(jnp.int32, vt.shape, 0)
    return jnp.where(row == V_DIM, 1.0, vt)


def _swiglu(xn, w1_ref, w3_ref, w2_ref):
    acc = None
    for c in range(D_FF // FF_CHUNK):
        sl = slice(c * FF_CHUNK, (c + 1) * FF_CHUNK)
        a = _dot(xn, w1_ref[:, sl])
        b = _dot(xn, w3_ref[:, sl])
        act = (a * jax.nn.sigmoid(a) * b).astype(BF16)
        d = _dot(act, w2_ref[sl, :])
        acc = d if acc is None else acc + d
    return acc


def _ffn_kernel(x_ref, gpre_ref, gpost_ref, w1_ref, w3_ref, w2_ref, o_ref):
    x = x_ref[...]
    xn = _rms(x, gpre_ref[...]).astype(BF16)
    f = _swiglu(xn, w1_ref, w3_ref, w2_ref)
    o_ref[...] = x + 0.5 * _rms(f, gpost_ref[...])


def _ffn(x, gpre, gpost, w1, w3, w2):
    m = x.shape[0]
    tok = pl.BlockSpec((TOK_TILE, D_MODEL), lambda i: (i, 0))
    return pl.pallas_call(
        _ffn_kernel,
        grid=(m // TOK_TILE,),
        in_specs=[tok, _const_spec((1, D_MODEL)), _const_spec((1, D_MODEL)),
                  _const_spec(w1.shape), _const_spec(w3.shape), _const_spec(w2.shape)],
        out_specs=tok,
        out_shape=jax.ShapeDtypeStruct((m, D_MODEL), F32),
        compiler_params=_params(1),
        name="ffn",
    )(x, gpre, gpost, w1, w3, w2)


def _pad_rows(x, rows):
    return jnp.concatenate([x, jnp.zeros((rows - x.shape[0], x.shape[1]), x.dtype)], axis=0)


def _rope_t(x, cos_t, sin_t):
    r2 = ROPE_DIM // 2
    x1, x2 = x[:r2], x[r2:]
    return jnp.concatenate([x1 * cos_t - x2 * sin_t, x1 * sin_t + x2 * cos_t], axis=0)


def _inproj_kernel(h_ref, g_ref, rope_ref, wgate_ref, wtr_ref, qngt_ref, kvngt_ref,
                   wqt_ref, wkt_ref, wvt_ref,
                   qat_ref, ka_ref, vat_ref, qbt_ref, kb_ref, vbt_ref, ga_ref, gb_ref):
    u = _rms(h_ref[...], g_ref[...])
    ub = u.astype(BF16)
    ut = u.T.astype(BF16)
    cos_t = rope_ref[:ROPE_DIM // 2]
    sin_t = rope_ref[ROPE_DIM // 2:]

    ga_ref[...] = jax.nn.sigmoid(_dot(ub, wgate_ref[:, :D_MODEL])).astype(BF16)
    gb_ref[...] = jax.nn.sigmoid(_dot(ub, wgate_ref[:, D_MODEL:])).astype(BF16)

    cqt = _rms(_dot(wtr_ref[_T_CQ:_T_CKV, :], ut), qngt_ref[...], axis=0).astype(BF16)
    ckvt = _rms(_dot(wtr_ref[_T_CKV:_T_KR, :], ut), kvngt_ref[...], axis=0).astype(BF16)
    rest = _dot(wtr_ref[_T_KR:_T_END, :], ut)
    off = lambda lo, hi: rest[lo - _T_KR:hi - _T_KR]
    krt = _rope_t(off(_T_KR, _T_QB), cos_t, sin_t)
    qbt = off(_T_QB, _T_VB) * LOG2E
    vbt = off(_T_VB, _T_KB)
    kbt = off(_T_KB, _T_END)
    for hb in range(HB):
        qbt_ref[hb] = _pad_rows(qbt[hb * HD:(hb + 1) * HD], LANES).astype(BF16)
    for g in range(KVH):
        kb_ref[g] = _pad_rows(kbt[g * HD:(g + 1) * HD], LANES).T.astype(BF16)
        vbt_ref[g] = _with_ones_row(_pad_rows(vbt[g * HD:(g + 1) * HD], LANES)).astype(BF16)

    qt_all = _dot(wqt_ref[...], cqt)
    knt_all = _dot(wkt_ref[...], ckvt)
    vt_all = _dot(wvt_ref[...], ckvt)
    qscale = QK_DIM ** -0.5 * LOG2E
    for h in range(HA):
        qh = qt_all[h * QK_DIM:(h + 1) * QK_DIM]
        qt = jnp.concatenate([qh[:NOPE_DIM], _rope_t(qh[NOPE_DIM:], cos_t, sin_t)], axis=0) * qscale
        qat_ref[h, 0] = _pad_rows(qt, LANES).astype(BF16)
        kt = jnp.concatenate([knt_all[h * NOPE_DIM:(h + 1) * NOPE_DIM], krt], axis=0)
        ka_ref[h] = _pad_rows(kt, LANES).T.astype(BF16)
        vat_ref[h, 0] = _with_ones_row(_pad_rows(vt_all[h * V_DIM:(h + 1) * V_DIM], LANES)).astype(BF16)


def _inproj(h, g, rope_t, wgate, wtr, qngt, kvngt, wqt, wkt, wvt, seq):
    m = h.shape[0]
    nt = m // TOK_TILE
    per_seq = seq // TOK_TILE
    per_group = MLA_KEY_GROUP // TOK_TILE
    tok = lambda w: pl.BlockSpec((TOK_TILE, w), lambda i: (i, 0))
    heads = lambda n: pl.BlockSpec((n, TOK_TILE, LANES), lambda i: (0, i, 0))
    heads_t = lambda n: pl.BlockSpec((n, LANES, TOK_TILE), lambda i: (0, 0, i))
    out_shape = (
        jax.ShapeDtypeStruct((HA, nt, LANES, TOK_TILE), BF16),
        jax.ShapeDtypeStruct((HA, m, LANES), BF16),
        jax.ShapeDtypeStruct((HA, m // MLA_KEY_GROUP, LANES, MLA_KEY_GROUP), BF16),
        jax.ShapeDtypeStruct((HB, LANES, m), BF16),
        jax.ShapeDtypeStruct((KVH, m, LANES), BF16),
        jax.ShapeDtypeStruct((KVH, LANES, m), BF16),
        jax.ShapeDtypeStruct((m, D_MODEL), BF16),
        jax.ShapeDtypeStruct((m, D_MODEL), BF16),
    )
    out_specs = (
        pl.BlockSpec((HA, 1, LANES, TOK_TILE), lambda i: (0, i, 0, 0)),
        heads(HA),
        pl.BlockSpec((HA, 1, LANES, TOK_TILE), lambda i: (0, i // per_group, 0, i % per_group)),
        heads_t(HB),
        heads(KVH),
        heads_t(KVH),
        tok(D_MODEL),
        tok(D_MODEL),
    )
    return pl.pallas_call(
        _inproj_kernel,
        grid=(nt,),
        in_specs=[tok(D_MODEL), _const_spec((1, D_MODEL)),
                  pl.BlockSpec((ROPE_DIM, TOK_TILE), lambda i: (0, i % per_seq)),
                  _const_spec(wgate.shape), _const_spec(wtr.shape),
                  _const_spec((Q_LORA, 1)), _const_spec((KV_LORA, 1)),
                  _const_spec(wqt.shape), _const_spec(wkt.shape), _const_spec(wvt.shape)],
        out_specs=out_specs,
        out_shape=out_shape,
        compiler_params=_params(1),
        name="inproj",
    )(h, g, rope_t, wgate, wtr, qngt, kvngt, wqt, wkt, wvt)


def _mla_kernel(qt_ref, qn_ref, k_ref, vt_ref, o_ref, s_ref, m_ref, q_all, o_acc, *, n_groups, q_tiles):
    gk, tq, ck = MLA_KEY_GROUP, MLA_Q_TILE, MLA_KEY_GROUP // MLA_SUB
    n_items = q_tiles * 2 * n_groups

    for qi in range(q_tiles):
        for j in range(2):
            q_all[qi * 2 + j] = qt_ref[j, qi]
    q_all[2 * q_tiles] = qn_ref[0, 0]

    def produce_chunk(slot, n, c):
        head, group = (n // n_groups) % 2, n % n_groups
        start = pl.multiple_of(group * gk + c * ck, ck)
        s = _dot(k_ref[head, pl.ds(start, ck), :], q_all[n // n_groups])
        s_ref[slot, c * ck:(c + 1) * ck, :] = s
        return jnp.max(s, axis=0, keepdims=True)

    def step(slot, n, carry):
        head, group = (n // n_groups) % 2, n % n_groups
        m_old, acc = carry
        m_old = jnp.where(group == 0, -jnp.inf, m_old)
        m_new = jnp.maximum(m_old, m_ref[slot])
        pv = m_nxt = None
        for c in range(MLA_SUB):
            mc = produce_chunk(1 - slot, n + 1, c)
            m_nxt = mc if m_nxt is None else jnp.maximum(m_nxt, mc)
            rows = slice(c * ck, (c + 1) * ck)
            p = jnp.exp2(s_ref[slot, rows, :] - m_new).astype(BF16)
            d = _dot(vt_ref[head, group, :, rows], p)
            pv = d if pv is None else pv + d
        m_ref[1 - slot] = m_nxt
        acc = jnp.exp2(m_old - m_new) * acc + pv
        o_acc[n // (2 * n_groups), pl.ds(pl.multiple_of(head * V_DIM, V_DIM), V_DIM), :] = (
            acc[:V_DIM] / acc[V_DIM:V_DIM + 1])
        return m_new, acc

    @pl.when(pl.program_id(2) == 0)
    def _():
        m_first = None
        for c in range(MLA_SUB):
            mc = produce_chunk(0, 0, c)
            m_first = mc if m_first is None else jnp.maximum(m_first, mc)
        m_ref[0] = m_first

    def pair(t, carry):
        return step(1, 2 * t + 1, step(0, 2 * t, carry))

    lax.fori_loop(0, n_items // 2, pair,
                  (jnp.full((1, tq), -jnp.inf, F32), jnp.zeros((LANES, tq), F32)))
    for qi in range(q_tiles):
        o_ref[qi * tq:(qi + 1) * tq, :] = o_acc[qi].T.astype(BF16)


def _mla(qat, ka, vat, batch, seq):
    m = ka.shape[1]
    n_groups = seq // MLA_KEY_GROUP
    q_tiles = max(1, MLA_ITEMS_PER_STEP // (2 * n_groups))
    tqs = q_tiles * MLA_Q_TILE
    nq = seq // tqs
    last_tile = m // MLA_Q_TILE - 1
    return pl.pallas_call(
        functools.partial(_mla_kernel, n_groups=n_groups, q_tiles=q_tiles),
        grid=(batch, HA // 2, nq),
        in_specs=[
            pl.BlockSpec((2, q_tiles, LANES, MLA_Q_TILE), lambda b, hp, i: (hp, b * nq + i, 0, 0)),
            pl.BlockSpec((1, 1, LANES, MLA_Q_TILE),
                         lambda b, hp, i: (2 * hp, jnp.minimum((b * nq + i + 1) * q_tiles, last_tile), 0, 0)),
            pl.BlockSpec((2, seq, LANES), lambda b, hp, i: (hp, b, 0)),
            pl.BlockSpec((2, n_groups, LANES, MLA_KEY_GROUP), lambda b, hp, i: (hp, b, 0, 0)),
        ],
        out_specs=pl.BlockSpec((tqs, LANES), lambda b, hp, i: (b * nq + i, hp)),
        out_shape=jax.ShapeDtypeStruct((m, HA * V_DIM), BF16),
        scratch_shapes=[pltpu.VMEM((2, MLA_KEY_GROUP, MLA_Q_TILE), F32),
                        pltpu.VMEM((2, 1, MLA_Q_TILE), F32),
                        pltpu.VMEM((2 * q_tiles + 1, LANES, MLA_Q_TILE), BF16),
                        pltpu.VMEM((q_tiles, LANES, MLA_Q_TILE), F32)],
        compiler_params=_params(3),
        name="mla",
    )(qat, qat, ka, vat)


def _bias_kernel(bucket_ref, relb_ref, o_ref):
    bucket = bucket_ref[...]
    key = lax.broadcasted_iota(jnp.int32, bucket.shape, 0)
    tq = WIN_Q_TILE
    for h in range(HB):
        acc = jnp.full(bucket.shape, NEG, F32)
        for b in range(REL_BUCKETS):
            acc = jnp.where(bucket == b, relb_ref[b * HB + h] * LOG2E, acc)
        g, cols = h // REP, slice((h % REP) * tq, (h % REP + 1) * tq)
        o_ref[0, g, :, cols] = jnp.where(key < WINDOW, NEG, acc)
        o_ref[1, g, :, cols] = acc
        o_ref[2, g, :, cols] = jnp.where(key >= WINDOW + tq, NEG, acc)


def _bias_table(rel_bias):
    c = jnp.arange(2 * WIN_Q_TILE, dtype=jnp.int32)[:, None]
    r = jnp.arange(WIN_Q_TILE, dtype=jnp.int32)[None, :]
    rel = c - WINDOW - r
    nb = REL_BUCKETS // 2
    max_exact = nb // 2
    ret = jnp.where(rel > 0, nb, 0)
    n = jnp.abs(rel)
    nf = jnp.maximum(n, 1).astype(F32)
    large = max_exact + (jnp.log(nf / max_exact) / math.log(REL_MAX_DIST / max_exact)
                         * (nb - max_exact)).astype(jnp.int32)
    large = jnp.minimum(large, nb - 1)
    bucket = ret + jnp.where(n < max_exact, n, large)
    bucket = jnp.where(n <= WINDOW, bucket, -1)
    return pl.pallas_call(
        _bias_kernel,
        in_specs=[pl.BlockSpec(memory_space=pltpu.VMEM), pl.BlockSpec(memory_space=pltpu.SMEM)],
        out_specs=pl.BlockSpec(memory_space=pltpu.VMEM),
        out_shape=jax.ShapeDtypeStruct((3, KVH, 2 * WIN_Q_TILE, REP * WIN_Q_TILE), F32),
        name="rel_bias_table",
    )(bucket, rel_bias.reshape(-1))


def _win_kernel(sink_ref, qt_ref, bias_lo_ref, bias_hi_ref, kp_ref, kc_ref, kn_ref,
                vtp_ref, vtc_ref, vtn_ref, o_ref, s_ref):
    tq, edge = WIN_Q_TILE, WIN_STEP - WINDOW
    chains = [(g, t) for g in range(KVH) for t in range(WIN_STEP // tq)]
    bias_refs = (bias_lo_ref, bias_hi_ref)

    def keys(t):
        return slice(t * tq, t * tq + 2 * tq)

    def scores(slot, g, t):
        k_span = jnp.concatenate([kp_ref[g][edge:], kc_ref[g], kn_ref[g][:WINDOW]], axis=0)
        qt = jnp.concatenate([qt_ref[g * REP + r][:, t * tq:(t + 1) * tq] for r in range(REP)],
                             axis=1)
        s_ref[slot] = _dot(k_span[keys(t)], qt) + bias_refs[t][0, g]

    def consume(slot, g, t):
        s = s_ref[slot]
        sink = jnp.concatenate(
            [jnp.full((1, tq), sink_ref[g * REP + r] * LOG2E, F32) for r in range(REP)], axis=1)
        m = jnp.maximum(jnp.max(s, axis=0, keepdims=True), sink)
        p = jnp.exp2(s - m).astype(BF16)
        sink_p = jnp.exp2(sink - m)
        vt_span = jnp.concatenate([vtp_ref[g][:, edge:], vtc_ref[g], vtn_ref[g][:, :WINDOW]], axis=1)
        halves = []
        for r in range(REP):
            cols = slice(r * tq, (r + 1) * tq)
            pv = _dot(vt_span[:, keys(t)], p[:, cols])
            halves.append(pv[:HD] / (pv[V_DIM:V_DIM + 1] + sink_p[:, cols]))
        o_ref[t * tq:(t + 1) * tq, g * REP * HD:(g + 1) * REP * HD] = (
            jnp.concatenate(halves, axis=0).T.astype(BF16))

    scores(0, *chains[0])
    for n, chain in enumerate(chains):
        if n + 1 < len(chains):
            scores((n + 1) % 2, *chains[n + 1])
        consume(n % 2, *chain)


def _win(sink, qbt, bias, kb, vbt, batch, seq):
    m = kb.shape[1]
    ts = WIN_STEP
    nt = seq // ts

    def neighbour(i, off):
        return jnp.clip(i + off, 0, nt - 1)

    def k_spec(off):
        return pl.BlockSpec((KVH, ts, LANES), lambda b, i: (0, b * nt + neighbour(i, off), 0))

    def vt_spec(off):
        return pl.BlockSpec((KVH, LANES, ts), lambda b, i: (0, 0, b * nt + neighbour(i, off)))

    bias_block = (1, KVH, 2 * WIN_Q_TILE, REP * WIN_Q_TILE)
    return pl.pallas_call(
        _win_kernel,
        grid=(batch, nt),
        in_specs=[
            pl.BlockSpec(memory_space=pltpu.SMEM),
            pl.BlockSpec((HB, LANES, ts), lambda b, i: (0, 0, b * nt + i)),
            pl.BlockSpec(bias_block, lambda b, i: (jnp.where(i == 0, 0, 1), 0, 0, 0)),
            pl.BlockSpec(bias_block, lambda b, i: (jnp.where(i == nt - 1, 2, 1), 0, 0, 0)),
            k_spec(-1), k_spec(0), k_spec(1),
            vt_spec(-1), vt_spec(0), vt_spec(1),
        ],
        out_specs=pl.BlockSpec((ts, HB * HD), lambda b, i: (b * nt + i, 0)),
        out_shape=jax.ShapeDtypeStruct((m, HB * HD), BF16),
        scratch_shapes=[pltpu.VMEM((2, 2 * WIN_Q_TILE, REP * WIN_Q_TILE), F32)],
        compiler_params=_params(2),
        name="window",
    )(sink, qbt, bias, bias, kb, kb, kb, vbt, vbt, vbt)


def _tail_kernel(h_ref, ya_ref, yb_ref, ga_ref, gb_ref, p_ref, gains_ref,
                 pa_ref, pb_ref, wo_ref, w1_ref, w3_ref, w2_ref, wg_ref, wp_ref, o_ref):
    g_mix, g_pre2, g_post2, g_pre_e, g_post_e = (gains_ref[k:k + 1] for k in range(5))
    ma = _dot(ya_ref[...], pa_ref[...])
    mb = _dot(yb_ref[...], pb_ref[...])
    mix = (ga_ref[...].astype(F32) * ma + gb_ref[...].astype(F32) * mb).astype(BF16)
    h = h_ref[...] + _rms(_dot(mix, wo_ref[...]), g_mix)
    f = _swiglu(_rms(h, g_pre2).astype(BF16), w1_ref, w3_ref, w2_ref)
    h = h + 0.5 * _rms(f, g_post2)
    gate = jax.nn.sigmoid(_dot(_rms(h, g_pre_e).astype(BF16), wg_ref[...]))
    e = _dot(p_ref[...].astype(BF16), wp_ref[...]) * gate
    o_ref[...] = h + _rms(e, g_post_e)


def _tail(h, ya, yb, ga, gb, p, gains, pa, pb, wo, w1, w3, w2, wg, wp):
    m = h.shape[0]
    tok = lambda w: pl.BlockSpec((TOK_TILE, w), lambda i: (i, 0))
    resident = lambda a: pl.BlockSpec(a.shape, lambda i: (0,) * a.ndim, pipeline_mode=pl.Buffered(1))
    weights = (pa, pb, wo, w1, w3, w2, wg, wp)
    return pl.pallas_call(
        _tail_kernel,
        grid=(m // TOK_TILE,),
        in_specs=[tok(D_MODEL), tok(HA * V_DIM), tok(HB * HD), tok(D_MODEL), tok(D_MODEL),
                  tok(PLE_DIM), _const_spec(gains.shape)] + [resident(a) for a in weights],
        out_specs=tok(D_MODEL),
        out_shape=jax.ShapeDtypeStruct((m, D_MODEL), F32),
        compiler_params=_params(1),
        name="tail",
    )(h, ya, yb, ga, gb, p, gains, *weights)


def _prep_weights(w_in, w_uq, w_uk, w_uv):
    splits = (Q_LORA, KV_LORA, ROPE_DIM, HB * HD, KVH * HD, KVH * HD, D_MODEL, D_MODEL)
    offs = [0]
    for s in splits:
        offs.append(offs[-1] + s)
    w_cq, w_ckv, w_kr, w_qb, w_kb, w_vb, w_ga, w_gb = (
        w_in[:, offs[k]:offs[k + 1]] for k in range(len(splits)))
    wgate = jnp.concatenate([w_ga, w_gb], axis=1).astype(BF16)
    wtr = jnp.concatenate([w_cq, w_ckv, w_kr, w_qb * HD ** -0.5, w_vb, w_kb],
                          axis=1).T.astype(BF16)
    return wgate, wtr, w_uq.T.astype(BF16), w_uk.T.astype(BF16), w_uv.T.astype(BF16)


def _rope_table_t(seq):
    inv = 1.0 / (ROPE_THETA ** (jnp.arange(0, ROPE_DIM, 2, dtype=F32) / ROPE_DIM))
    ang = inv[:, None] * jnp.arange(seq, dtype=F32)[None, :]
    return jnp.concatenate([jnp.cos(ang), jnp.sin(ang)], axis=0)


def _encoder_layer(x, p, bias, sink, wts):
    batch, seq, _ = x.shape
    m = batch * seq
    x2 = x.reshape(m, D_MODEL)
    p2 = p.reshape(m, PLE_DIM)
    h = _ffn(x2, wts["ffn1_pre_g"], wts["ffn1_post_g"], wts["ffn1_w1"], wts["ffn1_w3"], wts["ffn1_w2"])
    qat, ka, vat, qbt, kb, vbt, ga, gb = _inproj(
        h, wts["mix_pre_g"], _rope_table_t(seq), wts["wgate"], wts["wtr"],
        wts["q_norm_gt"], wts["kv_norm_gt"], wts["wqt"], wts["wkt"], wts["wvt"], seq)
    ya = _mla(qat, ka, vat, batch, seq)
    yb = _win(sink, qbt, bias, kb, vbt, batch, seq)
    gains = jnp.concatenate([wts["mix_post_g"], wts["ffn2_pre_g"], wts["ffn2_post_g"],
                             wts["ple_pre_g"], wts["ple_post_g"]], axis=0)
    out = _tail(h, ya, yb, ga, gb, p2, gains, wts["w_proj_a"], wts["w_proj_b"], wts["w_out"],
                wts["ffn2_w1"], wts["ffn2_w3"], wts["ffn2_w2"], wts["w_ple_gate"], wts["w_ple_proj"])
    return out.reshape(batch, seq, D_MODEL)


def _layer_weights(layer, w):
    wgate, wtr, wqt, wkt, wvt = _prep_weights(w["w_in"][layer], w["w_uq"][layer], w["w_uk"][layer],
                                              w["w_uv"][layer])
    wts = {
        "wgate": wgate, "wtr": wtr, "wqt": wqt, "wkt": wkt, "wvt": wvt,
        "q_norm_gt": w["q_norm_g"][layer][:, None], "kv_norm_gt": w["kv_norm_g"][layer][:, None],
    }
    for name in ("ffn1_pre_g", "ffn1_post_g", "mix_pre_g", "mix_post_g",
                 "ffn2_pre_g", "ffn2_post_g", "ple_pre_g", "ple_post_g"):
        wts[name] = w[name][layer][None]
    for name in ("ffn1_w1", "ffn1_w3", "ffn1_w2", "w_proj_a", "w_proj_b", "w_out",
                 "ffn2_w1", "ffn2_w3", "ffn2_w2", "w_ple_gate", "w_ple_proj"):
        wts[name] = w[name][layer].astype(BF16)
    return wts


def kernel(x_prompt, x_sample, p_prompt, p_sample, rel_bias, ffn1_pre_g, ffn1_post_g, ffn1_w1, ffn1_w3, ffn1_w2, mix_pre_g, mix_post_g, w_in, q_norm_g, kv_norm_g, w_uq, w_uk, w_uv, sink, w_proj_a, w_proj_b, w_out, ffn2_pre_g, ffn2_post_g, ffn2_w1, ffn2_w3, ffn2_w2, ple_pre_g, ple_post_g, w_ple_gate, w_ple_proj):
    w = dict(ffn1_pre_g=ffn1_pre_g, ffn1_post_g=ffn1_post_g, ffn1_w1=ffn1_w1, ffn1_w3=ffn1_w3,
             ffn1_w2=ffn1_w2, mix_pre_g=mix_pre_g, mix_post_g=mix_post_g, w_in=w_in,
             q_norm_g=q_norm_g, kv_norm_g=kv_norm_g, w_uq=w_uq, w_uk=w_uk, w_uv=w_uv,
             w_proj_a=w_proj_a, w_proj_b=w_proj_b, w_out=w_out, ffn2_pre_g=ffn2_pre_g,
             ffn2_post_g=ffn2_post_g, ffn2_w1=ffn2_w1, ffn2_w3=ffn2_w3, ffn2_w2=ffn2_w2,
             ple_pre_g=ple_pre_g, ple_post_g=ple_post_g, w_ple_gate=w_ple_gate,
             w_ple_proj=w_ple_proj)
    bias = _bias_table(rel_bias)
    y_prompt, y_sample = x_prompt, x_sample
    for layer in range(ffn1_w1.shape[0]):
        wts = _layer_weights(layer, w)
        y_prompt = _encoder_layer(y_prompt, p_prompt[layer], bias, sink[layer], wts)
        y_sample = _encoder_layer(y_sample, p_sample[layer], bias, sink[layer], wts)
    return (y_prompt, y_sample)
```

```python
import functools
import math

import jax
import jax.numpy as jnp
from jax import lax
from jax.experimental import pallas as pl
from jax.experimental.pallas import tpu as pltpu

D_MODEL = 1024
D_FF = 2816
PLE_DIM = 256
HA = 8
Q_LORA = 384
KV_LORA = 256
NOPE_DIM = 64
ROPE_DIM = 32
V_DIM = 64
ROPE_THETA = 10000.0
HB = 8
KVH = 2
REP = HB // KVH
HD = 64
WINDOW = 128
REL_BUCKETS = 32
REL_MAX_DIST = 128
EPS = 1e-6
NEG = -1e30

LANES = 128
VMEM_LIMIT = 56 * 1024 * 1024

TOK_TILE = 512
FF_CHUNK = 256
MLA_Q_TILE = TOK_TILE
MLA_KEY_GROUP = 2048
MLA_SUB = 8
MLA_ITEMS_PER_STEP = 8
MLA_UNROLL = 4
WIN_Q_TILE = 2 * WINDOW
WIN_STEP = 2 * WIN_Q_TILE
LOG2E = math.log2(math.e)

F32 = jnp.float32
BF16 = jnp.bfloat16

_T_CQ = 0
_T_CKV = _T_CQ + Q_LORA
_T_KR = _T_CKV + KV_LORA
_T_QB = _T_KR + ROPE_DIM
_T_VB = _T_QB + HB * HD
_T_KB = _T_VB + KVH * HD
_T_END = _T_KB + KVH * HD
QK_DIM = NOPE_DIM + ROPE_DIM


def _params(n_axes):
    return pltpu.CompilerParams(dimension_semantics=("arbitrary",) * n_axes,
                                vmem_limit_bytes=VMEM_LIMIT)


def _const_spec(shape):
    nd = len(shape)
    return pl.BlockSpec(shape, lambda *_: (0,) * nd)


def _rms(x, g, axis=-1):
    return x * lax.rsqrt(jnp.mean(x * x, axis=axis, keepdims=True) + EPS) * g


def _dot(a, b):
    return jnp.dot(a, b, preferred_element_type=F32)


def _with_ones_row(vt):
    row = lax.broadcasted_iota(jnp.int32, vt.shape, 0)
    return jnp.where(row == V_DIM, 1.0, vt)


def _swiglu(xn, w1_ref, w3_ref, w2_ref):
    acc = None
    for c in range(D_FF // FF_CHUNK):
        sl = slice(c * FF_CHUNK, (c + 1) * FF_CHUNK)
        a = _dot(xn, w1_ref[:, sl])
        b = _dot(xn, w3_ref[:, sl])
        act = (a * jax.nn.sigmoid(a) * b).astype(BF16)
        d = _dot(act, w2_ref[sl, :])
        acc = d if acc is None else acc + d
    return acc


def _ffn_kernel(x_ref, gpre_ref, gpost_ref, w1_ref, w3_ref, w2_ref, o_ref):
    x = x_ref[...]
    xn = _rms(x, gpre_ref[...]).astype(BF16)
    f = _swiglu(xn, w1_ref, w3_ref, w2_ref)
    o_ref[...] = x + 0.5 * _rms(f, gpost_ref[...])


def _ffn(x, gpre, gpost, w1, w3, w2):
    m = x.shape[0]
    tok = pl.BlockSpec((TOK_TILE, D_MODEL), lambda i: (i, 0))
    return pl.pallas_call(
        _ffn_kernel,
        grid=(m // TOK_TILE,),
        in_specs=[tok, _const_spec((1, D_MODEL)), _const_spec((1, D_MODEL)),
                  _const_spec(w1.shape), _const_spec(w3.shape), _const_spec(w2.shape)],
        out_specs=tok,
        out_shape=jax.ShapeDtypeStruct((m, D_MODEL), F32),
        compiler_params=_params(1),
        name="ffn",
    )(x, gpre, gpost, w1, w3, w2)


def _pad_rows(x, rows):
    return jnp.concatenate([x, jnp.zeros((rows - x.shape[0], x.shape[1]), x.dtype)], axis=0)


def _rope_t(x, cos_t, sin_t):
    r2 = ROPE_DIM // 2
    x1, x2 = x[:r2], x[r2:]
    return jnp.concatenate([x1 * cos_t - x2 * sin_t, x1 * sin_t + x2 * cos_t], axis=0)


def _inproj_kernel(h_ref, g_ref, rope_ref, wgate_ref, wtr_ref, qngt_ref, kvngt_ref,
                   wqt_ref, wkt_ref, wvt_ref,
                   qat_ref, ka_ref, vat_ref, qbt_ref, kb_ref, vbt_ref, ga_ref, gb_ref):
    u = _rms(h_ref[...], g_ref[...])
    ub = u.astype(BF16)
    ut = u.T.astype(BF16)
    cos_t = rope_ref[:ROPE_DIM // 2]
    sin_t = rope_ref[ROPE_DIM // 2:]

    ga_ref[...] = jax.nn.sigmoid(_dot(ub, wgate_ref[:, :D_MODEL])).astype(BF16)
    gb_ref[...] = jax.nn.sigmoid(_dot(ub, wgate_ref[:, D_MODEL:])).astype(BF16)

    cqt = _rms(_dot(wtr_ref[_T_CQ:_T_CKV, :], ut), qngt_ref[...], axis=0).astype(BF16)
    ckvt = _rms(_dot(wtr_ref[_T_CKV:_T_KR, :], ut), kvngt_ref[...], axis=0).astype(BF16)
    rest = _dot(wtr_ref[_T_KR:_T_END, :], ut)
    off = lambda lo, hi: rest[lo - _T_KR:hi - _T_KR]
    krt = _rope_t(off(_T_KR, _T_QB), cos_t, sin_t)
    qbt = off(_T_QB, _T_VB) * LOG2E
    vbt = off(_T_VB, _T_KB)
    kbt = off(_T_KB, _T_END)
    for hb in range(HB):
        qbt_ref[hb] = _pad_rows(qbt[hb * HD:(hb + 1) * HD], LANES).astype(BF16)
    for g in range(KVH):
        kb_ref[g] = _pad_rows(kbt[g * HD:(g + 1) * HD], LANES).T.astype(BF16)
        vbt_ref[g] = _with_ones_row(_pad_rows(vbt[g * HD:(g + 1) * HD], LANES)).astype(BF16)

    qt_all = _dot(wqt_ref[...], cqt)
    knt_all = _dot(wkt_ref[...], ckvt)
    vt_all = _dot(wvt_ref[...], ckvt)
    qscale = QK_DIM ** -0.5 * LOG2E
    for h in range(HA):
        qh = qt_all[h * QK_DIM:(h + 1) * QK_DIM]
        qt = jnp.concatenate([qh[:NOPE_DIM], _rope_t(qh[NOPE_DIM:], cos_t, sin_t)], axis=0) * qscale
        qat_ref[h, 0] = _pad_rows(qt, LANES).astype(BF16)
        kt = jnp.concatenate([knt_all[h * NOPE_DIM:(h + 1) * NOPE_DIM], krt], axis=0)
        ka_ref[h] = _pad_rows(kt, LANES).T.astype(BF16)
        vat_ref[h, 0] = _with_ones_row(_pad_rows(vt_all[h * V_DIM:(h + 1) * V_DIM], LANES)).astype(BF16)


def _inproj(h, g, rope_t, wgate, wtr, qngt, kvngt, wqt, wkt, wvt, seq):
    m = h.shape[0]
    nt = m // TOK_TILE
    per_seq = seq // TOK_TILE
    per_group = MLA_KEY_GROUP // TOK_TILE
    tok = lambda w: pl.BlockSpec((TOK_TILE, w), lambda i: (i, 0))
    heads = lambda n: pl.BlockSpec((n, TOK_TILE, LANES), lambda i: (0, i, 0))
    heads_t = lambda n: pl.BlockSpec((n, LANES, TOK_TILE), lambda i: (0, 0, i))
    out_shape = (
        jax.ShapeDtypeStruct((HA, nt, LANES, TOK_TILE), BF16),
        jax.ShapeDtypeStruct((HA, m, LANES), BF16),
        jax.ShapeDtypeStruct((HA, m // MLA_KEY_GROUP, LANES, MLA_KEY_GROUP), BF16),
        jax.ShapeDtypeStruct((HB, LANES, m), BF16),
        jax.ShapeDtypeStruct((KVH, m, LANES), BF16),
        jax.ShapeDtypeStruct((KVH, LANES, m), BF16),
        jax.ShapeDtypeStruct((m, D_MODEL), BF16),
        jax.ShapeDtypeStruct((m, D_MODEL), BF16),
    )
    out_specs = (
        pl.BlockSpec((HA, 1, LANES, TOK_TILE), lambda i: (0, i, 0, 0)),
        heads(HA),
        pl.BlockSpec((HA, 1, LANES, TOK_TILE), lambda i: (0, i // per_group, 0, i % per_group)),
        heads_t(HB),
        heads(KVH),
        heads_t(KVH),
        tok(D_MODEL),
        tok(D_MODEL),
    )
    return pl.pallas_call(
        _inproj_kernel,
        grid=(nt,),
        in_specs=[tok(D_MODEL), _const_spec((1, D_MODEL)),
                  pl.BlockSpec((ROPE_DIM, TOK_TILE), lambda i: (0, i % per_seq)),
                  _const_spec(wgate.shape), _const_spec(wtr.shape),
                  _const_spec((Q_LORA, 1)), _const_spec((KV_LORA, 1)),
                  _const_spec(wqt.shape), _const_spec(wkt.shape), _const_spec(wvt.shape)],
        out_specs=out_specs,
        out_shape=out_shape,
        compiler_params=_params(1),
        name="inproj",
    )(h, g, rope_t, wgate, wtr, qngt, kvngt, wqt, wkt, wvt)


def _mla_kernel(qt_ref, qn_ref, k_ref, vt_ref, o_ref, s_ref, m_ref, q_all, o_acc, *, n_groups, q_tiles):
    gk, tq, ck, half = MLA_KEY_GROUP, MLA_Q_TILE, MLA_KEY_GROUP // MLA_SUB, MLA_SUB // 2
    n_items = q_tiles * 2 * n_groups

    for qi in range(q_tiles):
        for j in range(2):
            q_all[qi * 2 + j] = qt_ref[j, qi]
    for j in range(2):
        q_all[2 * q_tiles + j] = qn_ref[j, 0]

    def produce(slot, n, c):
        head, group = (n // n_groups) % 2, n % n_groups
        start = pl.multiple_of(group * gk + c * ck, ck)
        s = _dot(k_ref[head, pl.ds(start, ck), :], q_all[n // n_groups])
        s_ref[slot, c * ck:(c + 1) * ck, :] = s
        return jnp.max(s, axis=0, keepdims=True)

    def produce_half(slot, n, which):
        mh = None
        for c in range(which * half, (which + 1) * half):
            mc = produce(slot, n, c)
            mh = mc if mh is None else jnp.maximum(mh, mc)
        m_ref[slot, which] = mh

    def step(slot, n, carry):
        head, group = (n // n_groups) % 2, n % n_groups
        m_old, acc = carry
        m_old = jnp.where(group == 0, -jnp.inf, m_old)
        m_new = jnp.maximum(m_old, jnp.maximum(m_ref[slot, 0], m_ref[slot, 1]))
        pv = mh = None
        for c in range(MLA_SUB):
            mc = produce(1 - slot, n + 1, c + half) if c < half else produce(slot, n + 2, c - half)
            mh = mc if c % half == 0 else jnp.maximum(mh, mc)
            rows = slice(c * ck, (c + 1) * ck)
            p = jnp.exp2(s_ref[slot, rows, :] - m_new).astype(BF16)
            d = _dot(vt_ref[head, group, :, rows], p)
            pv = d if pv is None else pv + d
            if c == half - 1:
                m_ref[1 - slot, 1] = mh
        m_ref[slot, 0] = mh
        acc = jnp.exp2(m_old - m_new) * acc + pv
        o_acc[n // (2 * n_groups), pl.ds(pl.multiple_of(head * V_DIM, V_DIM), V_DIM), :] = (
            acc[:V_DIM] / acc[V_DIM:V_DIM + 1])
        return m_new, acc

    @pl.when(pl.program_id(2) == 0)
    def _():
        produce_half(0, 0, 0)
        produce_half(0, 0, 1)
        produce_half(1, 1, 0)

    def body(t, carry):
        for u in range(MLA_UNROLL):
            carry = step(u % 2, MLA_UNROLL * t + u, carry)
        return carry

    lax.fori_loop(0, n_items // MLA_UNROLL, body,
                  (jnp.full((1, tq), -jnp.inf, F32), jnp.zeros((LANES, tq), F32)))
    for qi in range(q_tiles):
        o_ref[qi * tq:(qi + 1) * tq, :] = o_acc[qi].T.astype(BF16)


def _mla(qat, ka, vat, batch, seq):
    m = ka.shape[1]
    n_groups = seq // MLA_KEY_GROUP
    q_tiles = max(1, MLA_ITEMS_PER_STEP // (2 * n_groups))
    tqs = q_tiles * MLA_Q_TILE
    nq = seq // tqs
    last_tile = m // MLA_Q_TILE - 1
    return pl.pallas_call(
        functools.partial(_mla_kernel, n_groups=n_groups, q_tiles=q_tiles),
        grid=(batch, HA // 2, nq),
        in_specs=[
            pl.BlockSpec((2, q_tiles, LANES, MLA_Q_TILE), lambda b, hp, i: (hp, b * nq + i, 0, 0)),
            pl.BlockSpec((2, 1, LANES, MLA_Q_TILE),
                         lambda b, hp, i: (hp, jnp.minimum((b * nq + i + 1) * q_tiles, last_tile), 0, 0)),
            pl.BlockSpec((2, seq, LANES), lambda b, hp, i: (hp, b, 0)),
            pl.BlockSpec((2, n_groups, LANES, MLA_KEY_GROUP), lambda b, hp, i: (hp, b, 0, 0)),
        ],
        out_specs=pl.BlockSpec((tqs, LANES), lambda b, hp, i: (b * nq + i, hp)),
        out_shape=jax.ShapeDtypeStruct((m, HA * V_DIM), BF16),
        scratch_shapes=[pltpu.VMEM((2, MLA_KEY_GROUP, MLA_Q_TILE), F32),
                        pltpu.VMEM((2, 2, 1, MLA_Q_TILE), F32),
                        pltpu.VMEM((2 * q_tiles + 2, LANES, MLA_Q_TILE), BF16),
                        pltpu.VMEM((q_tiles, LANES, MLA_Q_TILE), F32)],
        compiler_params=_params(3),
        name="mla",
    )(qat, qat, ka, vat)


def _bias_kernel(bucket_ref, relb_ref, o_ref):
    bucket = bucket_ref[...]
    key = lax.broadcasted_iota(jnp.int32, bucket.shape, 0)
    tq = WIN_Q_TILE
    for h in range(HB):
        acc = jnp.full(bucket.shape, NEG, F32)
        for b in range(REL_BUCKETS):
            acc = jnp.where(bucket == b, relb_ref[b * HB + h] * LOG2E, acc)
        g, cols = h // REP, slice((h % REP) * tq, (h % REP + 1) * tq)
        o_ref[0, g, :, cols] = jnp.where(key < WINDOW, NEG, acc)
        o_ref[1, g, :, cols] = acc
        o_ref[2, g, :, cols] = jnp.where(key >= WINDOW + tq, NEG, acc)


def _bias_table(rel_bias):
    c = jnp.arange(2 * WIN_Q_TILE, dtype=jnp.int32)[:, None]
    r = jnp.arange(WIN_Q_TILE, dtype=jnp.int32)[None, :]
    rel = c - WINDOW - r
    nb = REL_BUCKETS // 2
    max_exact = nb // 2
    ret = jnp.where(rel > 0, nb, 0)
    n = jnp.abs(rel)
    nf = jnp.maximum(n, 1).astype(F32)
    large = max_exact + (jnp.log(nf / max_exact) / math.log(REL_MAX_DIST / max_exact)
                         * (nb - max_exact)).astype(jnp.int32)
    large = jnp.minimum(large, nb - 1)
    bucket = ret + jnp.where(n < max_exact, n, large)
    bucket = jnp.where(n <= WINDOW, bucket, -1)
    return pl.pallas_call(
        _bias_kernel,
        in_specs=[pl.BlockSpec(memory_space=pltpu.VMEM), pl.BlockSpec(memory_space=pltpu.SMEM)],
        out_specs=pl.BlockSpec(memory_space=pltpu.VMEM),
        out_shape=jax.ShapeDtypeStruct((3, KVH, 2 * WIN_Q_TILE, REP * WIN_Q_TILE), F32),
        name="rel_bias_table",
    )(bucket, rel_bias.reshape(-1))


def _win_kernel(sink_ref, qt_ref, bias_lo_ref, bias_hi_ref, kp_ref, kc_ref, kn_ref,
                vtp_ref, vtc_ref, vtn_ref, o_ref, s_ref):
    tq, edge = WIN_Q_TILE, WIN_STEP - WINDOW
    chains = [(g, t) for g in range(KVH) for t in range(WIN_STEP // tq)]
    bias_refs = (bias_lo_ref, bias_hi_ref)

    def keys(t):
        return slice(t * tq, t * tq + 2 * tq)

    def scores(slot, g, t):
        k_span = jnp.concatenate([kp_ref[g][edge:], kc_ref[g], kn_ref[g][:WINDOW]], axis=0)
        qt = jnp.concatenate([qt_ref[g * REP + r][:, t * tq:(t + 1) * tq] for r in range(REP)],
                             axis=1)
        s_ref[slot] = _dot(k_span[keys(t)], qt) + bias_refs[t][0, g]

    def consume(slot, g, t):
        s = s_ref[slot]
        sink = jnp.concatenate(
            [jnp.full((1, tq), sink_ref[g * REP + r] * LOG2E, F32) for r in range(REP)], axis=1)
        m = jnp.maximum(jnp.max(s, axis=0, keepdims=True), sink)
        p = jnp.exp2(s - m).astype(BF16)
        sink_p = jnp.exp2(sink - m)
        vt_span = jnp.concatenate([vtp_ref[g][:, edge:], vtc_ref[g], vtn_ref[g][:, :WINDOW]], axis=1)
        halves = []
        for r in range(REP):
            cols = slice(r * tq, (r + 1) * tq)
            pv = _dot(vt_span[:, keys(t)], p[:, cols])
            halves.append(pv[:HD] / (pv[V_DIM:V_DIM + 1] + sink_p[:, cols]))
        o_ref[t * tq:(t + 1) * tq, g * REP * HD:(g + 1) * REP * HD] = (
            jnp.concatenate(halves, axis=0).T.astype(BF16))

    scores(0, *chains[0])
    for n, chain in enumerate(chains):
        if n + 1 < len(chains):
            scores((n + 1) % 2, *chains[n + 1])
        consume(n % 2, *chain)


def _win(sink, qbt, bias, kb, vbt, batch, seq):
    m = kb.shape[1]
    ts = WIN_STEP
    nt = seq // ts

    def neighbour(i, off):
        return jnp.clip(i + off, 0, nt - 1)

    def k_spec(off):
        return pl.BlockSpec((KVH, ts, LANES), lambda b, i: (0, b * nt + neighbour(i, off), 0))

    def vt_spec(off):
        return pl.BlockSpec((KVH, LANES, ts), lambda b, i: (0, 0, b * nt + neighbour(i, off)))

    bias_block = (1, KVH, 2 * WIN_Q_TILE, REP * WIN_Q_TILE)
    return pl.pallas_call(
        _win_kernel,
        grid=(batch, nt),
        in_specs=[
            pl.BlockSpec(memory_space=pltpu.SMEM),
            pl.BlockSpec((HB, LANES, ts), lambda b, i: (0, 0, b * nt + i)),
            pl.BlockSpec(bias_block, lambda b, i: (jnp.where(i == 0, 0, 1), 0, 0, 0)),
            pl.BlockSpec(bias_block, lambda b, i: (jnp.where(i == nt - 1, 2, 1), 0, 0, 0)),
            k_spec(-1), k_spec(0), k_spec(1),
            vt_spec(-1), vt_spec(0), vt_spec(1),
        ],
        out_specs=pl.BlockSpec((ts, HB * HD), lambda b, i: (b * nt + i, 0)),
        out_shape=jax.ShapeDtypeStruct((m, HB * HD), BF16),
        scratch_shapes=[pltpu.VMEM((2, 2 * WIN_Q_TILE, REP * WIN_Q_TILE), F32)],
        compiler_params=_params(2),
        name="window",
    )(sink, qbt, bias, bias, kb, kb, kb, vbt, vbt, vbt)


def _tail_kernel(h_ref, ya_ref, yb_ref, ga_ref, gb_ref, p_ref, gains_ref,
                 pa_ref, pb_ref, wo_ref, w1_ref, w3_ref, w2_ref, wg_ref, wp_ref, o_ref):
    g_mix, g_pre2, g_post2, g_pre_e, g_post_e = (gains_ref[k:k + 1] for k in range(5))
    ma = _dot(ya_ref[...], pa_ref[...])
    mb = _dot(yb_ref[...], pb_ref[...])
    mix = (ga_ref[...].astype(F32) * ma + gb_ref[...].astype(F32) * mb).astype(BF16)
    h = h_ref[...] + _rms(_dot(mix, wo_ref[...]), g_mix)
    f = _swiglu(_rms(h, g_pre2).astype(BF16), w1_ref, w3_ref, w2_ref)
    h = h + 0.5 * _rms(f, g_post2)
    gate = jax.nn.sigmoid(_dot(_rms(h, g_pre_e).astype(BF16), wg_ref[...]))
    e = _dot(p_ref[...].astype(BF16), wp_ref[...]) * gate
    o_ref[...] = h + _rms(e, g_post_e)


def _tail(h, ya, yb, ga, gb, p, gains, pa, pb, wo, w1, w3, w2, wg, wp):
    m = h.shape[0]
    tok = lambda w: pl.BlockSpec((TOK_TILE, w), lambda i: (i, 0))
    resident = lambda a: pl.BlockSpec(a.shape, lambda i: (0,) * a.ndim, pipeline_mode=pl.Buffered(1))
    weights = (pa, pb, wo, w1, w3, w2, wg, wp)
    return pl.pallas_call(
        _tail_kernel,
        grid=(m // TOK_TILE,),
        in_specs=[tok(D_MODEL), tok(HA * V_DIM), tok(HB * HD), tok(D_MODEL), tok(D_MODEL),
                  tok(PLE_DIM), _const_spec(gains.shape)] + [resident(a) for a in weights],
        out_specs=tok(D_MODEL),
        out_shape=jax.ShapeDtypeStruct((m, D_MODEL), F32),
        compiler_params=_params(1),
        name="tail",
    )(h, ya, yb, ga, gb, p, gains, *weights)


def _prep_weights(w_in, w_uq, w_uk, w_uv):
    splits = (Q_LORA, KV_LORA, ROPE_DIM, HB * HD, KVH * HD, KVH * HD, D_MODEL, D_MODEL)
    offs = [0]
    for s in splits:
        offs.append(offs[-1] + s)
    w_cq, w_ckv, w_kr, w_qb, w_kb, w_vb, w_ga, w_gb = (
        w_in[:, offs[k]:offs[k + 1]] for k in range(len(splits)))
    wgate = jnp.concatenate([w_ga, w_gb], axis=1).astype(BF16)
    wtr = jnp.concatenate([w_cq, w_ckv, w_kr, w_qb * HD ** -0.5, w_vb, w_kb],
                          axis=1).T.astype(BF16)
    return wgate, wtr, w_uq.T.astype(BF16), w_uk.T.astype(BF16), w_uv.T.astype(BF16)


def _rope_table_t(seq):
    inv = 1.0 / (ROPE_THETA ** (jnp.arange(0, ROPE_DIM, 2, dtype=F32) / ROPE_DIM))
    ang = inv[:, None] * jnp.arange(seq, dtype=F32)[None, :]
    return jnp.concatenate([jnp.cos(ang), jnp.sin(ang)], axis=0)


def _encoder_layer(x, p, bias, sink, wts):
    batch, seq, _ = x.shape
    m = batch * seq
    x2 = x.reshape(m, D_MODEL)
    p2 = p.reshape(m, PLE_DIM)
    h = _ffn(x2, wts["ffn1_pre_g"], wts["ffn1_post_g"], wts["ffn1_w1"], wts["ffn1_w3"], wts["ffn1_w2"])
    qat, ka, vat, qbt, kb, vbt, ga, gb = _inproj(
        h, wts["mix_pre_g"], _rope_table_t(seq), wts["wgate"], wts["wtr"],
        wts["q_norm_gt"], wts["kv_norm_gt"], wts["wqt"], wts["wkt"], wts["wvt"], seq)
    ya = _mla(qat, ka, vat, batch, seq)
    yb = _win(sink, qbt, bias, kb, vbt, batch, seq)
    gains = jnp.concatenate([wts["mix_post_g"], wts["ffn2_pre_g"], wts["ffn2_post_g"],
                             wts["ple_pre_g"], wts["ple_post_g"]], axis=0)
    out = _tail(h, ya, yb, ga, gb, p2, gains, wts["w_proj_a"], wts["w_proj_b"], wts["w_out"],
                wts["ffn2_w1"], wts["ffn2_w3"], wts["ffn2_w2"], wts["w_ple_gate"], wts["w_ple_proj"])
    return out.reshape(batch, seq, D_MODEL)


def _layer_weights(layer, w):
    wgate, wtr, wqt, wkt, wvt = _prep_weights(w["w_in"][layer], w["w_uq"][layer], w["w_uk"][layer],
                                              w["w_uv"][layer])
    wts = {
        "wgate": wgate, "wtr": wtr, "wqt": wqt, "wkt": wkt, "wvt": wvt,
        "q_norm_gt": w["q_norm_g"][layer][:, None], "kv_norm_gt": w["kv_norm_g"][layer][:, None],
    }
    for name in ("ffn1_pre_g", "ffn1_post_g", "mix_pre_g", "mix_post_g",
                 "ffn2_pre_g", "ffn2_post_g", "ple_pre_g", "ple_post_g"):
        wts[name] = w[name][layer][None]
    for name in ("ffn1_w1", "ffn1_w3", "ffn1_w2", "w_proj_a", "w_proj_b", "w_out",
                 "ffn2_w1", "ffn2_w3", "ffn2_w2", "w_ple_gate", "w_ple_proj"):
        wts[name] = w[name][layer].astype(BF16)
    return wts


def kernel(x_prompt, x_sample, p_prompt, p_sample, rel_bias, ffn1_pre_g, ffn1_post_g, ffn1_w1, ffn1_w3, ffn1_w2, mix_pre_g, mix_post_g, w_in, q_norm_g, kv_norm_g, w_uq, w_uk, w_uv, sink, w_proj_a, w_proj_b, w_out, ffn2_pre_g, ffn2_post_g, ffn2_w1, ffn2_w3, ffn2_w2, ple_pre_g, ple_post_g, w_ple_gate, w_ple_proj):
    w = dict(ffn1_pre_g=ffn1_pre_g, ffn1_post_g=ffn1_post_g, ffn1_w1=ffn1_w1, ffn1_w3=ffn1_w3,
             ffn1_w2=ffn1_w2, mix_pre_g=mix_pre_g, mix_post_g=mix_post_g, w_in=w_in,
             q_norm_g=q_norm_g, kv_norm_g=kv_norm_g, w_uq=w_uq, w_uk=w_uk, w_uv=w_uv,
             w_proj_a=w_proj_a, w_proj_b=w_proj_b, w_out=w_out, ffn2_pre_g=ffn2_pre_g,
             ffn2_post_g=ffn2_post_g, ffn2_w1=ffn2_w1, ffn2_w3=ffn2_w3, ffn2_w2=ffn2_w2,
             ple_pre_g=ple_pre_g, ple_post_g=ple_post_g, w_ple_gate=w_ple_gate,
             w_ple_proj=w_ple_proj)
    bias = _bias_table(rel_bias)
    y_prompt, y_sample = x_prompt, x_sample
    for layer in range(ffn1_w1.shape[0]):
        wts = _layer_weights(layer, w)
        y_prompt = _encoder_layer(y_prompt, p_prompt[layer], bias, sink[layer], wts)
        y_sample = _encoder_layer(y_sample, p_sample[layer], bias, sink[layer], wts)
    return (y_prompt, y_sample)
```

```python
import functools
import math

import jax
import jax.numpy as jnp
from jax import lax
from jax.experimental import pallas as pl
from jax.experimental.pallas import tpu as pltpu

D_MODEL = 1024
D_FF = 2816
PLE_DIM = 256
HA = 8
Q_LORA = 384
KV_LORA = 256
NOPE_DIM = 64
ROPE_DIM = 32
V_DIM = 64
ROPE_THETA = 10000.0
HB = 8
KVH = 2
REP = HB // KVH
HD = 64
WINDOW = 128
REL_BUCKETS = 32
REL_MAX_DIST = 128
EPS = 1e-6
NEG = -1e30

LANES = 128
VMEM_LIMIT = 56 * 1024 * 1024

TOK_TILE = 512
FF_CHUNK = 256
MLA_Q_TILE = TOK_TILE
MLA_KEY_GROUP = 2048
MLA_SUB = 8
MLA_ITEMS_PER_STEP = 32
MLA_UNROLL = 8
WIN_Q_TILE = 2 * WINDOW
WIN_STEP = 2 * WIN_Q_TILE
LOG2E = math.log2(math.e)

F32 = jnp.float32
BF16 = jnp.bfloat16

_T_CQ = 0
_T_CKV = _T_CQ + Q_LORA
_T_KR = _T_CKV + KV_LORA
_T_QB = _T_KR + ROPE_DIM
_T_VB = _T_QB + HB * HD
_T_KB = _T_VB + KVH * HD
_T_END = _T_KB + KVH * HD
QK_DIM = NOPE_DIM + ROPE_DIM


def _params(n_axes):
    return pltpu.CompilerParams(dimension_semantics=("arbitrary",) * n_axes,
                                vmem_limit_bytes=VMEM_LIMIT)


def _const_spec(shape):
    nd = len(shape)
    return pl.BlockSpec(shape, lambda *_: (0,) * nd)


def _rms(x, g, axis=-1):
    return x * lax.rsqrt(jnp.mean(x * x, axis=axis, keepdims=True) + EPS) * g


def _dot(a, b):
    return jnp.dot(a, b, preferred_element_type=F32)


def _with_ones_row(vt):
    row = lax.broadcasted_iota(jnp.int32, vt.shape, 0)
    return jnp.where(row == V_DIM, 1.0, vt)


def _swiglu(xn, w1_ref, w3_ref, w2_ref):
    acc = None
    for c in range(D_FF // FF_CHUNK):
        sl = slice(c * FF_CHUNK, (c + 1) * FF_CHUNK)
        a = _dot(xn, w1_ref[:, sl])
        b = _dot(xn, w3_ref[:, sl])
        act = (a * jax.nn.sigmoid(a) * b).astype(BF16)
        d = _dot(act, w2_ref[sl, :])
        acc = d if acc is None else acc + d
    return acc


def _ffn_kernel(x_ref, gpre_ref, gpost_ref, w1_ref, w3_ref, w2_ref, o_ref):
    x = x_ref[...]
    xn = _rms(x, gpre_ref[...]).astype(BF16)
    f = _swiglu(xn, w1_ref, w3_ref, w2_ref)
    o_ref[...] = x + 0.5 * _rms(f, gpost_ref[...])


def _ffn(x, gpre, gpost, w1, w3, w2):
    m = x.shape[0]
    tok = pl.BlockSpec((TOK_TILE, D_MODEL), lambda i: (i, 0))
    return pl.pallas_call(
        _ffn_kernel,
        grid=(m // TOK_TILE,),
        in_specs=[tok, _const_spec((1, D_MODEL)), _const_spec((1, D_MODEL)),
                  _const_spec(w1.shape), _const_spec(w3.shape), _const_spec(w2.shape)],
        out_specs=tok,
        out_shape=jax.ShapeDtypeStruct((m, D_MODEL), F32),
        compiler_params=_params(1),
        name="ffn",
    )(x, gpre, gpost, w1, w3, w2)


def _pad_rows(x, rows):
    return jnp.concatenate([x, jnp.zeros((rows - x.shape[0], x.shape[1]), x.dtype)], axis=0)


def _rope_t(x, cos_t, sin_t):
    r2 = ROPE_DIM // 2
    x1, x2 = x[:r2], x[r2:]
    return jnp.concatenate([x1 * cos_t - x2 * sin_t, x1 * sin_t + x2 * cos_t], axis=0)


def _inproj_kernel(h_ref, g_ref, rope_ref, wgate_ref, wtr_ref, qngt_ref, kvngt_ref,
                   wqt_ref, wkt_ref, wvt_ref,
                   qat_ref, ka_ref, vat_ref, qbt_ref, kb_ref, vbt_ref, ga_ref, gb_ref):
    u = _rms(h_ref[...], g_ref[...])
    ub = u.astype(BF16)
    ut = u.T.astype(BF16)
    cos_t = rope_ref[:ROPE_DIM // 2]
    sin_t = rope_ref[ROPE_DIM // 2:]

    ga_ref[...] = jax.nn.sigmoid(_dot(ub, wgate_ref[:, :D_MODEL])).astype(BF16)
    gb_ref[...] = jax.nn.sigmoid(_dot(ub, wgate_ref[:, D_MODEL:])).astype(BF16)

    cqt = _rms(_dot(wtr_ref[_T_CQ:_T_CKV, :], ut), qngt_ref[...], axis=0).astype(BF16)
    ckvt = _rms(_dot(wtr_ref[_T_CKV:_T_KR, :], ut), kvngt_ref[...], axis=0).astype(BF16)
    rest = _dot(wtr_ref[_T_KR:_T_END, :], ut)
    off = lambda lo, hi: rest[lo - _T_KR:hi - _T_KR]
    krt = _rope_t(off(_T_KR, _T_QB), cos_t, sin_t)
    qbt = off(_T_QB, _T_VB) * LOG2E
    vbt = off(_T_VB, _T_KB)
    kbt = off(_T_KB, _T_END)
    for hb in range(HB):
        qbt_ref[hb] = _pad_rows(qbt[hb * HD:(hb + 1) * HD], LANES).astype(BF16)
    for g in range(KVH):
        kb_ref[g] = _pad_rows(kbt[g * HD:(g + 1) * HD], LANES).T.astype(BF16)
        vbt_ref[g] = _with_ones_row(_pad_rows(vbt[g * HD:(g + 1) * HD], LANES)).astype(BF16)

    qt_all = _dot(wqt_ref[...], cqt)
    knt_all = _dot(wkt_ref[...], ckvt)
    vt_all = _dot(wvt_ref[...], ckvt)
    qscale = QK_DIM ** -0.5 * LOG2E
    for h in range(HA):
        qh = qt_all[h * QK_DIM:(h + 1) * QK_DIM]
        qt = jnp.concatenate([qh[:NOPE_DIM], _rope_t(qh[NOPE_DIM:], cos_t, sin_t)], axis=0) * qscale
        qat_ref[h, 0] = _pad_rows(qt, LANES).astype(BF16)
        kt = jnp.concatenate([knt_all[h * NOPE_DIM:(h + 1) * NOPE_DIM], krt], axis=0)
        ka_ref[h] = _pad_rows(kt, LANES).T.astype(BF16)
        vat_ref[h, 0] = _with_ones_row(_pad_rows(vt_all[h * V_DIM:(h + 1) * V_DIM], LANES)).astype(BF16)


def _inproj(h, g, rope_t, wgate, wtr, qngt, kvngt, wqt, wkt, wvt, seq):
    m = h.shape[0]
    nt = m // TOK_TILE
    per_seq = seq // TOK_TILE
    per_group = MLA_KEY_GROUP // TOK_TILE
    tok = lambda w: pl.BlockSpec((TOK_TILE, w), lambda i: (i, 0))
    heads = lambda n: pl.BlockSpec((n, TOK_TILE, LANES), lambda i: (0, i, 0))
    heads_t = lambda n: pl.BlockSpec((n, LANES, TOK_TILE), lambda i: (0, 0, i))
    out_shape = (
        jax.ShapeDtypeStruct((HA, nt, LANES, TOK_TILE), BF16),
        jax.ShapeDtypeStruct((HA, m, LANES), BF16),
        jax.ShapeDtypeStruct((HA, m // MLA_KEY_GROUP, LANES, MLA_KEY_GROUP), BF16),
        jax.ShapeDtypeStruct((HB, LANES, m), BF16),
        jax.ShapeDtypeStruct((KVH, m, LANES), BF16),
        jax.ShapeDtypeStruct((KVH, LANES, m), BF16),
        jax.ShapeDtypeStruct((m, D_MODEL), BF16),
        jax.ShapeDtypeStruct((m, D_MODEL), BF16),
    )
    out_specs = (
        pl.BlockSpec((HA, 1, LANES, TOK_TILE), lambda i: (0, i, 0, 0)),
        heads(HA),
        pl.BlockSpec((HA, 1, LANES, TOK_TILE), lambda i: (0, i // per_group, 0, i % per_group)),
        heads_t(HB),
        heads(KVH),
        heads_t(KVH),
        tok(D_MODEL),
        tok(D_MODEL),
    )
    return pl.pallas_call(
        _inproj_kernel,
        grid=(nt,),
        in_specs=[tok(D_MODEL), _const_spec((1, D_MODEL)),
                  pl.BlockSpec((ROPE_DIM, TOK_TILE), lambda i: (0, i % per_seq)),
                  _const_spec(wgate.shape), _const_spec(wtr.shape),
                  _const_spec((Q_LORA, 1)), _const_spec((KV_LORA, 1)),
                  _const_spec(wqt.shape), _const_spec(wkt.shape), _const_spec(wvt.shape)],
        out_specs=out_specs,
        out_shape=out_shape,
        compiler_params=_params(1),
        name="inproj",
    )(h, g, rope_t, wgate, wtr, qngt, kvngt, wqt, wkt, wvt)


def _mla_kernel(qt_ref, qn_ref, k_ref, vt_ref, o_ref, s_ref, m_ref, q_all, o_acc, *, n_groups, q_tiles):
    gk, tq, ck, half = MLA_KEY_GROUP, MLA_Q_TILE, MLA_KEY_GROUP // MLA_SUB, MLA_SUB // 2
    n_items = q_tiles * 2 * n_groups

    for qi in range(q_tiles):
        for j in range(2):
            q_all[qi * 2 + j] = qt_ref[j, qi]
    for j in range(2):
        q_all[2 * q_tiles + j] = qn_ref[j, 0]

    def produce(slot, n, c):
        head, group = (n // n_groups) % 2, n % n_groups
        start = pl.multiple_of(group * gk + c * ck, ck)
        s = _dot(k_ref[head, pl.ds(start, ck), :], q_all[n // n_groups])
        s_ref[slot, c * ck:(c + 1) * ck, :] = s
        return jnp.max(s, axis=0, keepdims=True)

    def produce_half(slot, n, which):
        mh = None
        for c in range(which * half, (which + 1) * half):
            mc = produce(slot, n, c)
            mh = mc if mh is None else jnp.maximum(mh, mc)
        m_ref[slot, which] = mh

    def step(slot, n, carry):
        head, group = (n // n_groups) % 2, n % n_groups
        m_old, acc = carry
        m_old = jnp.where(group == 0, -jnp.inf, m_old)
        m_new = jnp.maximum(m_old, jnp.maximum(m_ref[slot, 0], m_ref[slot, 1]))
        pv = mh = None
        for c in range(MLA_SUB):
            mc = produce(1 - slot, n + 1, c + half) if c < half else produce(slot, n + 2, c - half)
            mh = mc if c % half == 0 else jnp.maximum(mh, mc)
            rows = slice(c * ck, (c + 1) * ck)
            p = jnp.exp2(s_ref[slot, rows, :] - m_new).astype(BF16)
            d = _dot(vt_ref[head, group, :, rows], p)
            pv = d if pv is None else pv + d
            if c == half - 1:
                m_ref[1 - slot, 1] = mh
        m_ref[slot, 0] = mh
        acc = jnp.exp2(m_old - m_new) * acc + pv
        o_acc[n // (2 * n_groups), pl.ds(pl.multiple_of(head * V_DIM, V_DIM), V_DIM), :] = (
            acc[:V_DIM] / acc[V_DIM:V_DIM + 1])
        return m_new, acc

    @pl.when(pl.program_id(2) == 0)
    def _():
        produce_half(0, 0, 0)
        produce_half(0, 0, 1)
        produce_half(1, 1, 0)

    def body(t, carry):
        for u in range(MLA_UNROLL):
            carry = step(u % 2, MLA_UNROLL * t + u, carry)
        return carry

    lax.fori_loop(0, n_items // MLA_UNROLL, body,
                  (jnp.full((1, tq), -jnp.inf, F32), jnp.zeros((LANES, tq), F32)))
    for qi in range(q_tiles):
        o_ref[qi * tq:(qi + 1) * tq, :] = o_acc[qi].T.astype(BF16)


def _mla(qat, ka, vat, batch, seq):
    m = ka.shape[1]
    n_groups = seq // MLA_KEY_GROUP
    q_tiles = min(max(1, MLA_ITEMS_PER_STEP // (2 * n_groups)), seq // MLA_Q_TILE)
    tqs = q_tiles * MLA_Q_TILE
    nq = seq // tqs
    last_tile = m // MLA_Q_TILE - 1
    return pl.pallas_call(
        functools.partial(_mla_kernel, n_groups=n_groups, q_tiles=q_tiles),
        grid=(batch, HA // 2, nq),
        in_specs=[
            pl.BlockSpec((2, q_tiles, LANES, MLA_Q_TILE), lambda b, hp, i: (hp, b * nq + i, 0, 0)),
            pl.BlockSpec((2, 1, LANES, MLA_Q_TILE),
                         lambda b, hp, i: (hp, jnp.minimum((b * nq + i + 1) * q_tiles, last_tile), 0, 0)),
            pl.BlockSpec((2, seq, LANES), lambda b, hp, i: (hp, b, 0)),
            pl.BlockSpec((2, n_groups, LANES, MLA_KEY_GROUP), lambda b, hp, i: (hp, b, 0, 0)),
        ],
        out_specs=pl.BlockSpec((tqs, LANES), lambda b, hp, i: (b * nq + i, hp)),
        out_shape=jax.ShapeDtypeStruct((m, HA * V_DIM), BF16),
        scratch_shapes=[pltpu.VMEM((2, MLA_KEY_GROUP, MLA_Q_TILE), F32),
                        pltpu.VMEM((2, 2, 1, MLA_Q_TILE), F32),
                        pltpu.VMEM((2 * q_tiles + 2, LANES, MLA_Q_TILE), BF16),
                        pltpu.VMEM((q_tiles, LANES, MLA_Q_TILE), F32)],
        compiler_params=_params(3),
        name="mla",
    )(qat, qat, ka, vat)


def _bias_kernel(bucket_ref, relb_ref, o_ref):
    bucket = bucket_ref[...]
    key = lax.broadcasted_iota(jnp.int32, bucket.shape, 0)
    tq = WIN_Q_TILE
    for h in range(HB):
        acc = jnp.full(bucket.shape, NEG, F32)
        for b in range(REL_BUCKETS):
            acc = jnp.where(bucket == b, relb_ref[b * HB + h] * LOG2E, acc)
        g, cols = h // REP, slice((h % REP) * tq, (h % REP + 1) * tq)
        o_ref[0, g, :, cols] = jnp.where(key < WINDOW, NEG, acc)
        o_ref[1, g, :, cols] = acc
        o_ref[2, g, :, cols] = jnp.where(key >= WINDOW + tq, NEG, acc)


def _bias_table(rel_bias):
    c = jnp.arange(2 * WIN_Q_TILE, dtype=jnp.int32)[:, None]
    r = jnp.arange(WIN_Q_TILE, dtype=jnp.int32)[None, :]
    rel = c - WINDOW - r
    nb = REL_BUCKETS // 2
    max_exact = nb // 2
    ret = jnp.where(rel > 0, nb, 0)
    n = jnp.abs(rel)
    nf = jnp.maximum(n, 1).astype(F32)
    large = max_exact + (jnp.log(nf / max_exact) / math.log(REL_MAX_DIST / max_exact)
                         * (nb - max_exact)).astype(jnp.int32)
    large = jnp.minimum(large, nb - 1)
    bucket = ret + jnp.where(n < max_exact, n, large)
    bucket = jnp.where(n <= WINDOW, bucket, -1)
    return pl.pallas_call(
        _bias_kernel,
        in_specs=[pl.BlockSpec(memory_space=pltpu.VMEM), pl.BlockSpec(memory_space=pltpu.SMEM)],
        out_specs=pl.BlockSpec(memory_space=pltpu.VMEM),
        out_shape=jax.ShapeDtypeStruct((3, KVH, 2 * WIN_Q_TILE, REP * WIN_Q_TILE), F32),
        name="rel_bias_table",
    )(bucket, rel_bias.reshape(-1))


def _win_kernel(sink_ref, qt_ref, bias_lo_ref, bias_hi_ref, kp_ref, kc_ref, kn_ref,
                vtp_ref, vtc_ref, vtn_ref, o_ref, s_ref):
    tq, edge = WIN_Q_TILE, WIN_STEP - WINDOW
    chains = [(g, t) for g in range(KVH) for t in range(WIN_STEP // tq)]
    bias_refs = (bias_lo_ref, bias_hi_ref)

    def keys(t):
        return slice(t * tq, t * tq + 2 * tq)

    def scores(slot, g, t):
        k_span = jnp.concatenate([kp_ref[g][edge:], kc_ref[g], kn_ref[g][:WINDOW]], axis=0)
        qt = jnp.concatenate([qt_ref[g * REP + r][:, t * tq:(t + 1) * tq] for r in range(REP)],
                             axis=1)
        s_ref[slot] = _dot(k_span[keys(t)], qt) + bias_refs[t][0, g]

    def consume(slot, g, t):
        s = s_ref[slot]
        sink = jnp.concatenate(
            [jnp.full((1, tq), sink_ref[g * REP + r] * LOG2E, F32) for r in range(REP)], axis=1)
        m = jnp.maximum(jnp.max(s, axis=0, keepdims=True), sink)
        p = jnp.exp2(s - m).astype(BF16)
        sink_p = jnp.exp2(sink - m)
        vt_span = jnp.concatenate([vtp_ref[g][:, edge:], vtc_ref[g], vtn_ref[g][:, :WINDOW]], axis=1)
        halves = []
        for r in range(REP):
            cols = slice(r * tq, (r + 1) * tq)
            pv = _dot(vt_span[:, keys(t)], p[:, cols])
            halves.append(pv[:HD] / (pv[V_DIM:V_DIM + 1] + sink_p[:, cols]))
        o_ref[t * tq:(t + 1) * tq, g * REP * HD:(g + 1) * REP * HD] = (
            jnp.concatenate(halves, axis=0).T.astype(BF16))

    scores(0, *chains[0])
    for n, chain in enumerate(chains):
        if n + 1 < len(chains):
            scores((n + 1) % 2, *chains[n + 1])
        consume(n % 2, *chain)


def _win(sink, qbt, bias, kb, vbt, batch, seq):
    m = kb.shape[1]
    ts = WIN_STEP
    nt = seq // ts

    def neighbour(i, off):
        return jnp.clip(i + off, 0, nt - 1)

    def k_spec(off):
        return pl.BlockSpec((KVH, ts, LANES), lambda b, i: (0, b * nt + neighbour(i, off), 0))

    def vt_spec(off):
        return pl.BlockSpec((KVH, LANES, ts), lambda b, i: (0, 0, b * nt + neighbour(i, off)))

    bias_block = (1, KVH, 2 * WIN_Q_TILE, REP * WIN_Q_TILE)
    return pl.pallas_call(
        _win_kernel,
        grid=(batch, nt),
        in_specs=[
            pl.BlockSpec(memory_space=pltpu.SMEM),
            pl.BlockSpec((HB, LANES, ts), lambda b, i: (0, 0, b * nt + i)),
            pl.BlockSpec(bias_block, lambda b, i: (jnp.where(i == 0, 0, 1), 0, 0, 0)),
            pl.BlockSpec(bias_block, lambda b, i: (jnp.where(i == nt - 1, 2, 1), 0, 0, 0)),
            k_spec(-1), k_spec(0), k_spec(1),
            vt_spec(-1), vt_spec(0), vt_spec(1),
        ],
        out_specs=pl.BlockSpec((ts, HB * HD), lambda b, i: (b * nt + i, 0)),
        out_shape=jax.ShapeDtypeStruct((m, HB * HD), BF16),
        scratch_shapes=[pltpu.VMEM((2, 2 * WIN_Q_TILE, REP * WIN_Q_TILE), F32)],
        compiler_params=_params(2),
        name="window",
    )(sink, qbt, bias, bias, kb, kb, kb, vbt, vbt, vbt)


def _tail_kernel(h_ref, ya_ref, yb_ref, ga_ref, gb_ref, p_ref, gains_ref,
                 pa_ref, pb_ref, wo_ref, w1_ref, w3_ref, w2_ref, wg_ref, wp_ref, o_ref):
    g_mix, g_pre2, g_post2, g_pre_e, g_post_e = (gains_ref[k:k + 1] for k in range(5))
    ma = _dot(ya_ref[...], pa_ref[...])
    mb = _dot(yb_ref[...], pb_ref[...])
    mix = (ga_ref[...].astype(F32) * ma + gb_ref[...].astype(F32) * mb).astype(BF16)
    h = h_ref[...] + _rms(_dot(mix, wo_ref[...]), g_mix)
    f = _swiglu(_rms(h, g_pre2).astype(BF16), w1_ref, w3_ref, w2_ref)
    h = h + 0.5 * _rms(f, g_post2)
    gate = jax.nn.sigmoid(_dot(_rms(h, g_pre_e).astype(BF16), wg_ref[...]))
    e = _dot(p_ref[...].astype(BF16), wp_ref[...]) * gate
    o_ref[...] = h + _rms(e, g_post_e)


def _tail(h, ya, yb, ga, gb, p, gains, pa, pb, wo, w1, w3, w2, wg, wp):
    m = h.shape[0]
    tok = lambda w: pl.BlockSpec((TOK_TILE, w), lambda i: (i, 0))
    resident = lambda a: pl.BlockSpec(a.shape, lambda i: (0,) * a.ndim, pipeline_mode=pl.Buffered(1))
    weights = (pa, pb, wo, w1, w3, w2, wg, wp)
    return pl.pallas_call(
        _tail_kernel,
        grid=(m // TOK_TILE,),
        in_specs=[tok(D_MODEL), tok(HA * V_DIM), tok(HB * HD), tok(D_MODEL), tok(D_MODEL),
                  tok(PLE_DIM), _const_spec(gains.shape)] + [resident(a) for a in weights],
        out_specs=tok(D_MODEL),
        out_shape=jax.ShapeDtypeStruct((m, D_MODEL), F32),
        compiler_params=_params(1),
        name="tail",
    )(h, ya, yb, ga, gb, p, gains, *weights)


def _prep_weights(w_in, w_uq, w_uk, w_uv):
    splits = (Q_LORA, KV_LORA, ROPE_DIM, HB * HD, KVH * HD, KVH * HD, D_MODEL, D_MODEL)
    offs = [0]
    for s in splits:
        offs.append(offs[-1] + s)
    w_cq, w_ckv, w_kr, w_qb, w_kb, w_vb, w_ga, w_gb = (
        w_in[:, offs[k]:offs[k + 1]] for k in range(len(splits)))
    wgate = jnp.concatenate([w_ga, w_gb], axis=1).astype(BF16)
    wtr = jnp.concatenate([w_cq, w_ckv, w_kr, w_qb * HD ** -0.5, w_vb, w_kb],
                          axis=1).T.astype(BF16)
    return wgate, wtr, w_uq.T.astype(BF16), w_uk.T.astype(BF16), w_uv.T.astype(BF16)


def _rope_table_t(seq):
    inv = 1.0 / (ROPE_THETA ** (jnp.arange(0, ROPE_DIM, 2, dtype=F32) / ROPE_DIM))
    ang = inv[:, None] * jnp.arange(seq, dtype=F32)[None, :]
    return jnp.concatenate([jnp.cos(ang), jnp.sin(ang)], axis=0)


def _encoder_layer(x, p, bias, sink, wts):
    batch, seq, _ = x.shape
    m = batch * seq
    x2 = x.reshape(m, D_MODEL)
    p2 = p.reshape(m, PLE_DIM)
    h = _ffn(x2, wts["ffn1_pre_g"], wts["ffn1_post_g"], wts["ffn1_w1"], wts["ffn1_w3"], wts["ffn1_w2"])
    qat, ka, vat, qbt, kb, vbt, ga, gb = _inproj(
        h, wts["mix_pre_g"], _rope_table_t(seq), wts["wgate"], wts["wtr"],
        wts["q_norm_gt"], wts["kv_norm_gt"], wts["wqt"], wts["wkt"], wts["wvt"], seq)
    ya = _mla(qat, ka, vat, batch, seq)
    yb = _win(sink, qbt, bias, kb, vbt, batch, seq)
    gains = jnp.concatenate([wts["mix_post_g"], wts["ffn2_pre_g"], wts["ffn2_post_g"],
                             wts["ple_pre_g"], wts["ple_post_g"]], axis=0)
    out = _tail(h, ya, yb, ga, gb, p2, gains, wts["w_proj_a"], wts["w_proj_b"], wts["w_out"],
                wts["ffn2_w1"], wts["ffn2_w3"], wts["ffn2_w2"], wts["w_ple_gate"], wts["w_ple_proj"])
    return out.reshape(batch, seq, D_MODEL)


def _layer_weights(layer, w):
    wgate, wtr, wqt, wkt, wvt = _prep_weights(w["w_in"][layer], w["w_uq"][layer], w["w_uk"][layer],
                                              w["w_uv"][layer])
    wts = {
        "wgate": wgate, "wtr": wtr, "wqt": wqt, "wkt": wkt, "wvt": wvt,
        "q_norm_gt": w["q_norm_g"][layer][:, None], "kv_norm_gt": w["kv_norm_g"][layer][:, None],
    }
    for name in ("ffn1_pre_g", "ffn1_post_g", "mix_pre_g", "mix_post_g",
                 "ffn2_pre_g", "ffn2_post_g", "ple_pre_g", "ple_post_g"):
        wts[name] = w[name][layer][None]
    for name in ("ffn1_w1", "ffn1_w3", "ffn1_w2", "w_proj_a", "w_proj_b", "w_out",
                 "ffn2_w1", "ffn2_w3", "ffn2_w2", "w_ple_gate", "w_ple_proj"):
        wts[name] = w[name][layer].astype(BF16)
    return wts


def kernel(x_prompt, x_sample, p_prompt, p_sample, rel_bias, ffn1_pre_g, ffn1_post_g, ffn1_w1, ffn1_w3, ffn1_w2, mix_pre_g, mix_post_g, w_in, q_norm_g, kv_norm_g, w_uq, w_uk, w_uv, sink, w_proj_a, w_proj_b, w_out, ffn2_pre_g, ffn2_post_g, ffn2_w1, ffn2_w3, ffn2_w2, ple_pre_g, ple_post_g, w_ple_gate, w_ple_proj):
    w = dict(ffn1_pre_g=ffn1_pre_g, ffn1_post_g=ffn1_post_g, ffn1_w1=ffn1_w1, ffn1_w3=ffn1_w3,
             ffn1_w2=ffn1_w2, mix_pre_g=mix_pre_g, mix_post_g=mix_post_g, w_in=w_in,
             q_norm_g=q_norm_g, kv_norm_g=kv_norm_g, w_uq=w_uq, w_uk=w_uk, w_uv=w_uv,
             w_proj_a=w_proj_a, w_proj_b=w_proj_b, w_out=w_out, ffn2_pre_g=ffn2_pre_g,
             ffn2_post_g=ffn2_post_g, ffn2_w1=ffn2_w1, ffn2_w3=ffn2_w3, ffn2_w2=ffn2_w2,
             ple_pre_g=ple_pre_g, ple_post_g=ple_post_g, w_ple_gate=w_ple_gate,
             w_ple_proj=w_ple_proj)
    bias = _bias_table(rel_bias)
    y_prompt, y_sample = x_prompt, x_sample
    for layer in range(ffn1_w1.shape[0]):
        wts = _layer_weights(layer, w)
        y_prompt = _encoder_layer(y_prompt, p_prompt[layer], bias, sink[layer], wts)
        y_sample = _encoder_layer(y_sample, p_sample[layer], bias, sink[layer], wts)
    return (y_prompt, y_sample)
```

```python
import functools
import math

import jax
import jax.numpy as jnp
from jax import lax
from jax.experimental import pallas as pl
from jax.experimental.pallas import tpu as pltpu

D_MODEL = 1024
D_FF = 2816
PLE_DIM = 256
HA = 8
Q_LORA = 384
KV_LORA = 256
NOPE_DIM = 64
ROPE_DIM = 32
V_DIM = 64
ROPE_THETA = 10000.0
HB = 8
KVH = 2
REP = HB // KVH
HD = 64
WINDOW = 128
REL_BUCKETS = 32
REL_MAX_DIST = 128
EPS = 1e-6
NEG = -1e30

LANES = 128
VMEM_LIMIT = 56 * 1024 * 1024

TOK_TILE = 512
FF_CHUNK = 256
MLA_Q_TILE = TOK_TILE
MLA_KEY_GROUP = 2048
MLA_SUB = 8
MLA_ITEMS_PER_STEP = 32
MLA_UNROLL = 8
MLA_HEADS_SHORT = 4
WIN_Q_TILE = 2 * WINDOW
WIN_STEP = 2 * WIN_Q_TILE
LOG2E = math.log2(math.e)

F32 = jnp.float32
BF16 = jnp.bfloat16

_T_CQ = 0
_T_CKV = _T_CQ + Q_LORA
_T_KR = _T_CKV + KV_LORA
_T_QB = _T_KR + ROPE_DIM
_T_VB = _T_QB + HB * HD
_T_KB = _T_VB + KVH * HD
_T_END = _T_KB + KVH * HD
QK_DIM = NOPE_DIM + ROPE_DIM


def _params(n_axes):
    return pltpu.CompilerParams(dimension_semantics=("arbitrary",) * n_axes,
                                vmem_limit_bytes=VMEM_LIMIT)


def _const_spec(shape):
    nd = len(shape)
    return pl.BlockSpec(shape, lambda *_: (0,) * nd)


def _rms(x, g, axis=-1):
    return x * lax.rsqrt(jnp.mean(x * x, axis=axis, keepdims=True) + EPS) * g


def _dot(a, b):
    return jnp.dot(a, b, preferred_element_type=F32)


def _with_ones_row(vt):
    row = lax.broadcasted_iota(jnp.int32, vt.shape, 0)
    return jnp.where(row == V_DIM, 1.0, vt)


def _swiglu(xn, w1_ref, w3_ref, w2_ref):
    acc = None
    for c in range(D_FF // FF_CHUNK):
        sl = slice(c * FF_CHUNK, (c + 1) * FF_CHUNK)
        a = _dot(xn, w1_ref[:, sl])
        b = _dot(xn, w3_ref[:, sl])
        act = (a * jax.nn.sigmoid(a) * b).astype(BF16)
        d = _dot(act, w2_ref[sl, :])
        acc = d if acc is None else acc + d
    return acc


def _ffn_kernel(x_ref, gpre_ref, gpost_ref, w1_ref, w3_ref, w2_ref, o_ref):
    x = x_ref[...]
    xn = _rms(x, gpre_ref[...]).astype(BF16)
    f = _swiglu(xn, w1_ref, w3_ref, w2_ref)
    o_ref[...] = x + 0.5 * _rms(f, gpost_ref[...])


def _ffn(x, gpre, gpost, w1, w3, w2):
    m = x.shape[0]
    tok = pl.BlockSpec((TOK_TILE, D_MODEL), lambda i: (i, 0))
    return pl.pallas_call(
        _ffn_kernel,
        grid=(m // TOK_TILE,),
        in_specs=[tok, _const_spec((1, D_MODEL)), _const_spec((1, D_MODEL)),
                  _const_spec(w1.shape), _const_spec(w3.shape), _const_spec(w2.shape)],
        out_specs=tok,
        out_shape=jax.ShapeDtypeStruct((m, D_MODEL), F32),
        compiler_params=_params(1),
        name="ffn",
    )(x, gpre, gpost, w1, w3, w2)


def _pad_rows(x, rows):
    return jnp.concatenate([x, jnp.zeros((rows - x.shape[0], x.shape[1]), x.dtype)], axis=0)


def _rope_t(x, cos_t, sin_t):
    r2 = ROPE_DIM // 2
    x1, x2 = x[:r2], x[r2:]
    return jnp.concatenate([x1 * cos_t - x2 * sin_t, x1 * sin_t + x2 * cos_t], axis=0)


def _inproj_kernel(h_ref, g_ref, rope_ref, wgate_ref, wtr_ref, qngt_ref, kvngt_ref,
                   wqt_ref, wkt_ref, wvt_ref,
                   qat_ref, ka_ref, vat_ref, qbt_ref, kb_ref, vbt_ref, ga_ref, gb_ref):
    u = _rms(h_ref[...], g_ref[...])
    ub = u.astype(BF16)
    ut = u.T.astype(BF16)
    cos_t = rope_ref[:ROPE_DIM // 2]
    sin_t = rope_ref[ROPE_DIM // 2:]

    ga_ref[...] = jax.nn.sigmoid(_dot(ub, wgate_ref[:, :D_MODEL])).astype(BF16)
    gb_ref[...] = jax.nn.sigmoid(_dot(ub, wgate_ref[:, D_MODEL:])).astype(BF16)

    cqt = _rms(_dot(wtr_ref[_T_CQ:_T_CKV, :], ut), qngt_ref[...], axis=0).astype(BF16)
    ckvt = _rms(_dot(wtr_ref[_T_CKV:_T_KR, :], ut), kvngt_ref[...], axis=0).astype(BF16)
    rest = _dot(wtr_ref[_T_KR:_T_END, :], ut)
    off = lambda lo, hi: rest[lo - _T_KR:hi - _T_KR]
    krt = _rope_t(off(_T_KR, _T_QB), cos_t, sin_t)
    qbt = off(_T_QB, _T_VB) * LOG2E
    vbt = off(_T_VB, _T_KB)
    kbt = off(_T_KB, _T_END)
    for hb in range(HB):
        qbt_ref[hb] = _pad_rows(qbt[hb * HD:(hb + 1) * HD], LANES).astype(BF16)
    for g in range(KVH):
        kb_ref[g] = _pad_rows(kbt[g * HD:(g + 1) * HD], LANES).T.astype(BF16)
        vbt_ref[g] = _with_ones_row(_pad_rows(vbt[g * HD:(g + 1) * HD], LANES)).astype(BF16)

    qt_all = _dot(wqt_ref[...], cqt)
    knt_all = _dot(wkt_ref[...], ckvt)
    vt_all = _dot(wvt_ref[...], ckvt)
    qscale = QK_DIM ** -0.5 * LOG2E
    for h in range(HA):
        qh = qt_all[h * QK_DIM:(h + 1) * QK_DIM]
        qt = jnp.concatenate([qh[:NOPE_DIM], _rope_t(qh[NOPE_DIM:], cos_t, sin_t)], axis=0) * qscale
        qat_ref[h, 0] = _pad_rows(qt, LANES).astype(BF16)
        kt = jnp.concatenate([knt_all[h * NOPE_DIM:(h + 1) * NOPE_DIM], krt], axis=0)
        ka_ref[h] = _pad_rows(kt, LANES).T.astype(BF16)
        vat_ref[h, 0] = _with_ones_row(_pad_rows(vt_all[h * V_DIM:(h + 1) * V_DIM], LANES)).astype(BF16)


def _inproj(h, g, rope_t, wgate, wtr, qngt, kvngt, wqt, wkt, wvt, seq):
    m = h.shape[0]
    nt = m // TOK_TILE
    per_seq = seq // TOK_TILE
    per_group = MLA_KEY_GROUP // TOK_TILE
    tok = lambda w: pl.BlockSpec((TOK_TILE, w), lambda i: (i, 0))
    heads = lambda n: pl.BlockSpec((n, TOK_TILE, LANES), lambda i: (0, i, 0))
    heads_t = lambda n: pl.BlockSpec((n, LANES, TOK_TILE), lambda i: (0, 0, i))
    out_shape = (
        jax.ShapeDtypeStruct((HA, nt, LANES, TOK_TILE), BF16),
        jax.ShapeDtypeStruct((HA, m, LANES), BF16),
        jax.ShapeDtypeStruct((HA, m // MLA_KEY_GROUP, LANES, MLA_KEY_GROUP), BF16),
        jax.ShapeDtypeStruct((HB, LANES, m), BF16),
        jax.ShapeDtypeStruct((KVH, m, LANES), BF16),
        jax.ShapeDtypeStruct((KVH, LANES, m), BF16),
        jax.ShapeDtypeStruct((m, D_MODEL), BF16),
        jax.ShapeDtypeStruct((m, D_MODEL), BF16),
    )
    out_specs = (
        pl.BlockSpec((HA, 1, LANES, TOK_TILE), lambda i: (0, i, 0, 0)),
        heads(HA),
        pl.BlockSpec((HA, 1, LANES, TOK_TILE), lambda i: (0, i // per_group, 0, i % per_group)),
        heads_t(HB),
        heads(KVH),
        heads_t(KVH),
        tok(D_MODEL),
        tok(D_MODEL),
    )
    return pl.pallas_call(
        _inproj_kernel,
        grid=(nt,),
        in_specs=[tok(D_MODEL), _const_spec((1, D_MODEL)),
                  pl.BlockSpec((ROPE_DIM, TOK_TILE), lambda i: (0, i % per_seq)),
                  _const_spec(wgate.shape), _const_spec(wtr.shape),
                  _const_spec((Q_LORA, 1)), _const_spec((KV_LORA, 1)),
                  _const_spec(wqt.shape), _const_spec(wkt.shape), _const_spec(wvt.shape)],
        out_specs=out_specs,
        out_shape=out_shape,
        compiler_params=_params(1),
        name="inproj",
    )(h, g, rope_t, wgate, wtr, qngt, kvngt, wqt, wkt, wvt)


def _mla_kernel(qt_ref, qn_ref, k_ref, vt_ref, o_ref, s_ref, m_ref, q_all, o_acc, *,
                n_groups, q_tiles, heads):
    gk, tq, ck, half = MLA_KEY_GROUP, MLA_Q_TILE, MLA_KEY_GROUP // MLA_SUB, MLA_SUB // 2
    n_items = q_tiles * heads * n_groups

    for qi in range(q_tiles):
        for j in range(heads):
            q_all[qi * heads + j] = qt_ref[j, qi]
    for j in range(2):
        q_all[heads * q_tiles + j] = qn_ref[j, 0]

    def produce(slot, n, c):
        head, group = (n // n_groups) % heads, n % n_groups
        start = pl.multiple_of(group * gk + c * ck, ck)
        s = _dot(k_ref[head, pl.ds(start, ck), :], q_all[n // n_groups])
        s_ref[slot, c * ck:(c + 1) * ck, :] = s
        return jnp.max(s, axis=0, keepdims=True)

    def produce_half(slot, n, which):
        mh = None
        for c in range(which * half, (which + 1) * half):
            mc = produce(slot, n, c)
            mh = mc if mh is None else jnp.maximum(mh, mc)
        m_ref[slot, which] = mh

    def step(slot, n, carry):
        head, group = (n // n_groups) % heads, n % n_groups
        m_old, acc = carry
        m_old = jnp.where(group == 0, -jnp.inf, m_old)
        m_new = jnp.maximum(m_old, jnp.maximum(m_ref[slot, 0], m_ref[slot, 1]))
        pv = mh = None
        for c in range(MLA_SUB):
            mc = produce(1 - slot, n + 1, c + half) if c < half else produce(slot, n + 2, c - half)
            mh = mc if c % half == 0 else jnp.maximum(mh, mc)
            rows = slice(c * ck, (c + 1) * ck)
            p = jnp.exp2(s_ref[slot, rows, :] - m_new).astype(BF16)
            d = _dot(vt_ref[head, group, :, rows], p)
            pv = d if pv is None else pv + d
            if c == half - 1:
                m_ref[1 - slot, 1] = mh
        m_ref[slot, 0] = mh
        acc = jnp.exp2(m_old - m_new) * acc + pv
        o_acc[n // (heads * n_groups), pl.ds(pl.multiple_of(head * V_DIM, V_DIM), V_DIM), :] = (
            acc[:V_DIM] / acc[V_DIM:V_DIM + 1])
        return m_new, acc

    @pl.when(pl.program_id(2) == 0)
    def _():
        produce_half(0, 0, 0)
        produce_half(0, 0, 1)
        produce_half(1, 1, 0)

    unroll = min(MLA_UNROLL, n_items // 2)

    def body(t, carry):
        for u in range(unroll):
            carry = step(u % 2, unroll * t + u, carry)
        return carry

    lax.fori_loop(0, n_items // unroll, body,
                  (jnp.full((1, tq), -jnp.inf, F32), jnp.zeros((LANES, tq), F32)))
    for qi in range(q_tiles):
        o_ref[qi * tq:(qi + 1) * tq, :] = o_acc[qi].T.astype(BF16)


def _mla(qat, ka, vat, batch, seq):
    m = ka.shape[1]
    n_groups = seq // MLA_KEY_GROUP
    hs = 2 if n_groups > 1 else MLA_HEADS_SHORT
    q_tiles = min(max(1, MLA_ITEMS_PER_STEP // (hs * n_groups)), seq // MLA_Q_TILE)
    tqs = q_tiles * MLA_Q_TILE
    nq = seq // tqs
    last_tile = m // MLA_Q_TILE - 1
    return pl.pallas_call(
        functools.partial(_mla_kernel, n_groups=n_groups, q_tiles=q_tiles, heads=hs),
        grid=(batch, HA // hs, nq),
        in_specs=[
            pl.BlockSpec((hs, q_tiles, LANES, MLA_Q_TILE), lambda b, hp, i: (hp, b * nq + i, 0, 0)),
            pl.BlockSpec((2, 1, LANES, MLA_Q_TILE),
                         lambda b, hp, i: (hp * (hs // 2),
                                           jnp.minimum((b * nq + i + 1) * q_tiles, last_tile), 0, 0)),
            pl.BlockSpec((hs, seq, LANES), lambda b, hp, i: (hp, b, 0)),
            pl.BlockSpec((hs, n_groups, LANES, MLA_KEY_GROUP), lambda b, hp, i: (hp, b, 0, 0)),
        ],
        out_specs=pl.BlockSpec((tqs, hs * V_DIM), lambda b, hp, i: (b * nq + i, hp)),
        out_shape=jax.ShapeDtypeStruct((m, HA * V_DIM), BF16),
        scratch_shapes=[pltpu.VMEM((2, MLA_KEY_GROUP, MLA_Q_TILE), F32),
                        pltpu.VMEM((2, 2, 1, MLA_Q_TILE), F32),
                        pltpu.VMEM((hs * q_tiles + 2, LANES, MLA_Q_TILE), BF16),
                        pltpu.VMEM((q_tiles, hs * V_DIM, MLA_Q_TILE), F32)],
        compiler_params=_params(3),
        name="mla",
    )(qat, qat, ka, vat)


def _bias_kernel(bucket_ref, relb_ref, o_ref):
    bucket = bucket_ref[...]
    key = lax.broadcasted_iota(jnp.int32, bucket.shape, 0)
    tq = WIN_Q_TILE
    for h in range(HB):
        acc = jnp.full(bucket.shape, NEG, F32)
        for b in range(REL_BUCKETS):
            acc = jnp.where(bucket == b, relb_ref[b * HB + h] * LOG2E, acc)
        g, cols = h // REP, slice((h % REP) * tq, (h % REP + 1) * tq)
        o_ref[0, g, :, cols] = jnp.where(key < WINDOW, NEG, acc)
        o_ref[1, g, :, cols] = acc
        o_ref[2, g, :, cols] = jnp.where(key >= WINDOW + tq, NEG, acc)


def _bias_table(rel_bias):
    c = jnp.arange(2 * WIN_Q_TILE, dtype=jnp.int32)[:, None]
    r = jnp.arange(WIN_Q_TILE, dtype=jnp.int32)[None, :]
    rel = c - WINDOW - r
    nb = REL_BUCKETS // 2
    max_exact = nb // 2
    ret = jnp.where(rel > 0, nb, 0)
    n = jnp.abs(rel)
    nf = jnp.maximum(n, 1).astype(F32)
    large = max_exact + (jnp.log(nf / max_exact) / math.log(REL_MAX_DIST / max_exact)
                         * (nb - max_exact)).astype(jnp.int32)
    large = jnp.minimum(large, nb - 1)
    bucket = ret + jnp.where(n < max_exact, n, large)
    bucket = jnp.where(n <= WINDOW, bucket, -1)
    return pl.pallas_call(
        _bias_kernel,
        in_specs=[pl.BlockSpec(memory_space=pltpu.VMEM), pl.BlockSpec(memory_space=pltpu.SMEM)],
        out_specs=pl.BlockSpec(memory_space=pltpu.VMEM),
        out_shape=jax.ShapeDtypeStruct((3, KVH, 2 * WIN_Q_TILE, REP * WIN_Q_TILE), F32),
        name="rel_bias_table",
    )(bucket, rel_bias.reshape(-1))


def _win_kernel(sink_ref, qt_ref, bias_lo_ref, bias_hi_ref, kp_ref, kc_ref, kn_ref,
                vtp_ref, vtc_ref, vtn_ref, o_ref, s_ref):
    tq, edge = WIN_Q_TILE, WIN_STEP - WINDOW
    chains = [(g, t) for g in range(KVH) for t in range(WIN_STEP // tq)]
    bias_refs = (bias_lo_ref, bias_hi_ref)

    def keys(t):
        return slice(t * tq, t * tq + 2 * tq)

    def scores(slot, g, t):
        k_span = jnp.concatenate([kp_ref[g][edge:], kc_ref[g], kn_ref[g][:WINDOW]], axis=0)
        qt = jnp.concatenate([qt_ref[g * REP + r][:, t * tq:(t + 1) * tq] for r in range(REP)],
                             axis=1)
        s_ref[slot] = _dot(k_span[keys(t)], qt) + bias_refs[t][0, g]

    def consume(slot, g, t):
        s = s_ref[slot]
        sink = jnp.concatenate(
            [jnp.full((1, tq), sink_ref[g * REP + r] * LOG2E, F32) for r in range(REP)], axis=1)
        m = jnp.maximum(jnp.max(s, axis=0, keepdims=True), sink)
        p = jnp.exp2(s - m).astype(BF16)
        sink_p = jnp.exp2(sink - m)
        vt_span = jnp.concatenate([vtp_ref[g][:, edge:], vtc_ref[g], vtn_ref[g][:, :WINDOW]], axis=1)
        halves = []
        for r in range(REP):
            cols = slice(r * tq, (r + 1) * tq)
            pv = _dot(vt_span[:, keys(t)], p[:, cols])
            halves.append(pv[:HD] / (pv[V_DIM:V_DIM + 1] + sink_p[:, cols]))
        o_ref[t * tq:(t + 1) * tq, g * REP * HD:(g + 1) * REP * HD] = (
            jnp.concatenate(halves, axis=0).T.astype(BF16))

    scores(0, *chains[0])
    for n, chain in enumerate(chains):
        if n + 1 < len(chains):
            scores((n + 1) % 2, *chains[n + 1])
        consume(n % 2, *chain)


def _win(sink, qbt, bias, kb, vbt, batch, seq):
    m = kb.shape[1]
    ts = WIN_STEP
    nt = seq // ts

    def neighbour(i, off):
        return jnp.clip(i + off, 0, nt - 1)

    def k_spec(off):
        return pl.BlockSpec((KVH, ts, LANES), lambda b, i: (0, b * nt + neighbour(i, off), 0))

    def vt_spec(off):
        return pl.BlockSpec((KVH, LANES, ts), lambda b, i: (0, 0, b * nt + neighbour(i, off)))

    bias_block = (1, KVH, 2 * WIN_Q_TILE, REP * WIN_Q_TILE)
    return pl.pallas_call(
        _win_kernel,
        grid=(batch, nt),
        in_specs=[
            pl.BlockSpec(memory_space=pltpu.SMEM),
            pl.BlockSpec((HB, LANES, ts), lambda b, i: (0, 0, b * nt + i)),
            pl.BlockSpec(bias_block, lambda b, i: (jnp.where(i == 0, 0, 1), 0, 0, 0)),
            pl.BlockSpec(bias_block, lambda b, i: (jnp.where(i == nt - 1, 2, 1), 0, 0, 0)),
            k_spec(-1), k_spec(0), k_spec(1),
            vt_spec(-1), vt_spec(0), vt_spec(1),
        ],
        out_specs=pl.BlockSpec((ts, HB * HD), lambda b, i: (b * nt + i, 0)),
        out_shape=jax.ShapeDtypeStruct((m, HB * HD), BF16),
        scratch_shapes=[pltpu.VMEM((2, 2 * WIN_Q_TILE, REP * WIN_Q_TILE), F32)],
        compiler_params=_params(2),
        name="window",
    )(sink, qbt, bias, bias, kb, kb, kb, vbt, vbt, vbt)


def _tail_kernel(h_ref, ya_ref, yb_ref, ga_ref, gb_ref, p_ref, gains_ref,
                 pa_ref, pb_ref, wo_ref, w1_ref, w3_ref, w2_ref, wg_ref, wp_ref, o_ref):
    g_mix, g_pre2, g_post2, g_pre_e, g_post_e = (gains_ref[k:k + 1] for k in range(5))
    ma = _dot(ya_ref[...], pa_ref[...])
    mb = _dot(yb_ref[...], pb_ref[...])
    mix = (ga_ref[...].astype(F32) * ma + gb_ref[...].astype(F32) * mb).astype(BF16)
    h = h_ref[...] + _rms(_dot(mix, wo_ref[...]), g_mix)
    f = _swiglu(_rms(h, g_pre2).astype(BF16), w1_ref, w3_ref, w2_ref)
    h = h + 0.5 * _rms(f, g_post2)
    gate = jax.nn.sigmoid(_dot(_rms(h, g_pre_e).astype(BF16), wg_ref[...]))
    e = _dot(p_ref[...].astype(BF16), wp_ref[...]) * gate
    o_ref[...] = h + _rms(e, g_post_e)


def _tail(h, ya, yb, ga, gb, p, gains, pa, pb, wo, w1, w3, w2, wg, wp):
    m = h.shape[0]
    tok = lambda w: pl.BlockSpec((TOK_TILE, w), lambda i: (i, 0))
    resident = lambda a: pl.BlockSpec(a.shape, lambda i: (0,) * a.ndim, pipeline_mode=pl.Buffered(1))
    weights = (pa, pb, wo, w1, w3, w2, wg, wp)
    return pl.pallas_call(
        _tail_kernel,
        grid=(m // TOK_TILE,),
        in_specs=[tok(D_MODEL), tok(HA * V_DIM), tok(HB * HD), tok(D_MODEL), tok(D_MODEL),
                  tok(PLE_DIM), _const_spec(gains.shape)] + [resident(a) for a in weights],
        out_specs=tok(D_MODEL),
        out_shape=jax.ShapeDtypeStruct((m, D_MODEL), F32),
        compiler_params=_params(1),
        name="tail",
    )(h, ya, yb, ga, gb, p, gains, *weights)


def _prep_weights(w_in, w_uq, w_uk, w_uv):
    splits = (Q_LORA, KV_LORA, ROPE_DIM, HB * HD, KVH * HD, KVH * HD, D_MODEL, D_MODEL)
    offs = [0]
    for s in splits:
        offs.append(offs[-1] + s)
    w_cq, w_ckv, w_kr, w_qb, w_kb, w_vb, w_ga, w_gb = (
        w_in[:, offs[k]:offs[k + 1]] for k in range(len(splits)))
    wgate = jnp.concatenate([w_ga, w_gb], axis=1).astype(BF16)
    wtr = jnp.concatenate([w_cq, w_ckv, w_kr, w_qb * HD ** -0.5, w_vb, w_kb],
                          axis=1).T.astype(BF16)
    return wgate, wtr, w_uq.T.astype(BF16), w_uk.T.astype(BF16), w_uv.T.astype(BF16)


def _rope_table_t(seq):
    inv = 1.0 / (ROPE_THETA ** (jnp.arange(0, ROPE_DIM, 2, dtype=F32) / ROPE_DIM))
    ang = inv[:, None] * jnp.arange(seq, dtype=F32)[None, :]
    return jnp.concatenate([jnp.cos(ang), jnp.sin(ang)], axis=0)


def _encoder_layer(x, p, bias, sink, wts):
    batch, seq, _ = x.shape
    m = batch * seq
    x2 = x.reshape(m, D_MODEL)
    p2 = p.reshape(m, PLE_DIM)
    h = _ffn(x2, wts["ffn1_pre_g"], wts["ffn1_post_g"], wts["ffn1_w1"], wts["ffn1_w3"], wts["ffn1_w2"])
    qat, ka, vat, qbt, kb, vbt, ga, gb = _inproj(
        h, wts["mix_pre_g"], _rope_table_t(seq), wts["wgate"], wts["wtr"],
        wts["q_norm_gt"], wts["kv_norm_gt"], wts["wqt"], wts["wkt"], wts["wvt"], seq)
    ya = _mla(qat, ka, vat, batch, seq)
    yb = _win(sink, qbt, bias, kb, vbt, batch, seq)
    gains = jnp.concatenate([wts["mix_post_g"], wts["ffn2_pre_g"], wts["ffn2_post_g"],
                             wts["ple_pre_g"], wts["ple_post_g"]], axis=0)
    out = _tail(h, ya, yb, ga, gb, p2, gains, wts["w_proj_a"], wts["w_proj_b"], wts["w_out"],
                wts["ffn2_w1"], wts["ffn2_w3"], wts["ffn2_w2"], wts["w_ple_gate"], wts["w_ple_proj"])
    return out.reshape(batch, seq, D_MODEL)


def _layer_weights(layer, w):
    wgate, wtr, wqt, wkt, wvt = _prep_weights(w["w_in"][layer], w["w_uq"][layer], w["w_uk"][layer],
                                              w["w_uv"][layer])
    wts = {
        "wgate": wgate, "wtr": wtr, "wqt": wqt, "wkt": wkt, "wvt": wvt,
        "q_norm_gt": w["q_norm_g"][layer][:, None], "kv_norm_gt": w["kv_norm_g"][layer][:, None],
    }
    for name in ("ffn1_pre_g", "ffn1_post_g", "mix_pre_g", "mix_post_g",
                 "ffn2_pre_g", "ffn2_post_g", "ple_pre_g", "ple_post_g"):
        wts[name] = w[name][layer][None]
    for name in ("ffn1_w1", "ffn1_w3", "ffn1_w2", "w_proj_a", "w_proj_b", "w_out",
                 "ffn2_w1", "ffn2_w3", "ffn2_w2", "w_ple_gate", "w_ple_proj"):
        wts[name] = w[name][layer].astype(BF16)
    return wts


def kernel(x_prompt, x_sample, p_prompt, p_sample, rel_bias, ffn1_pre_g, ffn1_post_g, ffn1_w1, ffn1_w3, ffn1_w2, mix_pre_g, mix_post_g, w_in, q_norm_g, kv_norm_g, w_uq, w_uk, w_uv, sink, w_proj_a, w_proj_b, w_out, ffn2_pre_g, ffn2_post_g, ffn2_w1, ffn2_w3, ffn2_w2, ple_pre_g, ple_post_g, w_ple_gate, w_ple_proj):
    w = dict(ffn1_pre_g=ffn1_pre_g, ffn1_post_g=ffn1_post_g, ffn1_w1=ffn1_w1, ffn1_w3=ffn1_w3,
             ffn1_w2=ffn1_w2, mix_pre_g=mix_pre_g, mix_post_g=mix_post_g, w_in=w_in,
             q_norm_g=q_norm_g, kv_norm_g=kv_norm_g, w_uq=w_uq, w_uk=w_uk, w_uv=w_uv,
             w_proj_a=w_proj_a, w_proj_b=w_proj_b, w_out=w_out, ffn2_pre_g=ffn2_pre_g,
             ffn2_post_g=ffn2_post_g, ffn2_w1=ffn2_w1, ffn2_w3=ffn2_w3, ffn2_w2=ffn2_w2,
             ple_pre_g=ple_pre_g, ple_post_g=ple_post_g, w_ple_gate=w_ple_gate,
             w_ple_proj=w_ple_proj)
    bias = _bias_table(rel_bias)
    y_prompt, y_sample = x_prompt, x_sample
    for layer in range(ffn1_w1.shape[0]):
        wts = _layer_weights(layer, w)
        y_prompt = _encoder_layer(y_prompt, p_prompt[layer], bias, sink[layer], wts)
        y_sample = _encoder_layer(y_sample, p_sample[layer], bias, sink[layer], wts)
    return (y_prompt, y_sample)
```

```python
import functools
import math

import jax
import jax.numpy as jnp
from jax import lax
from jax.experimental import pallas as pl
from jax.experimental.pallas import tpu as pltpu

D_MODEL = 1024
D_FF = 2816
PLE_DIM = 256
HA = 8
Q_LORA = 384
KV_LORA = 256
NOPE_DIM = 64
ROPE_DIM = 32
V_DIM = 64
ROPE_THETA = 10000.0
HB = 8
KVH = 2
REP = HB // KVH
HD = 64
WINDOW = 128
REL_BUCKETS = 32
REL_MAX_DIST = 128
EPS = 1e-6
NEG = -1e30

LANES = 128
BF16_SUBLANES = 16
V_ROWS = -(-(V_DIM + 1) // BF16_SUBLANES) * BF16_SUBLANES
VMEM_LIMIT = 56 * 1024 * 1024

TOK_TILE = 512
FF_CHUNK = 256
MLA_Q_TILE = TOK_TILE
MLA_KEY_GROUP = 2048
MLA_SUB = 8
MLA_ITEMS_PER_STEP = 32
MLA_UNROLL = 8
MLA_HEADS_SHORT = 4
WIN_Q_TILE = 2 * WINDOW
WIN_STEP = 2 * WIN_Q_TILE
LOG2E = math.log2(math.e)

F32 = jnp.float32
BF16 = jnp.bfloat16

_T_CQ = 0
_T_CKV = _T_CQ + Q_LORA
_T_KR = _T_CKV + KV_LORA
_T_QB = _T_KR + ROPE_DIM
_T_VB = _T_QB + HB * HD
_T_KB = _T_VB + KVH * HD
_T_END = _T_KB + KVH * HD
QK_DIM = NOPE_DIM + ROPE_DIM


def _params(n_axes):
    return pltpu.CompilerParams(dimension_semantics=("arbitrary",) * n_axes,
                                vmem_limit_bytes=VMEM_LIMIT)


def _const_spec(shape):
    nd = len(shape)
    return pl.BlockSpec(shape, lambda *_: (0,) * nd)


def _rms(x, g, axis=-1):
    return x * lax.rsqrt(jnp.mean(x * x, axis=axis, keepdims=True) + EPS) * g


def _dot(a, b):
    return jnp.dot(a, b, preferred_element_type=F32)


def _with_ones_row(vt):
    row = lax.broadcasted_iota(jnp.int32, vt.shape, 0)
    return jnp.where(row == V_DIM, 1.0, vt)


def _swiglu(xn, w1_ref, w3_ref, w2_ref):
    acc = None
    for c in range(D_FF // FF_CHUNK):
        sl = slice(c * FF_CHUNK, (c + 1) * FF_CHUNK)
        a = _dot(xn, w1_ref[:, sl])
        b = _dot(xn, w3_ref[:, sl])
        act = (a * jax.nn.sigmoid(a) * b).astype(BF16)
        d = _dot(act, w2_ref[sl, :])
        acc = d if acc is None else acc + d
    return acc


def _ffn_kernel(x_ref, gpre_ref, gpost_ref, w1_ref, w3_ref, w2_ref, o_ref):
    x = x_ref[...]
    xn = _rms(x, gpre_ref[...]).astype(BF16)
    f = _swiglu(xn, w1_ref, w3_ref, w2_ref)
    o_ref[...] = x + 0.5 * _rms(f, gpost_ref[...])


def _ffn(x, gpre, gpost, w1, w3, w2):
    m = x.shape[0]
    tok = pl.BlockSpec((TOK_TILE, D_MODEL), lambda i: (i, 0))
    return pl.pallas_call(
        _ffn_kernel,
        grid=(m // TOK_TILE,),
        in_specs=[tok, _const_spec((1, D_MODEL)), _const_spec((1, D_MODEL)),
                  _const_spec(w1.shape), _const_spec(w3.shape), _const_spec(w2.shape)],
        out_specs=tok,
        out_shape=jax.ShapeDtypeStruct((m, D_MODEL), F32),
        compiler_params=_params(1),
        name="ffn",
    )(x, gpre, gpost, w1, w3, w2)


def _pad_rows(x, rows):
    return jnp.concatenate([x, jnp.zeros((rows - x.shape[0], x.shape[1]), x.dtype)], axis=0)


def _rope_t(x, cos_t, sin_t):
    r2 = ROPE_DIM // 2
    x1, x2 = x[:r2], x[r2:]
    return jnp.concatenate([x1 * cos_t - x2 * sin_t, x1 * sin_t + x2 * cos_t], axis=0)


def _inproj_kernel(h_ref, g_ref, rope_ref, wgate_ref, wtr_ref, qngt_ref, kvngt_ref,
                   wqt_ref, wkt_ref, wvt_ref,
                   qat_ref, ka_ref, vat_ref, qbt_ref, kb_ref, vbt_ref, ga_ref, gb_ref):
    u = _rms(h_ref[...], g_ref[...])
    ub = u.astype(BF16)
    ut = u.T.astype(BF16)
    cos_t = rope_ref[:ROPE_DIM // 2]
    sin_t = rope_ref[ROPE_DIM // 2:]

    ga_ref[...] = jax.nn.sigmoid(_dot(ub, wgate_ref[:, :D_MODEL])).astype(BF16)
    gb_ref[...] = jax.nn.sigmoid(_dot(ub, wgate_ref[:, D_MODEL:])).astype(BF16)

    cqt = _rms(_dot(wtr_ref[_T_CQ:_T_CKV, :], ut), qngt_ref[...], axis=0).astype(BF16)
    ckvt = _rms(_dot(wtr_ref[_T_CKV:_T_KR, :], ut), kvngt_ref[...], axis=0).astype(BF16)
    rest = _dot(wtr_ref[_T_KR:_T_END, :], ut)
    off = lambda lo, hi: rest[lo - _T_KR:hi - _T_KR]
    krt = _rope_t(off(_T_KR, _T_QB), cos_t, sin_t)
    qbt = off(_T_QB, _T_VB) * LOG2E
    vbt = off(_T_VB, _T_KB)
    kbt = off(_T_KB, _T_END)
    for hb in range(HB):
        qbt_ref[hb] = _pad_rows(qbt[hb * HD:(hb + 1) * HD], LANES).astype(BF16)
    for g in range(KVH):
        kb_ref[g] = _pad_rows(kbt[g * HD:(g + 1) * HD], LANES).T.astype(BF16)
        vbt_ref[g] = _with_ones_row(_pad_rows(vbt[g * HD:(g + 1) * HD], V_ROWS)).astype(BF16)

    qt_all = _dot(wqt_ref[...], cqt)
    knt_all = _dot(wkt_ref[...], ckvt)
    vt_all = _dot(wvt_ref[...], ckvt)
    qscale = QK_DIM ** -0.5 * LOG2E
    for h in range(HA):
        qh = qt_all[h * QK_DIM:(h + 1) * QK_DIM]
        qt = jnp.concatenate([qh[:NOPE_DIM], _rope_t(qh[NOPE_DIM:], cos_t, sin_t)], axis=0) * qscale
        qat_ref[h, 0] = _pad_rows(qt, LANES).astype(BF16)
        kt = jnp.concatenate([knt_all[h * NOPE_DIM:(h + 1) * NOPE_DIM], krt], axis=0)
        ka_ref[h] = _pad_rows(kt, LANES).T.astype(BF16)
        vat_ref[h, 0] = _with_ones_row(_pad_rows(vt_all[h * V_DIM:(h + 1) * V_DIM], V_ROWS)).astype(BF16)


def _inproj(h, g, rope_t, wgate, wtr, qngt, kvngt, wqt, wkt, wvt, seq):
    m = h.shape[0]
    nt = m // TOK_TILE
    per_seq = seq // TOK_TILE
    per_group = MLA_KEY_GROUP // TOK_TILE
    tok = lambda w: pl.BlockSpec((TOK_TILE, w), lambda i: (i, 0))
    heads = lambda n: pl.BlockSpec((n, TOK_TILE, LANES), lambda i: (0, i, 0))
    heads_t = lambda n: pl.BlockSpec((n, LANES, TOK_TILE), lambda i: (0, 0, i))
    out_shape = (
        jax.ShapeDtypeStruct((HA, nt, LANES, TOK_TILE), BF16),
        jax.ShapeDtypeStruct((HA, m, LANES), BF16),
        jax.ShapeDtypeStruct((HA, m // MLA_KEY_GROUP, V_ROWS, MLA_KEY_GROUP), BF16),
        jax.ShapeDtypeStruct((HB, LANES, m), BF16),
        jax.ShapeDtypeStruct((KVH, m, LANES), BF16),
        jax.ShapeDtypeStruct((KVH, V_ROWS, m), BF16),
        jax.ShapeDtypeStruct((m, D_MODEL), BF16),
        jax.ShapeDtypeStruct((m, D_MODEL), BF16),
    )
    out_specs = (
        pl.BlockSpec((HA, 1, LANES, TOK_TILE), lambda i: (0, i, 0, 0)),
        heads(HA),
        pl.BlockSpec((HA, 1, V_ROWS, TOK_TILE), lambda i: (0, i // per_group, 0, i % per_group)),
        heads_t(HB),
        heads(KVH),
        pl.BlockSpec((KVH, V_ROWS, TOK_TILE), lambda i: (0, 0, i)),
        tok(D_MODEL),
        tok(D_MODEL),
    )
    return pl.pallas_call(
        _inproj_kernel,
        grid=(nt,),
        in_specs=[tok(D_MODEL), _const_spec((1, D_MODEL)),
                  pl.BlockSpec((ROPE_DIM, TOK_TILE), lambda i: (0, i % per_seq)),
                  _const_spec(wgate.shape), _const_spec(wtr.shape),
                  _const_spec((Q_LORA, 1)), _const_spec((KV_LORA, 1)),
                  _const_spec(wqt.shape), _const_spec(wkt.shape), _const_spec(wvt.shape)],
        out_specs=out_specs,
        out_shape=out_shape,
        compiler_params=_params(1),
        name="inproj",
    )(h, g, rope_t, wgate, wtr, qngt, kvngt, wqt, wkt, wvt)


def _mla_kernel(qt_ref, qn_ref, k_ref, vt_ref, o_ref, s_ref, m_ref, q_all, o_acc, *,
                n_groups, q_tiles, heads):
    gk, tq, ck, half = MLA_KEY_GROUP, MLA_Q_TILE, MLA_KEY_GROUP // MLA_SUB, MLA_SUB // 2
    n_items = q_tiles * heads * n_groups

    for qi in range(q_tiles):
        for j in range(heads):
            q_all[qi * heads + j] = qt_ref[j, qi]
    for j in range(2):
        q_all[heads * q_tiles + j] = qn_ref[j, 0]

    def produce(slot, n, c):
        head, group = (n // n_groups) % heads, n % n_groups
        start = pl.multiple_of(group * gk + c * ck, ck)
        s = _dot(k_ref[head, pl.ds(start, ck), :], q_all[n // n_groups])
        s_ref[slot, c * ck:(c + 1) * ck, :] = s
        return jnp.max(s, axis=0, keepdims=True)

    def produce_half(slot, n, which):
        mh = None
        for c in range(which * half, (which + 1) * half):
            mc = produce(slot, n, c)
            mh = mc if mh is None else jnp.maximum(mh, mc)
        m_ref[slot, which] = mh

    def step(slot, n, carry):
        head, group = (n // n_groups) % heads, n % n_groups
        m_old, acc = carry
        m_old = jnp.where(group == 0, -jnp.inf, m_old)
        m_new = jnp.maximum(m_old, jnp.maximum(m_ref[slot, 0], m_ref[slot, 1]))
        pv = mh = None
        for c in range(MLA_SUB):
            mc = produce(1 - slot, n + 1, c + half) if c < half else produce(slot, n + 2, c - half)
            mh = mc if c % half == 0 else jnp.maximum(mh, mc)
            rows = slice(c * ck, (c + 1) * ck)
            p = jnp.exp2(s_ref[slot, rows, :] - m_new).astype(BF16)
            d = _dot(vt_ref[head, group, :, rows], p)
            pv = d if pv is None else pv + d
            if c == half - 1:
                m_ref[1 - slot, 1] = mh
        m_ref[slot, 0] = mh
        acc = jnp.exp2(m_old - m_new) * acc + pv
        o_acc[n // (heads * n_groups), pl.ds(pl.multiple_of(head * V_DIM, V_DIM), V_DIM), :] = (
            acc[:V_DIM] / acc[V_DIM:V_DIM + 1])
        return m_new, acc

    @pl.when(pl.program_id(2) == 0)
    def _():
        produce_half(0, 0, 0)
        produce_half(0, 0, 1)
        produce_half(1, 1, 0)

    unroll = min(MLA_UNROLL, n_items // 2)

    def body(t, carry):
        for u in range(unroll):
            carry = step(u % 2, unroll * t + u, carry)
        return carry

    lax.fori_loop(0, n_items // unroll, body,
                  (jnp.full((1, tq), -jnp.inf, F32), jnp.zeros((V_ROWS, tq), F32)))
    for qi in range(q_tiles):
        o_ref[qi * tq:(qi + 1) * tq, :] = o_acc[qi].T.astype(BF16)


def _mla(qat, ka, vat, batch, seq):
    m = ka.shape[1]
    n_groups = seq // MLA_KEY_GROUP
    hs = 2 if n_groups > 1 else MLA_HEADS_SHORT
    q_tiles = min(max(1, MLA_ITEMS_PER_STEP // (hs * n_groups)), seq // MLA_Q_TILE)
    tqs = q_tiles * MLA_Q_TILE
    nq = seq // tqs
    last_tile = m // MLA_Q_TILE - 1
    return pl.pallas_call(
        functools.partial(_mla_kernel, n_groups=n_groups, q_tiles=q_tiles, heads=hs),
        grid=(batch, HA // hs, nq),
        in_specs=[
            pl.BlockSpec((hs, q_tiles, LANES, MLA_Q_TILE), lambda b, hp, i: (hp, b * nq + i, 0, 0)),
            pl.BlockSpec((2, 1, LANES, MLA_Q_TILE),
                         lambda b, hp, i: (hp * (hs // 2),
                                           jnp.minimum((b * nq + i + 1) * q_tiles, last_tile), 0, 0)),
            pl.BlockSpec((hs, seq, LANES), lambda b, hp, i: (hp, b, 0)),
            pl.BlockSpec((hs, n_groups, V_ROWS, MLA_KEY_GROUP), lambda b, hp, i: (hp, b, 0, 0)),
        ],
        out_specs=pl.BlockSpec((tqs, hs * V_DIM), lambda b, hp, i: (b * nq + i, hp)),
        out_shape=jax.ShapeDtypeStruct((m, HA * V_DIM), BF16),
        scratch_shapes=[pltpu.VMEM((2, MLA_KEY_GROUP, MLA_Q_TILE), F32),
                        pltpu.VMEM((2, 2, 1, MLA_Q_TILE), F32),
                        pltpu.VMEM((hs * q_tiles + 2, LANES, MLA_Q_TILE), BF16),
                        pltpu.VMEM((q_tiles, hs * V_DIM, MLA_Q_TILE), F32)],
        compiler_params=_params(3),
        name="mla",
    )(qat, qat, ka, vat)


def _bias_kernel(bucket_ref, relb_ref, o_ref):
    bucket = bucket_ref[...]
    key = lax.broadcasted_iota(jnp.int32, bucket.shape, 0)
    tq = WIN_Q_TILE
    for h in range(HB):
        acc = jnp.full(bucket.shape, NEG, F32)
        for b in range(REL_BUCKETS):
            acc = jnp.where(bucket == b, relb_ref[b * HB + h] * LOG2E, acc)
        g, cols = h // REP, slice((h % REP) * tq, (h % REP + 1) * tq)
        o_ref[0, g, :, cols] = jnp.where(key < WINDOW, NEG, acc)
        o_ref[1, g, :, cols] = acc
        o_ref[2, g, :, cols] = jnp.where(key >= WINDOW + tq, NEG, acc)


def _bias_table(rel_bias):
    c = jnp.arange(2 * WIN_Q_TILE, dtype=jnp.int32)[:, None]
    r = jnp.arange(WIN_Q_TILE, dtype=jnp.int32)[None, :]
    rel = c - WINDOW - r
    nb = REL_BUCKETS // 2
    max_exact = nb // 2
    ret = jnp.where(rel > 0, nb, 0)
    n = jnp.abs(rel)
    nf = jnp.maximum(n, 1).astype(F32)
    large = max_exact + (jnp.log(nf / max_exact) / math.log(REL_MAX_DIST / max_exact)
                         * (nb - max_exact)).astype(jnp.int32)
    large = jnp.minimum(large, nb - 1)
    bucket = ret + jnp.where(n < max_exact, n, large)
    bucket = jnp.where(n <= WINDOW, bucket, -1)
    return pl.pallas_call(
        _bias_kernel,
        in_specs=[pl.BlockSpec(memory_space=pltpu.VMEM), pl.BlockSpec(memory_space=pltpu.SMEM)],
        out_specs=pl.BlockSpec(memory_space=pltpu.VMEM),
        out_shape=jax.ShapeDtypeStruct((3, KVH, 2 * WIN_Q_TILE, REP * WIN_Q_TILE), F32),
        name="rel_bias_table",
    )(bucket, rel_bias.reshape(-1))


def _win_kernel(sink_ref, qt_ref, bias_lo_ref, bias_hi_ref, kp_ref, kc_ref, kn_ref,
                vtp_ref, vtc_ref, vtn_ref, o_ref, s_ref):
    tq, edge = WIN_Q_TILE, WIN_STEP - WINDOW
    chains = [(g, t) for g in range(KVH) for t in range(WIN_STEP // tq)]
    bias_refs = (bias_lo_ref, bias_hi_ref)

    def keys(t):
        return slice(t * tq, t * tq + 2 * tq)

    def scores(slot, g, t):
        k_span = jnp.concatenate([kp_ref[g][edge:], kc_ref[g], kn_ref[g][:WINDOW]], axis=0)
        qt = jnp.concatenate([qt_ref[g * REP + r][:, t * tq:(t + 1) * tq] for r in range(REP)],
                             axis=1)
        s_ref[slot] = _dot(k_span[keys(t)], qt) + bias_refs[t][0, g]

    def consume(slot, g, t):
        s = s_ref[slot]
        sink = jnp.concatenate(
            [jnp.full((1, tq), sink_ref[g * REP + r] * LOG2E, F32) for r in range(REP)], axis=1)
        m = jnp.maximum(jnp.max(s, axis=0, keepdims=True), sink)
        p = jnp.exp2(s - m).astype(BF16)
        sink_p = jnp.exp2(sink - m)
        vt_span = jnp.concatenate([vtp_ref[g][:, edge:], vtc_ref[g], vtn_ref[g][:, :WINDOW]], axis=1)
        halves = []
        for r in range(REP):
            cols = slice(r * tq, (r + 1) * tq)
            pv = _dot(vt_span[:, keys(t)], p[:, cols])
            halves.append(pv[:HD] / (pv[V_DIM:V_DIM + 1] + sink_p[:, cols]))
        o_ref[t * tq:(t + 1) * tq, g * REP * HD:(g + 1) * REP * HD] = (
            jnp.concatenate(halves, axis=0).T.astype(BF16))

    scores(0, *chains[0])
    for n, chain in enumerate(chains):
        if n + 1 < len(chains):
            scores((n + 1) % 2, *chains[n + 1])
        consume(n % 2, *chain)


def _win(sink, qbt, bias, kb, vbt, batch, seq):
    m = kb.shape[1]
    ts = WIN_STEP
    nt = seq // ts

    def neighbour(i, off):
        return jnp.clip(i + off, 0, nt - 1)

    def k_spec(off):
        return pl.BlockSpec((KVH, ts, LANES), lambda b, i: (0, b * nt + neighbour(i, off), 0))

    def vt_spec(off):
        return pl.BlockSpec((KVH, V_ROWS, ts), lambda b, i: (0, 0, b * nt + neighbour(i, off)))

    bias_block = (1, KVH, 2 * WIN_Q_TILE, REP * WIN_Q_TILE)
    return pl.pallas_call(
        _win_kernel,
        grid=(batch, nt),
        in_specs=[
            pl.BlockSpec(memory_space=pltpu.SMEM),
            pl.BlockSpec((HB, LANES, ts), lambda b, i: (0, 0, b * nt + i)),
            pl.BlockSpec(bias_block, lambda b, i: (jnp.where(i == 0, 0, 1), 0, 0, 0)),
            pl.BlockSpec(bias_block, lambda b, i: (jnp.where(i == nt - 1, 2, 1), 0, 0, 0)),
            k_spec(-1), k_spec(0), k_spec(1),
            vt_spec(-1), vt_spec(0), vt_spec(1),
        ],
        out_specs=pl.BlockSpec((ts, HB * HD), lambda b, i: (b * nt + i, 0)),
        out_shape=jax.ShapeDtypeStruct((m, HB * HD), BF16),
        scratch_shapes=[pltpu.VMEM((2, 2 * WIN_Q_TILE, REP * WIN_Q_TILE), F32)],
        compiler_params=_params(2),
        name="window",
    )(sink, qbt, bias, bias, kb, kb, kb, vbt, vbt, vbt)


def _tail_kernel(h_ref, ya_ref, yb_ref, ga_ref, gb_ref, p_ref, gains_ref,
                 pa_ref, pb_ref, wo_ref, w1_ref, w3_ref, w2_ref, wg_ref, wp_ref, o_ref):
    g_mix, g_pre2, g_post2, g_pre_e, g_post_e = (gains_ref[k:k + 1] for k in range(5))
    ma = _dot(ya_ref[...], pa_ref[...])
    mb = _dot(yb_ref[...], pb_ref[...])
    mix = (ga_ref[...].astype(F32) * ma + gb_ref[...].astype(F32) * mb).astype(BF16)
    h = h_ref[...] + _rms(_dot(mix, wo_ref[...]), g_mix)
    f = _swiglu(_rms(h, g_pre2).astype(BF16), w1_ref, w3_ref, w2_ref)
    h = h + 0.5 * _rms(f, g_post2)
    gate = jax.nn.sigmoid(_dot(_rms(h, g_pre_e).astype(BF16), wg_ref[...]))
    e = _dot(p_ref[...].astype(BF16), wp_ref[...]) * gate
    o_ref[...] = h + _rms(e, g_post_e)


def _tail(h, ya, yb, ga, gb, p, gains, pa, pb, wo, w1, w3, w2, wg, wp):
    m = h.shape[0]
    tok = lambda w: pl.BlockSpec((TOK_TILE, w), lambda i: (i, 0))
    resident = lambda a: pl.BlockSpec(a.shape, lambda i: (0,) * a.ndim, pipeline_mode=pl.Buffered(1))
    weights = (pa, pb, wo, w1, w3, w2, wg, wp)
    return pl.pallas_call(
        _tail_kernel,
        grid=(m // TOK_TILE,),
        in_specs=[tok(D_MODEL), tok(HA * V_DIM), tok(HB * HD), tok(D_MODEL), tok(D_MODEL),
                  tok(PLE_DIM), _const_spec(gains.shape)] + [resident(a) for a in weights],
        out_specs=tok(D_MODEL),
        out_shape=jax.ShapeDtypeStruct((m, D_MODEL), F32),
        compiler_params=_params(1),
        name="tail",
    )(h, ya, yb, ga, gb, p, gains, *weights)


def _prep_weights(w_in, w_uq, w_uk, w_uv):
    splits = (Q_LORA, KV_LORA, ROPE_DIM, HB * HD, KVH * HD, KVH * HD, D_MODEL, D_MODEL)
    offs = [0]
    for s in splits:
        offs.append(offs[-1] + s)
    w_cq, w_ckv, w_kr, w_qb, w_kb, w_vb, w_ga, w_gb = (
        w_in[:, offs[k]:offs[k + 1]] for k in range(len(splits)))
    wgate = jnp.concatenate([w_ga, w_gb], axis=1).astype(BF16)
    wtr = jnp.concatenate([w_cq, w_ckv, w_kr, w_qb * HD ** -0.5, w_vb, w_kb],
                          axis=1).T.astype(BF16)
    return wgate, wtr, w_uq.T.astype(BF16), w_uk.T.astype(BF16), w_uv.T.astype(BF16)


def _rope_table_t(seq):
    inv = 1.0 / (ROPE_THETA ** (jnp.arange(0, ROPE_DIM, 2, dtype=F32) / ROPE_DIM))
    ang = inv[:, None] * jnp.arange(seq, dtype=F32)[None, :]
    return jnp.concatenate([jnp.cos(ang), jnp.sin(ang)], axis=0)


def _encoder_layer(x, p, bias, sink, wts):
    batch, seq, _ = x.shape
    m = batch * seq
    x2 = x.reshape(m, D_MODEL)
    p2 = p.reshape(m, PLE_DIM)
    h = _ffn(x2, wts["ffn1_pre_g"], wts["ffn1_post_g"], wts["ffn1_w1"], wts["ffn1_w3"], wts["ffn1_w2"])
    qat, ka, vat, qbt, kb, vbt, ga, gb = _inproj(
        h, wts["mix_pre_g"], _rope_table_t(seq), wts["wgate"], wts["wtr"],
        wts["q_norm_gt"], wts["kv_norm_gt"], wts["wqt"], wts["wkt"], wts["wvt"], seq)
    ya = _mla(qat, ka, vat, batch, seq)
    yb = _win(sink, qbt, bias, kb, vbt, batch, seq)
    gains = jnp.concatenate([wts["mix_post_g"], wts["ffn2_pre_g"], wts["ffn2_post_g"],
                             wts["ple_pre_g"], wts["ple_post_g"]], axis=0)
    out = _tail(h, ya, yb, ga, gb, p2, gains, wts["w_proj_a"], wts["w_proj_b"], wts["w_out"],
                wts["ffn2_w1"], wts["ffn2_w3"], wts["ffn2_w2"], wts["w_ple_gate"], wts["w_ple_proj"])
    return out.reshape(batch, seq, D_MODEL)


def _layer_weights(layer, w):
    wgate, wtr, wqt, wkt, wvt = _prep_weights(w["w_in"][layer], w["w_uq"][layer], w["w_uk"][layer],
                                              w["w_uv"][layer])
    wts = {
        "wgate": wgate, "wtr": wtr, "wqt": wqt, "wkt": wkt, "wvt": wvt,
        "q_norm_gt": w["q_norm_g"][layer][:, None], "kv_norm_gt": w["kv_norm_g"][layer][:, None],
    }
    for name in ("ffn1_pre_g", "ffn1_post_g", "mix_pre_g", "mix_post_g",
                 "ffn2_pre_g", "ffn2_post_g", "ple_pre_g", "ple_post_g"):
        wts[name] = w[name][layer][None]
    for name in ("ffn1_w1", "ffn1_w3", "ffn1_w2", "w_proj_a", "w_proj_b", "w_out",
                 "ffn2_w1", "ffn2_w3", "ffn2_w2", "w_ple_gate", "w_ple_proj"):
        wts[name] = w[name][layer].astype(BF16)
    return wts


def kernel(x_prompt, x_sample, p_prompt, p_sample, rel_bias, ffn1_pre_g, ffn1_post_g, ffn1_w1, ffn1_w3, ffn1_w2, mix_pre_g, mix_post_g, w_in, q_norm_g, kv_norm_g, w_uq, w_uk, w_uv, sink, w_proj_a, w_proj_b, w_out, ffn2_pre_g, ffn2_post_g, ffn2_w1, ffn2_w3, ffn2_w2, ple_pre_g, ple_post_g, w_ple_gate, w_ple_proj):
    w = dict(ffn1_pre_g=ffn1_pre_g, ffn1_post_g=ffn1_post_g, ffn1_w1=ffn1_w1, ffn1_w3=ffn1_w3,
             ffn1_w2=ffn1_w2, mix_pre_g=mix_pre_g, mix_post_g=mix_post_g, w_in=w_in,
             q_norm_g=q_norm_g, kv_norm_g=kv_norm_g, w_uq=w_uq, w_uk=w_uk, w_uv=w_uv,
             w_proj_a=w_proj_a, w_proj_b=w_proj_b, w_out=w_out, ffn2_pre_g=ffn2_pre_g,
             ffn2_post_g=ffn2_post_g, ffn2_w1=ffn2_w1, ffn2_w3=ffn2_w3, ffn2_w2=ffn2_w2,
             ple_pre_g=ple_pre_g, ple_post_g=ple_post_g, w_ple_gate=w_ple_gate,
             w_ple_proj=w_ple_proj)
    bias = _bias_table(rel_bias)
    y_prompt, y_sample = x_prompt, x_sample
    for layer in range(ffn1_w1.shape[0]):
        wts = _layer_weights(layer, w)
        y_prompt = _encoder_layer(y_prompt, p_prompt[layer], bias, sink[layer], wts)
        y_sample = _encoder_layer(y_sample, p_sample[layer], bias, sink[layer], wts)
    return (y_prompt, y_sample)
```

```python
import functools
import math

import jax
import jax.numpy as jnp
from jax import lax
from jax.experimental import pallas as pl
from jax.experimental.pallas import tpu as pltpu

D_MODEL = 1024
D_FF = 2816
PLE_DIM = 256
HA = 8
Q_LORA = 384
KV_LORA = 256
NOPE_DIM = 64
ROPE_DIM = 32
V_DIM = 64
ROPE_THETA = 10000.0
HB = 8
KVH = 2
REP = HB // KVH
HD = 64
WINDOW = 128
REL_BUCKETS = 32
REL_MAX_DIST = 128
EPS = 1e-6
NEG = -1e30

LANES = 128
BF16_SUBLANES = 16
V_ROWS = -(-(V_DIM + 1) // BF16_SUBLANES) * BF16_SUBLANES
VMEM_LIMIT = 56 * 1024 * 1024

TOK_TILE = 512
FF_CHUNK = 256
MLA_Q_TILE = TOK_TILE
MLA_KEY_GROUP = 2048
MLA_SUB = 8
MLA_ITEMS_PER_STEP = 64
MLA_UNROLL = 16
MLA_HEADS_SHORT = 4
WIN_Q_TILE = 2 * WINDOW
WIN_STEP = 2 * WIN_Q_TILE
LOG2E = math.log2(math.e)

F32 = jnp.float32
BF16 = jnp.bfloat16

_T_CQ = 0
_T_CKV = _T_CQ + Q_LORA
_T_KR = _T_CKV + KV_LORA
_T_QB = _T_KR + ROPE_DIM
_T_VB = _T_QB + HB * HD
_T_KB = _T_VB + KVH * HD
_T_END = _T_KB + KVH * HD
QK_DIM = NOPE_DIM + ROPE_DIM


def _params(n_axes):
    return pltpu.CompilerParams(dimension_semantics=("arbitrary",) * n_axes,
                                vmem_limit_bytes=VMEM_LIMIT)


def _const_spec(shape):
    nd = len(shape)
    return pl.BlockSpec(shape, lambda *_: (0,) * nd)


def _rms(x, g, axis=-1):
    return x * lax.rsqrt(jnp.mean(x * x, axis=axis, keepdims=True) + EPS) * g


def _dot(a, b):
    return jnp.dot(a, b, preferred_element_type=F32)


def _with_ones_row(vt):
    row = lax.broadcasted_iota(jnp.int32, vt.shape, 0)
    return jnp.where(row == V_DIM, 1.0, vt)


def _swiglu(xn, w1_ref, w3_ref, w2_ref):
    acc = None
    for c in range(D_FF // FF_CHUNK):
        sl = slice(c * FF_CHUNK, (c + 1) * FF_CHUNK)
        a = _dot(xn, w1_ref[:, sl])
        b = _dot(xn, w3_ref[:, sl])
        act = (a * jax.nn.sigmoid(a) * b).astype(BF16)
        d = _dot(act, w2_ref[sl, :])
        acc = d if acc is None else acc + d
    return acc


def _ffn_kernel(x_ref, gpre_ref, gpost_ref, w1_ref, w3_ref, w2_ref, o_ref):
    x = x_ref[...]
    xn = _rms(x, gpre_ref[...]).astype(BF16)
    f = _swiglu(xn, w1_ref, w3_ref, w2_ref)
    o_ref[...] = x + 0.5 * _rms(f, gpost_ref[...])


def _ffn(x, gpre, gpost, w1, w3, w2):
    m = x.shape[0]
    tok = pl.BlockSpec((TOK_TILE, D_MODEL), lambda i: (i, 0))
    return pl.pallas_call(
        _ffn_kernel,
        grid=(m // TOK_TILE,),
        in_specs=[tok, _const_spec((1, D_MODEL)), _const_spec((1, D_MODEL)),
                  _const_spec(w1.shape), _const_spec(w3.shape), _const_spec(w2.shape)],
        out_specs=tok,
        out_shape=jax.ShapeDtypeStruct((m, D_MODEL), F32),
        compiler_params=_params(1),
        name="ffn",
    )(x, gpre, gpost, w1, w3, w2)


def _pad_rows(x, rows):
    return jnp.concatenate([x, jnp.zeros((rows - x.shape[0], x.shape[1]), x.dtype)], axis=0)


def _rope_t(x, cos_t, sin_t):
    r2 = ROPE_DIM // 2
    x1, x2 = x[:r2], x[r2:]
    return jnp.concatenate([x1 * cos_t - x2 * sin_t, x1 * sin_t + x2 * cos_t], axis=0)


def _inproj_kernel(h_ref, g_ref, rope_ref, wgate_ref, wtr_ref, qngt_ref, kvngt_ref,
                   wqt_ref, wkt_ref, wvt_ref,
                   qat_ref, ka_ref, vat_ref, qbt_ref, kb_ref, vbt_ref, ga_ref, gb_ref):
    u = _rms(h_ref[...], g_ref[...])
    ub = u.astype(BF16)
    ut = u.T.astype(BF16)
    cos_t = rope_ref[:ROPE_DIM // 2]
    sin_t = rope_ref[ROPE_DIM // 2:]

    ga_ref[...] = jax.nn.sigmoid(_dot(ub, wgate_ref[:, :D_MODEL])).astype(BF16)
    gb_ref[...] = jax.nn.sigmoid(_dot(ub, wgate_ref[:, D_MODEL:])).astype(BF16)

    cqt = _rms(_dot(wtr_ref[_T_CQ:_T_CKV, :], ut), qngt_ref[...], axis=0).astype(BF16)
    ckvt = _rms(_dot(wtr_ref[_T_CKV:_T_KR, :], ut), kvngt_ref[...], axis=0).astype(BF16)
    rest = _dot(wtr_ref[_T_KR:_T_END, :], ut)
    off = lambda lo, hi: rest[lo - _T_KR:hi - _T_KR]
    krt = _rope_t(off(_T_KR, _T_QB), cos_t, sin_t)
    qbt = off(_T_QB, _T_VB) * LOG2E
    vbt = off(_T_VB, _T_KB)
    kbt = off(_T_KB, _T_END)
    for hb in range(HB):
        qbt_ref[hb] = _pad_rows(qbt[hb * HD:(hb + 1) * HD], LANES).astype(BF16)
    for g in range(KVH):
        kb_ref[g] = _pad_rows(kbt[g * HD:(g + 1) * HD], LANES).T.astype(BF16)
        vbt_ref[g] = _with_ones_row(_pad_rows(vbt[g * HD:(g + 1) * HD], V_ROWS)).astype(BF16)

    qt_all = _dot(wqt_ref[...], cqt)
    knt_all = _dot(wkt_ref[...], ckvt)
    vt_all = _dot(wvt_ref[...], ckvt)
    qscale = QK_DIM ** -0.5 * LOG2E
    for h in range(HA):
        qh = qt_all[h * QK_DIM:(h + 1) * QK_DIM]
        qt = jnp.concatenate([qh[:NOPE_DIM], _rope_t(qh[NOPE_DIM:], cos_t, sin_t)], axis=0) * qscale
        qat_ref[h, 0] = _pad_rows(qt, LANES).astype(BF16)
        kt = jnp.concatenate([knt_all[h * NOPE_DIM:(h + 1) * NOPE_DIM], krt], axis=0)
        ka_ref[h] = _pad_rows(kt, LANES).T.astype(BF16)
        vat_ref[h, 0] = _with_ones_row(_pad_rows(vt_all[h * V_DIM:(h + 1) * V_DIM], V_ROWS)).astype(BF16)


def _inproj(h, g, rope_t, wgate, wtr, qngt, kvngt, wqt, wkt, wvt, seq):
    m = h.shape[0]
    nt = m // TOK_TILE
    per_seq = seq // TOK_TILE
    per_group = MLA_KEY_GROUP // TOK_TILE
    tok = lambda w: pl.BlockSpec((TOK_TILE, w), lambda i: (i, 0))
    heads = lambda n: pl.BlockSpec((n, TOK_TILE, LANES), lambda i: (0, i, 0))
    heads_t = lambda n: pl.BlockSpec((n, LANES, TOK_TILE), lambda i: (0, 0, i))
    out_shape = (
        jax.ShapeDtypeStruct((HA, nt, LANES, TOK_TILE), BF16),
        jax.ShapeDtypeStruct((HA, m, LANES), BF16),
        jax.ShapeDtypeStruct((HA, m // MLA_KEY_GROUP, V_ROWS, MLA_KEY_GROUP), BF16),
        jax.ShapeDtypeStruct((HB, LANES, m), BF16),
        jax.ShapeDtypeStruct((KVH, m, LANES), BF16),
        jax.ShapeDtypeStruct((KVH, V_ROWS, m), BF16),
        jax.ShapeDtypeStruct((m, D_MODEL), BF16),
        jax.ShapeDtypeStruct((m, D_MODEL), BF16),
    )
    out_specs = (
        pl.BlockSpec((HA, 1, LANES, TOK_TILE), lambda i: (0, i, 0, 0)),
        heads(HA),
        pl.BlockSpec((HA, 1, V_ROWS, TOK_TILE), lambda i: (0, i // per_group, 0, i % per_group)),
        heads_t(HB),
        heads(KVH),
        pl.BlockSpec((KVH, V_ROWS, TOK_TILE), lambda i: (0, 0, i)),
        tok(D_MODEL),
        tok(D_MODEL),
    )
    return pl.pallas_call(
        _inproj_kernel,
        grid=(nt,),
        in_specs=[tok(D_MODEL), _const_spec((1, D_MODEL)),
                  pl.BlockSpec((ROPE_DIM, TOK_TILE), lambda i: (0, i % per_seq)),
                  _const_spec(wgate.shape), _const_spec(wtr.shape),
                  _const_spec((Q_LORA, 1)), _const_spec((KV_LORA, 1)),
                  _const_spec(wqt.shape), _const_spec(wkt.shape), _const_spec(wvt.shape)],
        out_specs=out_specs,
        out_shape=out_shape,
        compiler_params=_params(1),
        name="inproj",
    )(h, g, rope_t, wgate, wtr, qngt, kvngt, wqt, wkt, wvt)


def _mla_kernel(qt_ref, qn_ref, k_ref, vt_ref, o_ref, s_ref, m_ref, q_all, o_acc, *,
                n_groups, q_tiles, heads):
    gk, tq, ck, half = MLA_KEY_GROUP, MLA_Q_TILE, MLA_KEY_GROUP // MLA_SUB, MLA_SUB // 2
    n_items = q_tiles * heads * n_groups

    for qi in range(q_tiles):
        for j in range(heads):
            q_all[qi * heads + j] = qt_ref[j, qi]
    for j in range(2):
        q_all[heads * q_tiles + j] = qn_ref[j, 0]

    def produce(slot, n, c):
        head, group = (n // n_groups) % heads, n % n_groups
        start = pl.multiple_of(group * gk + c * ck, ck)
        s = _dot(k_ref[head, pl.ds(start, ck), :], q_all[n // n_groups])
        s_ref[slot, c * ck:(c + 1) * ck, :] = s
        return jnp.max(s, axis=0, keepdims=True)

    def produce_half(slot, n, which):
        mh = None
        for c in range(which * half, (which + 1) * half):
            mc = produce(slot, n, c)
            mh = mc if mh is None else jnp.maximum(mh, mc)
        m_ref[slot, which] = mh

    def step(slot, n, carry):
        head, group = (n // n_groups) % heads, n % n_groups
        m_old, acc = carry
        m_old = jnp.where(group == 0, -jnp.inf, m_old)
        m_new = jnp.maximum(m_old, jnp.maximum(m_ref[slot, 0], m_ref[slot, 1]))
        pv = mh = None
        for c in range(MLA_SUB):
            mc = produce(1 - slot, n + 1, c + half) if c < half else produce(slot, n + 2, c - half)
            mh = mc if c % half == 0 else jnp.maximum(mh, mc)
            rows = slice(c * ck, (c + 1) * ck)
            p = jnp.exp2(s_ref[slot, rows, :] - m_new).astype(BF16)
            d = _dot(vt_ref[head, group, :, rows], p)
            pv = d if pv is None else pv + d
            if c == half - 1:
                m_ref[1 - slot, 1] = mh
        m_ref[slot, 0] = mh
        acc = jnp.exp2(m_old - m_new) * acc + pv
        o_acc[n // (heads * n_groups), pl.ds(pl.multiple_of(head * V_DIM, V_DIM), V_DIM), :] = (
            acc[:V_DIM] / acc[V_DIM:V_DIM + 1])
        return m_new, acc

    @pl.when(pl.program_id(2) == 0)
    def _():
        produce_half(0, 0, 0)
        produce_half(0, 0, 1)
        produce_half(1, 1, 0)

    unroll = min(MLA_UNROLL, n_items // 2)

    def body(t, carry):
        for u in range(unroll):
            carry = step(u % 2, unroll * t + u, carry)
        return carry

    lax.fori_loop(0, n_items // unroll, body,
                  (jnp.full((1, tq), -jnp.inf, F32), jnp.zeros((V_ROWS, tq), F32)))
    for qi in range(q_tiles):
        o_ref[qi * tq:(qi + 1) * tq, :] = o_acc[qi].T.astype(BF16)


def _mla(qat, ka, vat, batch, seq):
    m = ka.shape[1]
    n_groups = seq // MLA_KEY_GROUP
    hs = 2 if n_groups > 1 else MLA_HEADS_SHORT
    q_tiles = min(max(1, MLA_ITEMS_PER_STEP // (hs * n_groups)), seq // MLA_Q_TILE)
    tqs = q_tiles * MLA_Q_TILE
    nq = seq // tqs
    last_tile = m // MLA_Q_TILE - 1
    return pl.pallas_call(
        functools.partial(_mla_kernel, n_groups=n_groups, q_tiles=q_tiles, heads=hs),
        grid=(batch, HA // hs, nq),
        in_specs=[
            pl.BlockSpec((hs, q_tiles, LANES, MLA_Q_TILE), lambda b, hp, i: (hp, b * nq + i, 0, 0)),
            pl.BlockSpec((2, 1, LANES, MLA_Q_TILE),
                         lambda b, hp, i: (hp * (hs // 2),
                                           jnp.minimum((b * nq + i + 1) * q_tiles, last_tile), 0, 0)),
            pl.BlockSpec((hs, seq, LANES), lambda b, hp, i: (hp, b, 0)),
            pl.BlockSpec((hs, n_groups, V_ROWS, MLA_KEY_GROUP), lambda b, hp, i: (hp, b, 0, 0)),
        ],
        out_specs=pl.BlockSpec((tqs, hs * V_DIM), lambda b, hp, i: (b * nq + i, hp)),
        out_shape=jax.ShapeDtypeStruct((m, HA * V_DIM), BF16),
        scratch_shapes=[pltpu.VMEM((2, MLA_KEY_GROUP, MLA_Q_TILE), F32),
                        pltpu.VMEM((2, 2, 1, MLA_Q_TILE), F32),
                        pltpu.VMEM((hs * q_tiles + 2, LANES, MLA_Q_TILE), BF16),
                        pltpu.VMEM((q_tiles, hs * V_DIM, MLA_Q_TILE), F32)],
        compiler_params=_params(3),
        name="mla",
    )(qat, qat, ka, vat)


def _bias_kernel(bucket_ref, relb_ref, o_ref):
    bucket = bucket_ref[...]
    key = lax.broadcasted_iota(jnp.int32, bucket.shape, 0)
    tq = WIN_Q_TILE
    for h in range(HB):
        acc = jnp.full(bucket.shape, NEG, F32)
        for b in range(REL_BUCKETS):
            acc = jnp.where(bucket == b, relb_ref[b * HB + h] * LOG2E, acc)
        g, cols = h // REP, slice((h % REP) * tq, (h % REP + 1) * tq)
        o_ref[0, g, :, cols] = jnp.where(key < WINDOW, NEG, acc)
        o_ref[1, g, :, cols] = acc
        o_ref[2, g, :, cols] = jnp.where(key >= WINDOW + tq, NEG, acc)


def _bias_table(rel_bias):
    c = jnp.arange(2 * WIN_Q_TILE, dtype=jnp.int32)[:, None]
    r = jnp.arange(WIN_Q_TILE, dtype=jnp.int32)[None, :]
    rel = c - WINDOW - r
    nb = REL_BUCKETS // 2
    max_exact = nb // 2
    ret = jnp.where(rel > 0, nb, 0)
    n = jnp.abs(rel)
    nf = jnp.maximum(n, 1).astype(F32)
    large = max_exact + (jnp.log(nf / max_exact) / math.log(REL_MAX_DIST / max_exact)
                         * (nb - max_exact)).astype(jnp.int32)
    large = jnp.minimum(large, nb - 1)
    bucket = ret + jnp.where(n < max_exact, n, large)
    bucket = jnp.where(n <= WINDOW, bucket, -1)
    return pl.pallas_call(
        _bias_kernel,
        in_specs=[pl.BlockSpec(memory_space=pltpu.VMEM), pl.BlockSpec(memory_space=pltpu.SMEM)],
        out_specs=pl.BlockSpec(memory_space=pltpu.VMEM),
        out_shape=jax.ShapeDtypeStruct((3, KVH, 2 * WIN_Q_TILE, REP * WIN_Q_TILE), F32),
        name="rel_bias_table",
    )(bucket, rel_bias.reshape(-1))


def _win_kernel(sink_ref, qt_ref, bias_lo_ref, bias_hi_ref, kp_ref, kc_ref, kn_ref,
                vtp_ref, vtc_ref, vtn_ref, o_ref, s_ref):
    tq, edge = WIN_Q_TILE, WIN_STEP - WINDOW
    chains = [(g, t) for g in range(KVH) for t in range(WIN_STEP // tq)]
    bias_refs = (bias_lo_ref, bias_hi_ref)

    def keys(t):
        return slice(t * tq, t * tq + 2 * tq)

    def scores(slot, g, t):
        k_span = jnp.concatenate([kp_ref[g][edge:], kc_ref[g], kn_ref[g][:WINDOW]], axis=0)
        qt = jnp.concatenate([qt_ref[g * REP + r][:, t * tq:(t + 1) * tq] for r in range(REP)],
                             axis=1)
        s_ref[slot] = _dot(k_span[keys(t)], qt) + bias_refs[t][0, g]

    def consume(slot, g, t):
        s = s_ref[slot]
        sink = jnp.concatenate(
            [jnp.full((1, tq), sink_ref[g * REP + r] * LOG2E, F32) for r in range(REP)], axis=1)
        m = jnp.maximum(jnp.max(s, axis=0, keepdims=True), sink)
        p = jnp.exp2(s - m).astype(BF16)
        sink_p = jnp.exp2(sink - m)
        vt_span = jnp.concatenate([vtp_ref[g][:, edge:], vtc_ref[g], vtn_ref[g][:, :WINDOW]], axis=1)
        halves = []
        for r in range(REP):
            cols = slice(r * tq, (r + 1) * tq)
            pv = _dot(vt_span[:, keys(t)], p[:, cols])
            halves.append(pv[:HD] / (pv[V_DIM:V_DIM + 1] + sink_p[:, cols]))
        o_ref[t * tq:(t + 1) * tq, g * REP * HD:(g + 1) * REP * HD] = (
            jnp.concatenate(halves, axis=0).T.astype(BF16))

    scores(0, *chains[0])
    for n, chain in enumerate(chains):
        if n + 1 < len(chains):
            scores((n + 1) % 2, *chains[n + 1])
        consume(n % 2, *chain)


def _win(sink, qbt, bias, kb, vbt, batch, seq):
    m = kb.shape[1]
    ts = WIN_STEP
    nt = seq // ts

    def neighbour(i, off):
        return jnp.clip(i + off, 0, nt - 1)

    def k_spec(off):
        return pl.BlockSpec((KVH, ts, LANES), lambda b, i: (0, b * nt + neighbour(i, off), 0))

    def vt_spec(off):
        return pl.BlockSpec((KVH, V_ROWS, ts), lambda b, i: (0, 0, b * nt + neighbour(i, off)))

    bias_block = (1, KVH, 2 * WIN_Q_TILE, REP * WIN_Q_TILE)
    return pl.pallas_call(
        _win_kernel,
        grid=(batch, nt),
        in_specs=[
            pl.BlockSpec(memory_space=pltpu.SMEM),
            pl.BlockSpec((HB, LANES, ts), lambda b, i: (0, 0, b * nt + i)),
            pl.BlockSpec(bias_block, lambda b, i: (jnp.where(i == 0, 0, 1), 0, 0, 0)),
            pl.BlockSpec(bias_block, lambda b, i: (jnp.where(i == nt - 1, 2, 1), 0, 0, 0)),
            k_spec(-1), k_spec(0), k_spec(1),
            vt_spec(-1), vt_spec(0), vt_spec(1),
        ],
        out_specs=pl.BlockSpec((ts, HB * HD), lambda b, i: (b * nt + i, 0)),
        out_shape=jax.ShapeDtypeStruct((m, HB * HD), BF16),
        scratch_shapes=[pltpu.VMEM((2, 2 * WIN_Q_TILE, REP * WIN_Q_TILE), F32)],
        compiler_params=_params(2),
        name="window",
    )(sink, qbt, bias, bias, kb, kb, kb, vbt, vbt, vbt)


def _tail_kernel(h_ref, ya_ref, yb_ref, ga_ref, gb_ref, p_ref, gains_ref,
                 pa_ref, pb_ref, wo_ref, w1_ref, w3_ref, w2_ref, wg_ref, wp_ref, o_ref):
    g_mix, g_pre2, g_post2, g_pre_e, g_post_e = (gains_ref[k:k + 1] for k in range(5))
    ma = _dot(ya_ref[...], pa_ref[...])
    mb = _dot(yb_ref[...], pb_ref[...])
    mix = (ga_ref[...].astype(F32) * ma + gb_ref[...].astype(F32) * mb).astype(BF16)
    h = h_ref[...] + _rms(_dot(mix, wo_ref[...]), g_mix)
    f = _swiglu(_rms(h, g_pre2).astype(BF16), w1_ref, w3_ref, w2_ref)
    h = h + 0.5 * _rms(f, g_post2)
    gate = jax.nn.sigmoid(_dot(_rms(h, g_pre_e).astype(BF16), wg_ref[...]))
    e = _dot(p_ref[...].astype(BF16), wp_ref[...]) * gate
    o_ref[...] = h + _rms(e, g_post_e)


def _tail(h, ya, yb, ga, gb, p, gains, pa, pb, wo, w1, w3, w2, wg, wp):
    m = h.shape[0]
    tok = lambda w: pl.BlockSpec((TOK_TILE, w), lambda i: (i, 0))
    resident = lambda a: pl.BlockSpec(a.shape, lambda i: (0,) * a.ndim, pipeline_mode=pl.Buffered(1))
    weights = (pa, pb, wo, w1, w3, w2, wg, wp)
    return pl.pallas_call(
        _tail_kernel,
        grid=(m // TOK_TILE,),
        in_specs=[tok(D_MODEL), tok(HA * V_DIM), tok(HB * HD), tok(D_MODEL), tok(D_MODEL),
                  tok(PLE_DIM), _const_spec(gains.shape)] + [resident(a) for a in weights],
        out_specs=tok(D_MODEL),
        out_shape=jax.ShapeDtypeStruct((m, D_MODEL), F32),
        compiler_params=_params(1),
        name="tail",
    )(h, ya, yb, ga, gb, p, gains, *weights)


def _prep_weights(w_in, w_uq, w_uk, w_uv):
    splits = (Q_LORA, KV_LORA, ROPE_DIM, HB * HD, KVH * HD, KVH * HD, D_MODEL, D_MODEL)
    offs = [0]
    for s in splits:
        offs.append(offs[-1] + s)
    w_cq, w_ckv, w_kr, w_qb, w_kb, w_vb, w_ga, w_gb = (
        w_in[:, offs[k]:offs[k + 1]] for k in range(len(splits)))
    wgate = jnp.concatenate([w_ga, w_gb], axis=1).astype(BF16)
    wtr = jnp.concatenate([w_cq, w_ckv, w_kr, w_qb * HD ** -0.5, w_vb, w_kb],
                          axis=1).T.astype(BF16)
    return wgate, wtr, w_uq.T.astype(BF16), w_uk.T.astype(BF16), w_uv.T.astype(BF16)


def _rope_table_t(seq):
    inv = 1.0 / (ROPE_THETA ** (jnp.arange(0, ROPE_DIM, 2, dtype=F32) / ROPE_DIM))
    ang = inv[:, None] * jnp.arange(seq, dtype=F32)[None, :]
    return jnp.concatenate([jnp.cos(ang), jnp.sin(ang)], axis=0)


def _encoder_layer(x, p, bias, sink, wts):
    batch, seq, _ = x.shape
    m = batch * seq
    x2 = x.reshape(m, D_MODEL)
    p2 = p.reshape(m, PLE_DIM)
    h = _ffn(x2, wts["ffn1_pre_g"], wts["ffn1_post_g"], wts["ffn1_w1"], wts["ffn1_w3"], wts["ffn1_w2"])
    qat, ka, vat, qbt, kb, vbt, ga, gb = _inproj(
        h, wts["mix_pre_g"], _rope_table_t(seq), wts["wgate"], wts["wtr"],
        wts["q_norm_gt"], wts["kv_norm_gt"], wts["wqt"], wts["wkt"], wts["wvt"], seq)
    ya = _mla(qat, ka, vat, batch, seq)
    yb = _win(sink, qbt, bias, kb, vbt, batch, seq)
    gains = jnp.concatenate([wts["mix_post_g"], wts["ffn2_pre_g"], wts["ffn2_post_g"],
                             wts["ple_pre_g"], wts["ple_post_g"]], axis=0)
    out = _tail(h, ya, yb, ga, gb, p2, gains, wts["w_proj_a"], wts["w_proj_b"], wts["w_out"],
                wts["ffn2_w1"], wts["ffn2_w3"], wts["ffn2_w2"], wts["w_ple_gate"], wts["w_ple_proj"])
    return out.reshape(batch, seq, D_MODEL)


def _layer_weights(layer, w):
    wgate, wtr, wqt, wkt, wvt = _prep_weights(w["w_in"][layer], w["w_uq"][layer], w["w_uk"][layer],
                                              w["w_uv"][layer])
    wts = {
        "wgate": wgate, "wtr": wtr, "wqt": wqt, "wkt": wkt, "wvt": wvt,
        "q_norm_gt": w["q_norm_g"][layer][:, None], "kv_norm_gt": w["kv_norm_g"][layer][:, None],
    }
    for name in ("ffn1_pre_g", "ffn1_post_g", "mix_pre_g", "mix_post_g",
                 "ffn2_pre_g", "ffn2_post_g", "ple_pre_g", "ple_post_g"):
        wts[name] = w[name][layer][None]
    for name in ("ffn1_w1", "ffn1_w3", "ffn1_w2", "w_proj_a", "w_proj_b", "w_out",
                 "ffn2_w1", "ffn2_w3", "ffn2_w2", "w_ple_gate", "w_ple_proj"):
        wts[name] = w[name][layer].astype(BF16)
    return wts


def kernel(x_prompt, x_sample, p_prompt, p_sample, rel_bias, ffn1_pre_g, ffn1_post_g, ffn1_w1, ffn1_w3, ffn1_w2, mix_pre_g, mix_post_g, w_in, q_norm_g, kv_norm_g, w_uq, w_uk, w_uv, sink, w_proj_a, w_proj_b, w_out, ffn2_pre_g, ffn2_post_g, ffn2_w1, ffn2_w3, ffn2_w2, ple_pre_g, ple_post_g, w_ple_gate, w_ple_proj):
    w = dict(ffn1_pre_g=ffn1_pre_g, ffn1_post_g=ffn1_post_g, ffn1_w1=ffn1_w1, ffn1_w3=ffn1_w3,
             ffn1_w2=ffn1_w2, mix_pre_g=mix_pre_g, mix_post_g=mix_post_g, w_in=w_in,
             q_norm_g=q_norm_g, kv_norm_g=kv_norm_g, w_uq=w_uq, w_uk=w_uk, w_uv=w_uv,
             w_proj_a=w_proj_a, w_proj_b=w_proj_b, w_out=w_out, ffn2_pre_g=ffn2_pre_g,
             ffn2_post_g=ffn2_post_g, ffn2_w1=ffn2_w1, ffn2_w3=ffn2_w3, ffn2_w2=ffn2_w2,
             ple_pre_g=ple_pre_g, ple_post_g=ple_post_g, w_ple_gate=w_ple_gate,
             w_ple_proj=w_ple_proj)
    bias = _bias_table(rel_bias)
    y_prompt, y_sample = x_prompt, x_sample
    for layer in range(ffn1_w1.shape[0]):
        wts = _layer_weights(layer, w)
        y_prompt = _encoder_layer(y_prompt, p_prompt[layer], bias, sink[layer], wts)
        y_sample = _encoder_layer(y_sample, p_sample[layer], bias, sink[layer], wts)
    return (y_prompt, y_sample)
```

```python
import functools
import math

import jax
import jax.numpy as jnp
from jax import lax
from jax.experimental import pallas as pl
from jax.experimental.pallas import tpu as pltpu

D_MODEL = 1024
D_FF = 2816
PLE_DIM = 256
HA = 8
Q_LORA = 384
KV_LORA = 256
NOPE_DIM = 64
ROPE_DIM = 32
V_DIM = 64
ROPE_THETA = 10000.0
HB = 8
KVH = 2
REP = HB // KVH
HD = 64
WINDOW = 128
REL_BUCKETS = 32
REL_MAX_DIST = 128
EPS = 1e-6
NEG = -1e30

LANES = 128
BF16_SUBLANES = 16
V_ROWS = -(-(V_DIM + 1) // BF16_SUBLANES) * BF16_SUBLANES
VMEM_LIMIT = 56 * 1024 * 1024

TOK_TILE = 512
FF_CHUNK = 256
MLA_Q_TILE = TOK_TILE
MLA_KEY_GROUP = 2048
MLA_SUB = 8
MLA_ITEMS_PER_STEP = 64
MLA_UNROLL = 16
MLA_HEADS_SHORT = 8
WIN_Q_TILE = 2 * WINDOW
WIN_STEP = 4 * WIN_Q_TILE
LOG2E = math.log2(math.e)

F32 = jnp.float32
BF16 = jnp.bfloat16

_T_CQ = 0
_T_CKV = _T_CQ + Q_LORA
_T_KR = _T_CKV + KV_LORA
_T_QB = _T_KR + ROPE_DIM
_T_VB = _T_QB + HB * HD
_T_KB = _T_VB + KVH * HD
_T_END = _T_KB + KVH * HD
QK_DIM = NOPE_DIM + ROPE_DIM


def _params(n_axes):
    return pltpu.CompilerParams(dimension_semantics=("arbitrary",) * n_axes,
                                vmem_limit_bytes=VMEM_LIMIT)


def _const_spec(shape):
    nd = len(shape)
    return pl.BlockSpec(shape, lambda *_: (0,) * nd)


def _rms(x, g, axis=-1):
    return x * lax.rsqrt(jnp.mean(x * x, axis=axis, keepdims=True) + EPS) * g


def _dot(a, b):
    return jnp.dot(a, b, preferred_element_type=F32)


def _with_ones_row(vt):
    row = lax.broadcasted_iota(jnp.int32, vt.shape, 0)
    return jnp.where(row == V_DIM, 1.0, vt)


def _swiglu(xn, w1_ref, w3_ref, w2_ref):
    acc = None
    for c in range(D_FF // FF_CHUNK):
        sl = slice(c * FF_CHUNK, (c + 1) * FF_CHUNK)
        a = _dot(xn, w1_ref[:, sl])
        b = _dot(xn, w3_ref[:, sl])
        act = (a * jax.nn.sigmoid(a) * b).astype(BF16)
        d = _dot(act, w2_ref[sl, :])
        acc = d if acc is None else acc + d
    return acc


def _ffn_kernel(x_ref, gpre_ref, gpost_ref, w1_ref, w3_ref, w2_ref, o_ref):
    x = x_ref[...]
    xn = _rms(x, gpre_ref[...]).astype(BF16)
    f = _swiglu(xn, w1_ref, w3_ref, w2_ref)
    o_ref[...] = x + 0.5 * _rms(f, gpost_ref[...])


def _ffn(x, gpre, gpost, w1, w3, w2):
    m = x.shape[0]
    tok = pl.BlockSpec((TOK_TILE, D_MODEL), lambda i: (i, 0))
    return pl.pallas_call(
        _ffn_kernel,
        grid=(m // TOK_TILE,),
        in_specs=[tok, _const_spec((1, D_MODEL)), _const_spec((1, D_MODEL)),
                  _const_spec(w1.shape), _const_spec(w3.shape), _const_spec(w2.shape)],
        out_specs=tok,
        out_shape=jax.ShapeDtypeStruct((m, D_MODEL), F32),
        compiler_params=_params(1),
        name="ffn",
    )(x, gpre, gpost, w1, w3, w2)


def _pad_rows(x, rows):
    return jnp.concatenate([x, jnp.zeros((rows - x.shape[0], x.shape[1]), x.dtype)], axis=0)


def _rope_t(x, cos_t, sin_t):
    r2 = ROPE_DIM // 2
    x1, x2 = x[:r2], x[r2:]
    return jnp.concatenate([x1 * cos_t - x2 * sin_t, x1 * sin_t + x2 * cos_t], axis=0)


def _inproj_kernel(h_ref, g_ref, rope_ref, wgate_ref, wtr_ref, qngt_ref, kvngt_ref,
                   wqt_ref, wkt_ref, wvt_ref,
                   qat_ref, ka_ref, vat_ref, qbt_ref, kb_ref, vbt_ref, ga_ref, gb_ref):
    u = _rms(h_ref[...], g_ref[...])
    ub = u.astype(BF16)
    ut = u.T.astype(BF16)
    cos_t = rope_ref[:ROPE_DIM // 2]
    sin_t = rope_ref[ROPE_DIM // 2:]

    ga_ref[...] = jax.nn.sigmoid(_dot(ub, wgate_ref[:, :D_MODEL])).astype(BF16)
    gb_ref[...] = jax.nn.sigmoid(_dot(ub, wgate_ref[:, D_MODEL:])).astype(BF16)

    cqt = _rms(_dot(wtr_ref[_T_CQ:_T_CKV, :], ut), qngt_ref[...], axis=0).astype(BF16)
    ckvt = _rms(_dot(wtr_ref[_T_CKV:_T_KR, :], ut), kvngt_ref[...], axis=0).astype(BF16)
    rest = _dot(wtr_ref[_T_KR:_T_END, :], ut)
    off = lambda lo, hi: rest[lo - _T_KR:hi - _T_KR]
    krt = _rope_t(off(_T_KR, _T_QB), cos_t, sin_t)
    qbt = off(_T_QB, _T_VB) * LOG2E
    vbt = off(_T_VB, _T_KB)
    kbt = off(_T_KB, _T_END)
    for hb in range(HB):
        qbt_ref[hb] = _pad_rows(qbt[hb * HD:(hb + 1) * HD], LANES).astype(BF16)
    for g in range(KVH):
        kb_ref[g] = _pad_rows(kbt[g * HD:(g + 1) * HD], LANES).T.astype(BF16)
        vbt_ref[g] = _with_ones_row(_pad_rows(vbt[g * HD:(g + 1) * HD], V_ROWS)).astype(BF16)

    qt_all = _dot(wqt_ref[...], cqt)
    knt_all = _dot(wkt_ref[...], ckvt)
    vt_all = _dot(wvt_ref[...], ckvt)
    qscale = QK_DIM ** -0.5 * LOG2E
    for h in range(HA):
        qh = qt_all[h * QK_DIM:(h + 1) * QK_DIM]
        qt = jnp.concatenate([qh[:NOPE_DIM], _rope_t(qh[NOPE_DIM:], cos_t, sin_t)], axis=0) * qscale
        qat_ref[h, 0] = _pad_rows(qt, LANES).astype(BF16)
        kt = jnp.concatenate([knt_all[h * NOPE_DIM:(h + 1) * NOPE_DIM], krt], axis=0)
        ka_ref[h] = _pad_rows(kt, LANES).T.astype(BF16)
        vat_ref[h, 0] = _with_ones_row(_pad_rows(vt_all[h * V_DIM:(h + 1) * V_DIM], V_ROWS)).astype(BF16)


def _inproj(h, g, rope_t, wgate, wtr, qngt, kvngt, wqt, wkt, wvt, seq):
    m = h.shape[0]
    nt = m // TOK_TILE
    per_seq = seq // TOK_TILE
    per_group = MLA_KEY_GROUP // TOK_TILE
    tok = lambda w: pl.BlockSpec((TOK_TILE, w), lambda i: (i, 0))
    heads = lambda n: pl.BlockSpec((n, TOK_TILE, LANES), lambda i: (0, i, 0))
    heads_t = lambda n: pl.BlockSpec((n, LANES, TOK_TILE), lambda i: (0, 0, i))
    out_shape = (
        jax.ShapeDtypeStruct((HA, nt, LANES, TOK_TILE), BF16),
        jax.ShapeDtypeStruct((HA, m, LANES), BF16),
        jax.ShapeDtypeStruct((HA, m // MLA_KEY_GROUP, V_ROWS, MLA_KEY_GROUP), BF16),
        jax.ShapeDtypeStruct((HB, LANES, m), BF16),
        jax.ShapeDtypeStruct((KVH, m, LANES), BF16),
        jax.ShapeDtypeStruct((KVH, V_ROWS, m), BF16),
        jax.ShapeDtypeStruct((m, D_MODEL), BF16),
        jax.ShapeDtypeStruct((m, D_MODEL), BF16),
    )
    out_specs = (
        pl.BlockSpec((HA, 1, LANES, TOK_TILE), lambda i: (0, i, 0, 0)),
        heads(HA),
        pl.BlockSpec((HA, 1, V_ROWS, TOK_TILE), lambda i: (0, i // per_group, 0, i % per_group)),
        heads_t(HB),
        heads(KVH),
        pl.BlockSpec((KVH, V_ROWS, TOK_TILE), lambda i: (0, 0, i)),
        tok(D_MODEL),
        tok(D_MODEL),
    )
    return pl.pallas_call(
        _inproj_kernel,
        grid=(nt,),
        in_specs=[tok(D_MODEL), _const_spec((1, D_MODEL)),
                  pl.BlockSpec((ROPE_DIM, TOK_TILE), lambda i: (0, i % per_seq)),
                  _const_spec(wgate.shape), _const_spec(wtr.shape),
                  _const_spec((Q_LORA, 1)), _const_spec((KV_LORA, 1)),
                  _const_spec(wqt.shape), _const_spec(wkt.shape), _const_spec(wvt.shape)],
        out_specs=out_specs,
        out_shape=out_shape,
        compiler_params=_params(1),
        name="inproj",
    )(h, g, rope_t, wgate, wtr, qngt, kvngt, wqt, wkt, wvt)


def _mla_kernel(qt_ref, qn_ref, k_ref, vt_ref, o_ref, s_ref, m_ref, q_all, o_acc, *,
                n_groups, q_tiles, heads):
    gk, tq, ck, half = MLA_KEY_GROUP, MLA_Q_TILE, MLA_KEY_GROUP // MLA_SUB, MLA_SUB // 2
    n_items = q_tiles * heads * n_groups

    for qi in range(q_tiles):
        for j in range(heads):
            q_all[qi * heads + j] = qt_ref[j, qi]
    for j in range(2):
        q_all[heads * q_tiles + j] = qn_ref[j, 0]

    def produce(slot, n, c):
        head, group = (n // n_groups) % heads, n % n_groups
        start = pl.multiple_of(group * gk + c * ck, ck)
        s = _dot(k_ref[head, pl.ds(start, ck), :], q_all[n // n_groups])
        s_ref[slot, c * ck:(c + 1) * ck, :] = s
        return jnp.max(s, axis=0, keepdims=True)

    def produce_half(slot, n, which):
        mh = None
        for c in range(which * half, (which + 1) * half):
            mc = produce(slot, n, c)
            mh = mc if mh is None else jnp.maximum(mh, mc)
        m_ref[slot, which] = mh

    def step(slot, n, carry):
        head, group = (n // n_groups) % heads, n % n_groups
        m_old, acc = carry
        m_old = jnp.where(group == 0, -jnp.inf, m_old)
        m_new = jnp.maximum(m_old, jnp.maximum(m_ref[slot, 0], m_ref[slot, 1]))
        pv = mh = None
        for c in range(MLA_SUB):
            mc = produce(1 - slot, n + 1, c + half) if c < half else produce(slot, n + 2, c - half)
            mh = mc if c % half == 0 else jnp.maximum(mh, mc)
            rows = slice(c * ck, (c + 1) * ck)
            p = jnp.exp2(s_ref[slot, rows, :] - m_new).astype(BF16)
            d = _dot(vt_ref[head, group, :, rows], p)
            pv = d if pv is None else pv + d
            if c == half - 1:
                m_ref[1 - slot, 1] = mh
        m_ref[slot, 0] = mh
        acc = jnp.exp2(m_old - m_new) * acc + pv
        o_acc[n // (heads * n_groups), pl.ds(pl.multiple_of(head * V_DIM, V_DIM), V_DIM), :] = (
            acc[:V_DIM] / acc[V_DIM:V_DIM + 1])
        return m_new, acc

    @pl.when(pl.program_id(2) == 0)
    def _():
        produce_half(0, 0, 0)
        produce_half(0, 0, 1)
        produce_half(1, 1, 0)

    unroll = min(MLA_UNROLL, n_items // 2)

    def body(t, carry):
        for u in range(unroll):
            carry = step(u % 2, unroll * t + u, carry)
        return carry

    lax.fori_loop(0, n_items // unroll, body,
                  (jnp.full((1, tq), -jnp.inf, F32), jnp.zeros((V_ROWS, tq), F32)))
    for qi in range(q_tiles):
        o_ref[qi * tq:(qi + 1) * tq, :] = o_acc[qi].T.astype(BF16)


def _mla(qat, ka, vat, batch, seq):
    m = ka.shape[1]
    n_groups = seq // MLA_KEY_GROUP
    hs = 2 if n_groups > 1 else MLA_HEADS_SHORT
    q_tiles = min(max(1, MLA_ITEMS_PER_STEP // (hs * n_groups)), seq // MLA_Q_TILE)
    tqs = q_tiles * MLA_Q_TILE
    nq = seq // tqs
    last_tile = m // MLA_Q_TILE - 1
    return pl.pallas_call(
        functools.partial(_mla_kernel, n_groups=n_groups, q_tiles=q_tiles, heads=hs),
        grid=(batch, HA // hs, nq),
        in_specs=[
            pl.BlockSpec((hs, q_tiles, LANES, MLA_Q_TILE), lambda b, hp, i: (hp, b * nq + i, 0, 0)),
            pl.BlockSpec((2, 1, LANES, MLA_Q_TILE),
                         lambda b, hp, i: (hp * (hs // 2),
                                           jnp.minimum((b * nq + i + 1) * q_tiles, last_tile), 0, 0)),
            pl.BlockSpec((hs, seq, LANES), lambda b, hp, i: (hp, b, 0)),
            pl.BlockSpec((hs, n_groups, V_ROWS, MLA_KEY_GROUP), lambda b, hp, i: (hp, b, 0, 0)),
        ],
        out_specs=pl.BlockSpec((tqs, hs * V_DIM), lambda b, hp, i: (b * nq + i, hp)),
        out_shape=jax.ShapeDtypeStruct((m, HA * V_DIM), BF16),
        scratch_shapes=[pltpu.VMEM((2, MLA_KEY_GROUP, MLA_Q_TILE), F32),
                        pltpu.VMEM((2, 2, 1, MLA_Q_TILE), F32),
                        pltpu.VMEM((hs * q_tiles + 2, LANES, MLA_Q_TILE), BF16),
                        pltpu.VMEM((q_tiles, hs * V_DIM, MLA_Q_TILE), F32)],
        compiler_params=_params(3),
        name="mla",
    )(qat, qat, ka, vat)


def _bias_kernel(bucket_ref, relb_ref, o_ref):
    bucket = bucket_ref[...]
    key = lax.broadcasted_iota(jnp.int32, bucket.shape, 0)
    tq = WIN_Q_TILE
    for h in range(HB):
        acc = jnp.full(bucket.shape, NEG, F32)
        for b in range(REL_BUCKETS):
            acc = jnp.where(bucket == b, relb_ref[b * HB + h] * LOG2E, acc)
        g, cols = h // REP, slice((h % REP) * tq, (h % REP + 1) * tq)
        o_ref[0, g, :, cols] = jnp.where(key < WINDOW, NEG, acc)
        o_ref[1, g, :, cols] = acc
        o_ref[2, g, :, cols] = jnp.where(key >= WINDOW + tq, NEG, acc)


def _bias_table(rel_bias):
    c = jnp.arange(2 * WIN_Q_TILE, dtype=jnp.int32)[:, None]
    r = jnp.arange(WIN_Q_TILE, dtype=jnp.int32)[None, :]
    rel = c - WINDOW - r
    nb = REL_BUCKETS // 2
    max_exact = nb // 2
    ret = jnp.where(rel > 0, nb, 0)
    n = jnp.abs(rel)
    nf = jnp.maximum(n, 1).astype(F32)
    large = max_exact + (jnp.log(nf / max_exact) / math.log(REL_MAX_DIST / max_exact)
                         * (nb - max_exact)).astype(jnp.int32)
    large = jnp.minimum(large, nb - 1)
    bucket = ret + jnp.where(n < max_exact, n, large)
    bucket = jnp.where(n <= WINDOW, bucket, -1)
    return pl.pallas_call(
        _bias_kernel,
        in_specs=[pl.BlockSpec(memory_space=pltpu.VMEM), pl.BlockSpec(memory_space=pltpu.SMEM)],
        out_specs=pl.BlockSpec(memory_space=pltpu.VMEM),
        out_shape=jax.ShapeDtypeStruct((3, KVH, 2 * WIN_Q_TILE, REP * WIN_Q_TILE), F32),
        name="rel_bias_table",
    )(bucket, rel_bias.reshape(-1))


def _win_kernel(sink_ref, qt_ref, bias_lo_ref, bias_mid_ref, bias_hi_ref, kp_ref, kc_ref, kn_ref,
                vtp_ref, vtc_ref, vtn_ref, o_ref, s_ref):
    tq, edge, tiles = WIN_Q_TILE, WIN_STEP - WINDOW, WIN_STEP // WIN_Q_TILE
    chains = [(g, t) for g in range(KVH) for t in range(tiles)]
    bias_refs = (bias_lo_ref,) + (bias_mid_ref,) * (tiles - 2) + (bias_hi_ref,)

    def keys(t):
        return slice(t * tq, t * tq + 2 * tq)

    def scores(slot, g, t):
        k_span = jnp.concatenate([kp_ref[g][edge:], kc_ref[g], kn_ref[g][:WINDOW]], axis=0)
        qt = jnp.concatenate([qt_ref[g * REP + r][:, t * tq:(t + 1) * tq] for r in range(REP)],
                             axis=1)
        s_ref[slot] = _dot(k_span[keys(t)], qt) + bias_refs[t][0, g]

    def consume(slot, g, t):
        s = s_ref[slot]
        sink = jnp.concatenate(
            [jnp.full((1, tq), sink_ref[g * REP + r] * LOG2E, F32) for r in range(REP)], axis=1)
        m = jnp.maximum(jnp.max(s, axis=0, keepdims=True), sink)
        p = jnp.exp2(s - m).astype(BF16)
        sink_p = jnp.exp2(sink - m)
        vt_span = jnp.concatenate([vtp_ref[g][:, edge:], vtc_ref[g], vtn_ref[g][:, :WINDOW]], axis=1)
        halves = []
        for r in range(REP):
            cols = slice(r * tq, (r + 1) * tq)
            pv = _dot(vt_span[:, keys(t)], p[:, cols])
            halves.append(pv[:HD] / (pv[V_DIM:V_DIM + 1] + sink_p[:, cols]))
        o_ref[t * tq:(t + 1) * tq, g * REP * HD:(g + 1) * REP * HD] = (
            jnp.concatenate(halves, axis=0).T.astype(BF16))

    scores(0, *chains[0])
    for n, chain in enumerate(chains):
        if n + 1 < len(chains):
            scores((n + 1) % 2, *chains[n + 1])
        consume(n % 2, *chain)


def _win(sink, qbt, bias, kb, vbt, batch, seq):
    m = kb.shape[1]
    ts = WIN_STEP
    nt = seq // ts

    def neighbour(i, off):
        return jnp.clip(i + off, 0, nt - 1)

    def k_spec(off):
        return pl.BlockSpec((KVH, ts, LANES), lambda b, i: (0, b * nt + neighbour(i, off), 0))

    def vt_spec(off):
        return pl.BlockSpec((KVH, V_ROWS, ts), lambda b, i: (0, 0, b * nt + neighbour(i, off)))

    bias_block = (1, KVH, 2 * WIN_Q_TILE, REP * WIN_Q_TILE)
    return pl.pallas_call(
        _win_kernel,
        grid=(batch, nt),
        in_specs=[
            pl.BlockSpec(memory_space=pltpu.SMEM),
            pl.BlockSpec((HB, LANES, ts), lambda b, i: (0, 0, b * nt + i)),
            pl.BlockSpec(bias_block, lambda b, i: (jnp.where(i == 0, 0, 1), 0, 0, 0)),
            pl.BlockSpec(bias_block, lambda b, i: (1, 0, 0, 0)),
            pl.BlockSpec(bias_block, lambda b, i: (jnp.where(i == nt - 1, 2, 1), 0, 0, 0)),
            k_spec(-1), k_spec(0), k_spec(1),
            vt_spec(-1), vt_spec(0), vt_spec(1),
        ],
        out_specs=pl.BlockSpec((ts, HB * HD), lambda b, i: (b * nt + i, 0)),
        out_shape=jax.ShapeDtypeStruct((m, HB * HD), BF16),
        scratch_shapes=[pltpu.VMEM((2, 2 * WIN_Q_TILE, REP * WIN_Q_TILE), F32)],
        compiler_params=_params(2),
        name="window",
    )(sink, qbt, bias, bias, bias, kb, kb, kb, vbt, vbt, vbt)


def _tail_kernel(h_ref, ya_ref, yb_ref, ga_ref, gb_ref, p_ref, gains_ref,
                 pa_ref, pb_ref, wo_ref, w1_ref, w3_ref, w2_ref, wg_ref, wp_ref, o_ref):
    g_mix, g_pre2, g_post2, g_pre_e, g_post_e = (gains_ref[k:k + 1] for k in range(5))
    ma = _dot(ya_ref[...], pa_ref[...])
    mb = _dot(yb_ref[...], pb_ref[...])
    mix = (ga_ref[...].astype(F32) * ma + gb_ref[...].astype(F32) * mb).astype(BF16)
    h = h_ref[...] + _rms(_dot(mix, wo_ref[...]), g_mix)
    f = _swiglu(_rms(h, g_pre2).astype(BF16), w1_ref, w3_ref, w2_ref)
    h = h + 0.5 * _rms(f, g_post2)
    gate = jax.nn.sigmoid(_dot(_rms(h, g_pre_e).astype(BF16), wg_ref[...]))
    e = _dot(p_ref[...].astype(BF16), wp_ref[...]) * gate
    o_ref[...] = h + _rms(e, g_post_e)


def _tail(h, ya, yb, ga, gb, p, gains, pa, pb, wo, w1, w3, w2, wg, wp):
    m = h.shape[0]
    tok = lambda w: pl.BlockSpec((TOK_TILE, w), lambda i: (i, 0))
    resident = lambda a: pl.BlockSpec(a.shape, lambda i: (0,) * a.ndim, pipeline_mode=pl.Buffered(1))
    weights = (pa, pb, wo, w1, w3, w2, wg, wp)
    return pl.pallas_call(
        _tail_kernel,
        grid=(m // TOK_TILE,),
        in_specs=[tok(D_MODEL), tok(HA * V_DIM), tok(HB * HD), tok(D_MODEL), tok(D_MODEL),
                  tok(PLE_DIM), _const_spec(gains.shape)] + [resident(a) for a in weights],
        out_specs=tok(D_MODEL),
        out_shape=jax.ShapeDtypeStruct((m, D_MODEL), F32),
        compiler_params=_params(1),
        name="tail",
    )(h, ya, yb, ga, gb, p, gains, *weights)


def _prep_weights(w_in, w_uq, w_uk, w_uv):
    splits = (Q_LORA, KV_LORA, ROPE_DIM, HB * HD, KVH * HD, KVH * HD, D_MODEL, D_MODEL)
    offs = [0]
    for s in splits:
        offs.append(offs[-1] + s)
    w_cq, w_ckv, w_kr, w_qb, w_kb, w_vb, w_ga, w_gb = (
        w_in[:, offs[k]:offs[k + 1]] for k in range(len(splits)))
    wgate = jnp.concatenate([w_ga, w_gb], axis=1).astype(BF16)
    wtr = jnp.concatenate([w_cq, w_ckv, w_kr, w_qb * HD ** -0.5, w_vb, w_kb],
                          axis=1).T.astype(BF16)
    return wgate, wtr, w_uq.T.astype(BF16), w_uk.T.astype(BF16), w_uv.T.astype(BF16)


def _rope_table_t(seq):
    inv = 1.0 / (ROPE_THETA ** (jnp.arange(0, ROPE_DIM, 2, dtype=F32) / ROPE_DIM))
    ang = inv[:, None] * jnp.arange(seq, dtype=F32)[None, :]
    return jnp.concatenate([jnp.cos(ang), jnp.sin(ang)], axis=0)


def _encoder_layer(x, p, bias, sink, wts):
    batch, seq, _ = x.shape
    m = batch * seq
    x2 = x.reshape(m, D_MODEL)
    p2 = p.reshape(m, PLE_DIM)
    h = _ffn(x2, wts["ffn1_pre_g"], wts["ffn1_post_g"], wts["ffn1_w1"], wts["ffn1_w3"], wts["ffn1_w2"])
    qat, ka, vat, qbt, kb, vbt, ga, gb = _inproj(
        h, wts["mix_pre_g"], _rope_table_t(seq), wts["wgate"], wts["wtr"],
        wts["q_norm_gt"], wts["kv_norm_gt"], wts["wqt"], wts["wkt"], wts["wvt"], seq)
    ya = _mla(qat, ka, vat, batch, seq)
    yb = _win(sink, qbt, bias, kb, vbt, batch, seq)
    gains = jnp.concatenate([wts["mix_post_g"], wts["ffn2_pre_g"], wts["ffn2_post_g"],
                             wts["ple_pre_g"], wts["ple_post_g"]], axis=0)
    out = _tail(h, ya, yb, ga, gb, p2, gains, wts["w_proj_a"], wts["w_proj_b"], wts["w_out"],
                wts["ffn2_w1"], wts["ffn2_w3"], wts["ffn2_w2"], wts["w_ple_gate"], wts["w_ple_proj"])
    return out.reshape(batch, seq, D_MODEL)


def _layer_weights(layer, w):
    wgate, wtr, wqt, wkt, wvt = _prep_weights(w["w_in"][layer], w["w_uq"][layer], w["w_uk"][layer],
                                              w["w_uv"][layer])
    wts = {
        "wgate": wgate, "wtr": wtr, "wqt": wqt, "wkt": wkt, "wvt": wvt,
        "q_norm_gt": w["q_norm_g"][layer][:, None], "kv_norm_gt": w["kv_norm_g"][layer][:, None],
    }
    for name in ("ffn1_pre_g", "ffn1_post_g", "mix_pre_g", "mix_post_g",
                 "ffn2_pre_g", "ffn2_post_g", "ple_pre_g", "ple_post_g"):
        wts[name] = w[name][layer][None]
    for name in ("ffn1_w1", "ffn1_w3", "ffn1_w2", "w_proj_a", "w_proj_b", "w_out",
                 "ffn2_w1", "ffn2_w3", "ffn2_w2", "w_ple_gate", "w_ple_proj"):
        wts[name] = w[name][layer].astype(BF16)
    return wts


def kernel(x_prompt, x_sample, p_prompt, p_sample, rel_bias, ffn1_pre_g, ffn1_post_g, ffn1_w1, ffn1_w3, ffn1_w2, mix_pre_g, mix_post_g, w_in, q_norm_g, kv_norm_g, w_uq, w_uk, w_uv, sink, w_proj_a, w_proj_b, w_out, ffn2_pre_g, ffn2_post_g, ffn2_w1, ffn2_w3, ffn2_w2, ple_pre_g, ple_post_g, w_ple_gate, w_ple_proj):
    w = dict(ffn1_pre_g=ffn1_pre_g, ffn1_post_g=ffn1_post_g, ffn1_w1=ffn1_w1, ffn1_w3=ffn1_w3,
             ffn1_w2=ffn1_w2, mix_pre_g=mix_pre_g, mix_post_g=mix_post_g, w_in=w_in,
             q_norm_g=q_norm_g, kv_norm_g=kv_norm_g, w_uq=w_uq, w_uk=w_uk, w_uv=w_uv,
             w_proj_a=w_proj_a, w_proj_b=w_proj_b, w_out=w_out, ffn2_pre_g=ffn2_pre_g,
             ffn2_post_g=ffn2_post_g, ffn2_w1=ffn2_w1, ffn2_w3=ffn2_w3, ffn2_w2=ffn2_w2,
             ple_pre_g=ple_pre_g, ple_post_g=ple_post_g, w_ple_gate=w_ple_gate,
             w_ple_proj=w_ple_proj)
    bias = _bias_table(rel_bias)
    y_prompt, y_sample = x_prompt, x_sample
    for layer in range(ffn1_w1.shape[0]):
        wts = _layer_weights(layer, w)
        y_prompt = _encoder_layer(y_prompt, p_prompt[layer], bias, sink[layer], wts)
        y_sample = _encoder_layer(y_sample, p_sample[layer], bias, sink[layer], wts)
    return (y_prompt, y_sample)
```

```python
import functools
import math

import jax
import jax.numpy as jnp
from jax import lax
from jax.experimental import pallas as pl
from jax.experimental.pallas import tpu as pltpu

D_MODEL = 1024
D_FF = 2816
PLE_DIM = 256
HA = 8
Q_LORA = 384
KV_LORA = 256
NOPE_DIM = 64
ROPE_DIM = 32
V_DIM = 64
ROPE_THETA = 10000.0
HB = 8
KVH = 2
REP = HB // KVH
HD = 64
WINDOW = 128
REL_BUCKETS = 32
REL_MAX_DIST = 128
EPS = 1e-6
NEG = -1e30

LANES = 128
BF16_SUBLANES = 16
V_ROWS = -(-(V_DIM + 1) // BF16_SUBLANES) * BF16_SUBLANES
VMEM_LIMIT = 56 * 1024 * 1024

TOK_TILE = 512
FF_CHUNK = 256
MLA_Q_TILE = TOK_TILE
MLA_KEY_GROUP = 2048
MLA_SUB = 8
MLA_ITEMS_PER_STEP = 64
MLA_UNROLL = 16
MLA_HEADS_SHORT = 8
WIN_Q_TILE = 2 * WINDOW
WIN_STEP = 4 * WIN_Q_TILE
LOG2E = math.log2(math.e)

F32 = jnp.float32
BF16 = jnp.bfloat16

_T_CQ = 0
_T_CKV = _T_CQ + Q_LORA
_T_KR = _T_CKV + KV_LORA
_T_QB = _T_KR + ROPE_DIM
_T_VB = _T_QB + HB * HD
_T_KB = _T_VB + KVH * HD
_T_END = _T_KB + KVH * HD
QK_DIM = NOPE_DIM + ROPE_DIM


def _params(n_axes):
    return pltpu.CompilerParams(dimension_semantics=("arbitrary",) * n_axes,
                                vmem_limit_bytes=VMEM_LIMIT)


def _const_spec(shape):
    nd = len(shape)
    return pl.BlockSpec(shape, lambda *_: (0,) * nd)


def _rms(x, g, axis=-1):
    return x * lax.rsqrt(jnp.mean(x * x, axis=axis, keepdims=True) + EPS) * g


def _dot(a, b):
    return jnp.dot(a, b, preferred_element_type=F32)


def _with_ones_row(vt):
    row = lax.broadcasted_iota(jnp.int32, vt.shape, 0)
    return jnp.where(row == V_DIM, 1.0, vt)


def _swiglu(xn, w1_ref, w3_ref, w2_ref):
    acc = None
    for c in range(D_FF // FF_CHUNK):
        sl = slice(c * FF_CHUNK, (c + 1) * FF_CHUNK)
        a = _dot(xn, w1_ref[:, sl])
        b = _dot(xn, w3_ref[:, sl])
        act = (a * jax.nn.sigmoid(a) * b).astype(BF16)
        d = _dot(act, w2_ref[sl, :])
        acc = d if acc is None else acc + d
    return acc


def _ffn_kernel(x_ref, gpre_ref, gpost_ref, w1_ref, w3_ref, w2_ref, o_ref):
    x = x_ref[...]
    xn = _rms(x, gpre_ref[...]).astype(BF16)
    f = _swiglu(xn, w1_ref, w3_ref, w2_ref)
    o_ref[...] = x + 0.5 * _rms(f, gpost_ref[...])


def _ffn(x, gpre, gpost, w1, w3, w2):
    m = x.shape[0]
    tok = pl.BlockSpec((TOK_TILE, D_MODEL), lambda i: (i, 0))
    return pl.pallas_call(
        _ffn_kernel,
        grid=(m // TOK_TILE,),
        in_specs=[tok, _const_spec((1, D_MODEL)), _const_spec((1, D_MODEL)),
                  _const_spec(w1.shape), _const_spec(w3.shape), _const_spec(w2.shape)],
        out_specs=tok,
        out_shape=jax.ShapeDtypeStruct((m, D_MODEL), F32),
        compiler_params=_params(1),
        name="ffn",
    )(x, gpre, gpost, w1, w3, w2)


def _pad_rows(x, rows):
    return jnp.concatenate([x, jnp.zeros((rows - x.shape[0], x.shape[1]), x.dtype)], axis=0)


def _rope_t(x, cos_t, sin_t):
    r2 = ROPE_DIM // 2
    x1, x2 = x[:r2], x[r2:]
    return jnp.concatenate([x1 * cos_t - x2 * sin_t, x1 * sin_t + x2 * cos_t], axis=0)


def _inproj_kernel(h_ref, g_ref, rope_ref, wgate_ref, wtr_ref, qngt_ref, kvngt_ref,
                   wqt_ref, wkt_ref, wvt_ref,
                   qat_ref, ka_ref, vat_ref, qbt_ref, kb_ref, vbt_ref, ga_ref, gb_ref):
    u = _rms(h_ref[...], g_ref[...])
    ub = u.astype(BF16)
    ut = u.T.astype(BF16)
    cos_t = rope_ref[:ROPE_DIM // 2]
    sin_t = rope_ref[ROPE_DIM // 2:]

    ga_ref[...] = jax.nn.sigmoid(_dot(ub, wgate_ref[:, :D_MODEL])).astype(BF16)
    gb_ref[...] = jax.nn.sigmoid(_dot(ub, wgate_ref[:, D_MODEL:])).astype(BF16)

    cqt = _rms(_dot(wtr_ref[_T_CQ:_T_CKV, :], ut), qngt_ref[...], axis=0).astype(BF16)
    ckvt = _rms(_dot(wtr_ref[_T_CKV:_T_KR, :], ut), kvngt_ref[...], axis=0).astype(BF16)
    rest = _dot(wtr_ref[_T_KR:_T_END, :], ut)
    off = lambda lo, hi: rest[lo - _T_KR:hi - _T_KR]
    krt = _rope_t(off(_T_KR, _T_QB), cos_t, sin_t)
    qbt = off(_T_QB, _T_VB) * LOG2E
    vbt = off(_T_VB, _T_KB)
    kbt = off(_T_KB, _T_END)
    for hb in range(HB):
        qbt_ref[hb] = _pad_rows(qbt[hb * HD:(hb + 1) * HD], LANES).astype(BF16)
    for g in range(KVH):
        kb_ref[g] = _pad_rows(kbt[g * HD:(g + 1) * HD], LANES).T.astype(BF16)
        vbt_ref[g] = _with_ones_row(_pad_rows(vbt[g * HD:(g + 1) * HD], V_ROWS)).astype(BF16)

    qt_all = _dot(wqt_ref[...], cqt)
    knt_all = _dot(wkt_ref[...], ckvt)
    vt_all = _dot(wvt_ref[...], ckvt)
    qscale = QK_DIM ** -0.5 * LOG2E
    for h in range(HA):
        qh = qt_all[h * QK_DIM:(h + 1) * QK_DIM]
        qt = jnp.concatenate([qh[:NOPE_DIM], _rope_t(qh[NOPE_DIM:], cos_t, sin_t)], axis=0) * qscale
        qat_ref[h, 0] = _pad_rows(qt, LANES).astype(BF16)
        kt = jnp.concatenate([knt_all[h * NOPE_DIM:(h + 1) * NOPE_DIM], krt], axis=0)
        ka_ref[h] = _pad_rows(kt, LANES).T.astype(BF16)
        vat_ref[h, 0] = _with_ones_row(_pad_rows(vt_all[h * V_DIM:(h + 1) * V_DIM], V_ROWS)).astype(BF16)


def _inproj(h, g, rope_t, wgate, wtr, qngt, kvngt, wqt, wkt, wvt, seq):
    m = h.shape[0]
    nt = m // TOK_TILE
    per_seq = seq // TOK_TILE
    per_group = MLA_KEY_GROUP // TOK_TILE
    tok = lambda w: pl.BlockSpec((TOK_TILE, w), lambda i: (i, 0))
    heads = lambda n: pl.BlockSpec((n, TOK_TILE, LANES), lambda i: (0, i, 0))
    heads_t = lambda n: pl.BlockSpec((n, LANES, TOK_TILE), lambda i: (0, 0, i))
    out_shape = (
        jax.ShapeDtypeStruct((HA, nt, LANES, TOK_TILE), BF16),
        jax.ShapeDtypeStruct((HA, m, LANES), BF16),
        jax.ShapeDtypeStruct((HA, m // MLA_KEY_GROUP, V_ROWS, MLA_KEY_GROUP), BF16),
        jax.ShapeDtypeStruct((HB, LANES, m), BF16),
        jax.ShapeDtypeStruct((KVH, m, LANES), BF16),
        jax.ShapeDtypeStruct((KVH, V_ROWS, m), BF16),
        jax.ShapeDtypeStruct((m, D_MODEL), BF16),
        jax.ShapeDtypeStruct((m, D_MODEL), BF16),
    )
    out_specs = (
        pl.BlockSpec((HA, 1, LANES, TOK_TILE), lambda i: (0, i, 0, 0)),
        heads(HA),
        pl.BlockSpec((HA, 1, V_ROWS, TOK_TILE), lambda i: (0, i // per_group, 0, i % per_group)),
        heads_t(HB),
        heads(KVH),
        pl.BlockSpec((KVH, V_ROWS, TOK_TILE), lambda i: (0, 0, i)),
        tok(D_MODEL),
        tok(D_MODEL),
    )
    return pl.pallas_call(
        _inproj_kernel,
        grid=(nt,),
        in_specs=[tok(D_MODEL), _const_spec((1, D_MODEL)),
                  pl.BlockSpec((ROPE_DIM, TOK_TILE), lambda i: (0, i % per_seq)),
                  _const_spec(wgate.shape), _const_spec(wtr.shape),
                  _const_spec((Q_LORA, 1)), _const_spec((KV_LORA, 1)),
                  _const_spec(wqt.shape), _const_spec(wkt.shape), _const_spec(wvt.shape)],
        out_specs=out_specs,
        out_shape=out_shape,
        compiler_params=_params(1),
        name="inproj",
    )(h, g, rope_t, wgate, wtr, qngt, kvngt, wqt, wkt, wvt)


def _mla_kernel(qt_ref, qn_ref, k_ref, vt_ref, o_ref, s_ref, m_ref, q_all, o_acc, *,
                n_groups, q_tiles, heads):
    gk, tq, ck, half = MLA_KEY_GROUP, MLA_Q_TILE, MLA_KEY_GROUP // MLA_SUB, MLA_SUB // 2
    n_items = q_tiles * heads * n_groups

    for qi in range(q_tiles):
        for j in range(heads):
            q_all[qi * heads + j] = qt_ref[j, qi]
    for j in range(2):
        q_all[heads * q_tiles + j] = qn_ref[j, 0]

    def produce(slot, n, c):
        head, group = (n // n_groups) % heads, n % n_groups
        start = pl.multiple_of(group * gk + c * ck, ck)
        s = _dot(k_ref[head, pl.ds(start, ck), :], q_all[n // n_groups])
        s_ref[slot, c * ck:(c + 1) * ck, :] = s
        return jnp.max(s, axis=0, keepdims=True)

    def produce_half(slot, n, which):
        mh = None
        for c in range(which * half, (which + 1) * half):
            mc = produce(slot, n, c)
            mh = mc if mh is None else jnp.maximum(mh, mc)
        m_ref[slot, which] = mh

    def step(slot, n, carry):
        head, group = (n // n_groups) % heads, n % n_groups
        m_old, acc = carry
        m_old = jnp.where(group == 0, -jnp.inf, m_old)
        m_new = jnp.maximum(m_old, jnp.maximum(m_ref[slot, 0], m_ref[slot, 1]))
        pv = mh = None
        for c in range(MLA_SUB):
            mc = produce(1 - slot, n + 1, c + half) if c < half else produce(slot, n + 2, c - half)
            mh = mc if c % half == 0 else jnp.maximum(mh, mc)
            rows = slice(c * ck, (c + 1) * ck)
            p = jnp.exp2(s_ref[slot, rows, :] - m_new).astype(BF16)
            d = _dot(vt_ref[head, group, :, rows], p)
            pv = d if pv is None else pv + d
            if c == half - 1:
                m_ref[1 - slot, 1] = mh
        m_ref[slot, 0] = mh
        acc = jnp.exp2(m_old - m_new) * acc + pv
        o_acc[n // (heads * n_groups), pl.ds(pl.multiple_of(head * V_DIM, V_DIM), V_DIM), :] = (
            acc[:V_DIM] / acc[V_DIM:V_DIM + 1])
        return m_new, acc

    @pl.when(pl.program_id(2) == 0)
    def _():
        produce_half(0, 0, 0)
        produce_half(0, 0, 1)
        produce_half(1, 1, 0)

    unroll = min(MLA_UNROLL, n_items // 2)

    def body(t, carry):
        for u in range(unroll):
            carry = step(u % 2, unroll * t + u, carry)
        return carry

    lax.fori_loop(0, n_items // unroll, body,
                  (jnp.full((1, tq), -jnp.inf, F32), jnp.zeros((V_ROWS, tq), F32)))
    for qi in range(q_tiles):
        o_ref[qi * tq:(qi + 1) * tq, :] = o_acc[qi].T.astype(BF16)


def _mla(qat, ka, vat, batch, seq):
    m = ka.shape[1]
    n_groups = seq // MLA_KEY_GROUP
    hs = 2 if n_groups > 1 else MLA_HEADS_SHORT
    q_tiles = min(max(1, MLA_ITEMS_PER_STEP // (hs * n_groups)), seq // MLA_Q_TILE)
    tqs = q_tiles * MLA_Q_TILE
    nq = seq // tqs
    last_tile = m // MLA_Q_TILE - 1
    return pl.pallas_call(
        functools.partial(_mla_kernel, n_groups=n_groups, q_tiles=q_tiles, heads=hs),
        grid=(batch, HA // hs, nq),
        in_specs=[
            pl.BlockSpec((hs, q_tiles, LANES, MLA_Q_TILE), lambda b, hp, i: (hp, b * nq + i, 0, 0)),
            pl.BlockSpec((2, 1, LANES, MLA_Q_TILE),
                         lambda b, hp, i: (hp * (hs // 2),
                                           jnp.minimum((b * nq + i + 1) * q_tiles, last_tile), 0, 0)),
            pl.BlockSpec((hs, seq, LANES), lambda b, hp, i: (hp, b, 0)),
            pl.BlockSpec((hs, n_groups, V_ROWS, MLA_KEY_GROUP), lambda b, hp, i: (hp, b, 0, 0)),
        ],
        out_specs=pl.BlockSpec((tqs, hs * V_DIM), lambda b, hp, i: (b * nq + i, hp)),
        out_shape=jax.ShapeDtypeStruct((m, HA * V_DIM), BF16),
        scratch_shapes=[pltpu.VMEM((2, MLA_KEY_GROUP, MLA_Q_TILE), F32),
                        pltpu.VMEM((2, 2, 1, MLA_Q_TILE), F32),
                        pltpu.VMEM((hs * q_tiles + 2, LANES, MLA_Q_TILE), BF16),
                        pltpu.VMEM((q_tiles, hs * V_DIM, MLA_Q_TILE), F32)],
        compiler_params=_params(3),
        name="mla",
    )(qat, qat, ka, vat)


def _bias_kernel(bucket_ref, relb_ref, o_ref):
    bucket = bucket_ref[...]
    key = lax.broadcasted_iota(jnp.int32, bucket.shape, 0)
    tq = WIN_Q_TILE
    for h in range(HB):
        acc = jnp.full(bucket.shape, NEG, F32)
        for b in range(REL_BUCKETS):
            acc = jnp.where(bucket == b, relb_ref[b * HB + h] * LOG2E, acc)
        g, cols = h // REP, slice((h % REP) * tq, (h % REP + 1) * tq)
        o_ref[0, g, :, cols] = jnp.where(key < WINDOW, NEG, acc)
        o_ref[1, g, :, cols] = acc
        o_ref[2, g, :, cols] = jnp.where(key >= WINDOW + tq, NEG, acc)


def _bias_table(rel_bias):
    c = jnp.arange(2 * WIN_Q_TILE, dtype=jnp.int32)[:, None]
    r = jnp.arange(WIN_Q_TILE, dtype=jnp.int32)[None, :]
    rel = c - WINDOW - r
    nb = REL_BUCKETS // 2
    max_exact = nb // 2
    ret = jnp.where(rel > 0, nb, 0)
    n = jnp.abs(rel)
    nf = jnp.maximum(n, 1).astype(F32)
    large = max_exact + (jnp.log(nf / max_exact) / math.log(REL_MAX_DIST / max_exact)
                         * (nb - max_exact)).astype(jnp.int32)
    large = jnp.minimum(large, nb - 1)
    bucket = ret + jnp.where(n < max_exact, n, large)
    bucket = jnp.where(n <= WINDOW, bucket, -1)
    return pl.pallas_call(
        _bias_kernel,
        in_specs=[pl.BlockSpec(memory_space=pltpu.VMEM), pl.BlockSpec(memory_space=pltpu.SMEM)],
        out_specs=pl.BlockSpec(memory_space=pltpu.VMEM),
        out_shape=jax.ShapeDtypeStruct((3, KVH, 2 * WIN_Q_TILE, REP * WIN_Q_TILE), F32),
        name="rel_bias_table",
    )(bucket, rel_bias.reshape(-1))


def _win_kernel(sink_ref, qt_ref, bias_lo_ref, bias_mid_ref, bias_hi_ref, kp_ref, kc_ref, kn_ref,
                vtp_ref, vtc_ref, vtn_ref, o_ref, s_ref):
    tq, edge, tiles = WIN_Q_TILE, WIN_STEP - WINDOW, WIN_STEP // WIN_Q_TILE
    chains = [(g, t) for g in range(KVH) for t in range(tiles)]
    bias_refs = (bias_lo_ref,) + (bias_mid_ref,) * (tiles - 2) + (bias_hi_ref,)

    def keys(t):
        return slice(t * tq, t * tq + 2 * tq)

    def scores(slot, g, t):
        k_span = jnp.concatenate([kp_ref[g][edge:], kc_ref[g], kn_ref[g][:WINDOW]], axis=0)
        qt = jnp.concatenate([qt_ref[g * REP + r][:, t * tq:(t + 1) * tq] for r in range(REP)],
                             axis=1)
        s_ref[slot] = _dot(k_span[keys(t)], qt) + bias_refs[t][0, g]

    def consume(slot, g, t):
        s = s_ref[slot]
        sink = jnp.concatenate(
            [jnp.full((1, tq), sink_ref[g * REP + r] * LOG2E, F32) for r in range(REP)], axis=1)
        m = jnp.maximum(jnp.max(s, axis=0, keepdims=True), sink)
        p = jnp.exp2(s - m).astype(BF16)
        sink_p = jnp.exp2(sink - m)
        vt_span = jnp.concatenate([vtp_ref[g][:, edge:], vtc_ref[g], vtn_ref[g][:, :WINDOW]], axis=1)
        halves = []
        for r in range(REP):
            cols = slice(r * tq, (r + 1) * tq)
            pv = _dot(vt_span[:, keys(t)], p[:, cols])
            halves.append(pv[:HD] / (pv[V_DIM:V_DIM + 1] + sink_p[:, cols]))
        o_ref[t * tq:(t + 1) * tq, g * REP * HD:(g + 1) * REP * HD] = (
            jnp.concatenate(halves, axis=0).T.astype(BF16))

    scores(0, *chains[0])
    for n, chain in enumerate(chains):
        if n + 1 < len(chains):
            scores((n + 1) % 2, *chains[n + 1])
        consume(n % 2, *chain)


def _win(sink, qbt, bias, kb, vbt, batch, seq):
    m = kb.shape[1]
    ts = WIN_STEP
    nt = seq // ts

    def neighbour(i, off):
        return jnp.clip(i + off, 0, nt - 1)

    def k_spec(off):
        return pl.BlockSpec((KVH, ts, LANES), lambda b, i: (0, b * nt + neighbour(i, off), 0))

    def vt_spec(off):
        return pl.BlockSpec((KVH, V_ROWS, ts), lambda b, i: (0, 0, b * nt + neighbour(i, off)))

    bias_block = (1, KVH, 2 * WIN_Q_TILE, REP * WIN_Q_TILE)
    return pl.pallas_call(
        _win_kernel,
        grid=(batch, nt),
        in_specs=[
            pl.BlockSpec(memory_space=pltpu.SMEM),
            pl.BlockSpec((HB, LANES, ts), lambda b, i: (0, 0, b * nt + i)),
            pl.BlockSpec(bias_block, lambda b, i: (jnp.where(i == 0, 0, 1), 0, 0, 0)),
            pl.BlockSpec(bias_block, lambda b, i: (1, 0, 0, 0)),
            pl.BlockSpec(bias_block, lambda b, i: (jnp.where(i == nt - 1, 2, 1), 0, 0, 0)),
            k_spec(-1), k_spec(0), k_spec(1),
            vt_spec(-1), vt_spec(0), vt_spec(1),
        ],
        out_specs=pl.BlockSpec((ts, HB * HD), lambda b, i: (b * nt + i, 0)),
        out_shape=jax.ShapeDtypeStruct((m, HB * HD), BF16),
        scratch_shapes=[pltpu.VMEM((2, 2 * WIN_Q_TILE, REP * WIN_Q_TILE), F32)],
        compiler_params=_params(2),
        name="window",
    )(sink, qbt, bias, bias, bias, kb, kb, kb, vbt, vbt, vbt)


def _tail_kernel(h_ref, ya_ref, yb_ref, ga_ref, gb_ref, p_ref, gains_ref,
                 pa_ref, pb_ref, wo_ref, w1_ref, w3_ref, w2_ref, wg_ref, wp_ref, o_ref):
    g_mix, g_pre2, g_post2, g_pre_e, g_post_e = (gains_ref[k:k + 1] for k in range(5))
    halves = [slice(k * TOK_TILE // 2, (k + 1) * TOK_TILE // 2) for k in range(2)]
    heads = [(_dot(ya_ref[r], pa_ref[...]), _dot(yb_ref[r], pb_ref[...])) for r in halves]
    outs = []
    for r, (ma, mb) in zip(halves, heads):
        mix = (ga_ref[r].astype(F32) * ma + gb_ref[r].astype(F32) * mb).astype(BF16)
        outs.append(_dot(mix, wo_ref[...]))
    h = jnp.concatenate([h_ref[r] + _rms(o, g_mix) for r, o in zip(halves, outs)], axis=0)
    f = _swiglu(_rms(h, g_pre2).astype(BF16), w1_ref, w3_ref, w2_ref)
    for r in halves:
        h2 = h[r] + 0.5 * _rms(f[r], g_post2)
        gate = jax.nn.sigmoid(_dot(_rms(h2, g_pre_e).astype(BF16), wg_ref[...]))
        e = _dot(p_ref[r].astype(BF16), wp_ref[...]) * gate
        o_ref[r] = h2 + _rms(e, g_post_e)


def _tail(h, ya, yb, ga, gb, p, gains, pa, pb, wo, w1, w3, w2, wg, wp):
    m = h.shape[0]
    tok = lambda w: pl.BlockSpec((TOK_TILE, w), lambda i: (i, 0))
    resident = lambda a: pl.BlockSpec(a.shape, lambda i: (0,) * a.ndim, pipeline_mode=pl.Buffered(1))
    weights = (pa, pb, wo, w1, w3, w2, wg, wp)
    return pl.pallas_call(
        _tail_kernel,
        grid=(m // TOK_TILE,),
        in_specs=[tok(D_MODEL), tok(HA * V_DIM), tok(HB * HD), tok(D_MODEL), tok(D_MODEL),
                  tok(PLE_DIM), _const_spec(gains.shape)] + [resident(a) for a in weights],
        out_specs=tok(D_MODEL),
        out_shape=jax.ShapeDtypeStruct((m, D_MODEL), F32),
        compiler_params=_params(1),
        name="tail",
    )(h, ya, yb, ga, gb, p, gains, *weights)


def _prep_weights(w_in, w_uq, w_uk, w_uv):
    splits = (Q_LORA, KV_LORA, ROPE_DIM, HB * HD, KVH * HD, KVH * HD, D_MODEL, D_MODEL)
    offs = [0]
    for s in splits:
        offs.append(offs[-1] + s)
    w_cq, w_ckv, w_kr, w_qb, w_kb, w_vb, w_ga, w_gb = (
        w_in[:, offs[k]:offs[k + 1]] for k in range(len(splits)))
    wgate = jnp.concatenate([w_ga, w_gb], axis=1).astype(BF16)
    wtr = jnp.concatenate([w_cq, w_ckv, w_kr, w_qb * HD ** -0.5, w_vb, w_kb],
                          axis=1).T.astype(BF16)
    return wgate, wtr, w_uq.T.astype(BF16), w_uk.T.astype(BF16), w_uv.T.astype(BF16)


def _rope_table_t(seq):
    inv = 1.0 / (ROPE_THETA ** (jnp.arange(0, ROPE_DIM, 2, dtype=F32) / ROPE_DIM))
    ang = inv[:, None] * jnp.arange(seq, dtype=F32)[None, :]
    return jnp.concatenate([jnp.cos(ang), jnp.sin(ang)], axis=0)


def _encoder_layer(x, p, bias, sink, wts):
    batch, seq, _ = x.shape
    m = batch * seq
    x2 = x.reshape(m, D_MODEL)
    p2 = p.reshape(m, PLE_DIM)
    h = _ffn(x2, wts["ffn1_pre_g"], wts["ffn1_post_g"], wts["ffn1_w1"], wts["ffn1_w3"], wts["ffn1_w2"])
    qat, ka, vat, qbt, kb, vbt, ga, gb = _inproj(
        h, wts["mix_pre_g"], _rope_table_t(seq), wts["wgate"], wts["wtr"],
        wts["q_norm_gt"], wts["kv_norm_gt"], wts["wqt"], wts["wkt"], wts["wvt"], seq)
    ya = _mla(qat, ka, vat, batch, seq)
    yb = _win(sink, qbt, bias, kb, vbt, batch, seq)
    gains = jnp.concatenate([wts["mix_post_g"], wts["ffn2_pre_g"], wts["ffn2_post_g"],
                             wts["ple_pre_g"], wts["ple_post_g"]], axis=0)
    out = _tail(h, ya, yb, ga, gb, p2, gains, wts["w_proj_a"], wts["w_proj_b"], wts["w_out"],
                wts["ffn2_w1"], wts["ffn2_w3"], wts["ffn2_w2"], wts["w_ple_gate"], wts["w_ple_proj"])
    return out.reshape(batch, seq, D_MODEL)


def _layer_weights(layer, w):
    wgate, wtr, wqt, wkt, wvt = _prep_weights(w["w_in"][layer], w["w_uq"][layer], w["w_uk"][layer],
                                              w["w_uv"][layer])
    wts = {
        "wgate": wgate, "wtr": wtr, "wqt": wqt, "wkt": wkt, "wvt": wvt,
        "q_norm_gt": w["q_norm_g"][layer][:, None], "kv_norm_gt": w["kv_norm_g"][layer][:, None],
    }
    for name in ("ffn1_pre_g", "ffn1_post_g", "mix_pre_g", "mix_post_g",
                 "ffn2_pre_g", "ffn2_post_g", "ple_pre_g", "ple_post_g"):
        wts[name] = w[name][layer][None]
    for name in ("ffn1_w1", "ffn1_w3", "ffn1_w2", "w_proj_a", "w_proj_b", "w_out",
                 "ffn2_w1", "ffn2_w3", "ffn2_w2", "w_ple_gate", "w_ple_proj"):
        wts[name] = w[name][layer].astype(BF16)
    return wts


def kernel(x_prompt, x_sample, p_prompt, p_sample, rel_bias, ffn1_pre_g, ffn1_post_g, ffn1_w1, ffn1_w3, ffn1_w2, mix_pre_g, mix_post_g, w_in, q_norm_g, kv_norm_g, w_uq, w_uk, w_uv, sink, w_proj_a, w_proj_b, w_out, ffn2_pre_g, ffn2_post_g, ffn2_w1, ffn2_w3, ffn2_w2, ple_pre_g, ple_post_g, w_ple_gate, w_ple_proj):
    w = dict(ffn1_pre_g=ffn1_pre_g, ffn1_post_g=ffn1_post_g, ffn1_w1=ffn1_w1, ffn1_w3=ffn1_w3,
             ffn1_w2=ffn1_w2, mix_pre_g=mix_pre_g, mix_post_g=mix_post_g, w_in=w_in,
             q_norm_g=q_norm_g, kv_norm_g=kv_norm_g, w_uq=w_uq, w_uk=w_uk, w_uv=w_uv,
             w_proj_a=w_proj_a, w_proj_b=w_proj_b, w_out=w_out, ffn2_pre_g=ffn2_pre_g,
             ffn2_post_g=ffn2_post_g, ffn2_w1=ffn2_w1, ffn2_w3=ffn2_w3, ffn2_w2=ffn2_w2,
             ple_pre_g=ple_pre_g, ple_post_g=ple_post_g, w_ple_gate=w_ple_gate,
             w_ple_proj=w_ple_proj)
    bias = _bias_table(rel_bias)
    y_prompt, y_sample = x_prompt, x_sample
    for layer in range(ffn1_w1.shape[0]):
        wts = _layer_weights(layer, w)
        y_prompt = _encoder_layer(y_prompt, p_prompt[layer], bias, sink[layer], wts)
        y_sample = _encoder_layer(y_sample, p_sample[layer], bias, sink[layer], wts)
    return (y_prompt, y_sample)
```
